```python
import jax, jax.numpy as jnp
from jax import lax
import numpy as np

D_MODEL = 2048
BATCH = 16
SEQ = 2048
DEPTH = 1

CHUNK = 64
Q_BLOCK = 128
N_MEM = 256
EPS = 1e-6

D_FF = 5504

MLA_HEADS = 8
QK_NOPE = 128
QK_ROPE = 64
V_HEAD = 128
Q_LORA = 512
KV_LORA = 256
ROPE_THETA = 10000.0

RWKV_HEADS = 16
RWKV_HEAD = 64
RWKV_WIDTH = RWKV_HEADS * RWKV_HEAD
DECAY_LORA = 64
A_LORA = 64
GATE_LORA = 128
LNX_EPS = 64e-5

MEM_HEADS = 4
MEM_HEAD = 256

N_BRANCH = 2
MLA_COLS = Q_LORA + KV_LORA + QK_ROPE
RWKV_COLS = 3 * RWKV_WIDTH + DECAY_LORA + A_LORA + GATE_LORA
GATE_COLS = N_BRANCH * D_MODEL
IN_COLS = MLA_COLS + RWKV_COLS + GATE_COLS

kernel_name = 'hybrid_mla_rwkv7_gated_macaron_layer'


def rmsnorm(x, g, eps=EPS):
    xf = x.astype(jnp.float32)
    y = xf * lax.rsqrt(jnp.mean(xf * xf, axis=-1, keepdims=True) + eps)
    return (y * g.astype(jnp.float32)).astype(x.dtype)


def apply_rope(x, positions):
    half = x.shape[-1] // 2
    inv = ROPE_THETA ** (-jnp.arange(half, dtype=jnp.float32) / half)
    ang = positions.astype(jnp.float32)[..., None] * inv
    if x.ndim == 4:
        ang = ang[:, :, None, :]
    cos, sin = jnp.cos(ang), jnp.sin(ang)
    xf = x.astype(jnp.float32)
    x1, x2 = xf[..., :half], xf[..., half:]
    return jnp.concatenate([x1 * cos - x2 * sin, x1 * sin + x2 * cos], axis=-1).astype(x.dtype)


def swiglu(h, w_gate, w_up, w_down):
    return (jax.nn.silu(h @ w_gate) * (h @ w_up)) @ w_down


def mla_branch(c_q, c_kv, k_rope, positions, n_q_lat, w_uq, n_kv_lat, w_ukv, w_oa):
    B, S, _ = c_q.shape
    q = (rmsnorm(c_q, n_q_lat) @ w_uq).reshape(B, S, MLA_HEADS, QK_NOPE + QK_ROPE)
    q_nope = q[..., :QK_NOPE]
    q_rope = apply_rope(q[..., QK_NOPE:], positions)
    kv = (rmsnorm(c_kv, n_kv_lat) @ w_ukv).reshape(B, S, MLA_HEADS, QK_NOPE + V_HEAD)
    k_nope, v = kv[..., :QK_NOPE], kv[..., QK_NOPE:]
    k_r = apply_rope(k_rope, positions)
    scale = (QK_NOPE + QK_ROPE) ** -0.5
    outs = []
    for qb in range(S // Q_BLOCK):
        qs, qe = qb * Q_BLOCK, (qb + 1) * Q_BLOCK
        s = (jnp.einsum('bqhd,bkhd->bhqk', q_nope[:, qs:qe], k_nope[:, :qe])
             + jnp.einsum('bqhd,bkd->bhqk', q_rope[:, qs:qe], k_r[:, :qe]))
        s = s.astype(jnp.float32) * scale
        q_chunk = (qs + jnp.arange(Q_BLOCK)) // CHUNK
        k_chunk = jnp.arange(qe) // CHUNK
        mask = k_chunk[None, :] <= q_chunk[:, None]
        s = jnp.where(mask[None, None], s, -jnp.inf)
        p = jax.nn.softmax(s, axis=-1).astype(v.dtype)
        outs.append(jnp.einsum('bhqk,bkhd->bqhd', p, v[:, :qe]))
    o = jnp.concatenate(outs, axis=1).reshape(B, S, MLA_HEADS * V_HEAD)
    return o @ w_oa


def token_shift(y, mu):
    y_prev = jnp.pad(y, ((0, 0), (1, 0), (0, 0)))[:, :-1]
    return y + (y_prev - y) * mu


def rwkv7_branch(proj, mu_shift, w0, w_w2, a0, w_a2, w_g2, k_k, k_a, r_k, lnx_w, lnx_b, w_ob):
    B, S, _ = proj.shape
    C, H, N = RWKV_WIDTH, RWKV_HEADS, RWKV_HEAD
    p = token_shift(proj, mu_shift)
    r, k, v, dw, da, dg = jnp.split(
        p, [C, 2 * C, 3 * C, 3 * C + DECAY_LORA, 3 * C + DECAY_LORA + A_LORA], axis=-1)
    w = -jax.nn.softplus(-(w0 + jnp.tanh(dw) @ w_w2)) - 0.5
    decay = jnp.exp(-jnp.exp(w.astype(jnp.float32)))
    a = jax.nn.sigmoid(a0 + da @ w_a2)
    g = jax.nn.sigmoid(dg) @ w_g2
    kk = (k * k_k).reshape(B, S, H, N).astype(jnp.float32)
    kk = kk / jnp.maximum(jnp.sqrt(jnp.sum(kk * kk, axis=-1, keepdims=True)), 1e-12)
    k = k * (1.0 + (a - 1.0) * k_a)
    heads = lambda t: t.reshape(B, S, H, N).astype(jnp.float32)
    rh, kh, vh, ah, wh = heads(r), heads(k), heads(v), heads(a), heads(decay)
    a_in = -kk
    b_in = kk * ah
    xs = tuple(jnp.moveaxis(t, 1, 0) for t in (rh, wh, kh, vh, a_in, b_in))

    def step(state, inp):
        r_t, w_t, k_t, v_t, a_t, b_t = inp
        sa = jnp.einsum('bhvk,bhk->bhv', state, a_t)
        state = (state * w_t[:, :, None, :] + sa[..., None] * b_t[:, :, None, :]
                 + v_t[..., None] * k_t[:, :, None, :])
        return state, jnp.einsum('bhvk,bhk->bhv', state, r_t)

    s0 = jnp.zeros((B, H, N, N), jnp.float32)
    _, o = lax.scan(step, s0, xs)
    o = jnp.moveaxis(o, 0, 1)
    mean = jnp.mean(o, axis=-1, keepdims=True)
    var = jnp.mean(jnp.square(o - mean), axis=-1, keepdims=True)
    o = ((o - mean) * lax.rsqrt(var + LNX_EPS)).reshape(B, S, C)
    o = o * lnx_w.astype(jnp.float32) + lnx_b.astype(jnp.float32)
    bonus = jnp.sum(rh * kh * r_k.astype(jnp.float32), axis=-1, keepdims=True) * vh
    o = (o + bonus.reshape(B, S, C)).astype(proj.dtype)
    return (o * g) @ w_ob


def memory_xattn(h, mem, n_mem, w_cq, w_ckv, w_co):
    B, S, _ = h.shape
    M = mem.shape[1]
    q = (h @ w_cq).reshape(B, S, MEM_HEADS, MEM_HEAD)
    kv = (rmsnorm(mem, n_mem) @ w_ckv).reshape(B, M, MEM_HEADS, 2 * MEM_HEAD)
    k, v = kv[..., :MEM_HEAD], kv[..., MEM_HEAD:]
    s = jnp.einsum('bqhd,bmhd->bhqm', q, k).astype(jnp.float32) * (MEM_HEAD ** -0.5)
    p = jax.nn.softmax(s, axis=-1).astype(v.dtype)
    o = jnp.einsum('bhqm,bmhd->bqhd', p, v).reshape(B, S, MEM_HEADS * MEM_HEAD)
    return o @ w_co


def setup_inputs(seed: int = 0) -> dict:
    key = jax.random.key(seed)
    ks = iter(jax.random.split(key, 64))
    f32 = jnp.float32
    L, D = DEPTH, D_MODEL

    def nrm(shape, scale):
        return jax.random.normal(next(ks), shape, f32) * scale

    def gain(shape):
        return 1.0 + nrm(shape, 0.05)

    def dense(shape):
        return nrm(shape, shape[-2] ** -0.5)

    x = nrm((BATCH, SEQ, D), 1.0)
    mem = nrm((BATCH, N_MEM, D), 1.0)
    offset = jax.random.randint(next(ks), (BATCH, 1), 0, 4096, jnp.int32)
    positions = offset + jnp.arange(SEQ, dtype=jnp.int32)[None, :]
    return {
        'x': x, 'mem': mem, 'positions': positions,
        'n_ffn1_pre': gain((L, D)), 'n_ffn1_post': gain((L, D)),
        'w_ffn1_gate': dense((L, D, D_FF)), 'w_ffn1_up': dense((L, D, D_FF)),
        'w_ffn1_down': dense((L, D_FF, D)),
        'n_mix_pre': gain((L, D)), 'n_mix_post': gain((L, D)),
        'w_in': dense((L, D, IN_COLS)), 'b_gate': nrm((L, GATE_COLS), 0.02),
        'n_q_lat': gain((L, Q_LORA)), 'w_uq': dense((L, Q_LORA, MLA_HEADS * (QK_NOPE + QK_ROPE))),
        'n_kv_lat': gain((L, KV_LORA)), 'w_ukv': dense((L, KV_LORA, MLA_HEADS * (QK_NOPE + V_HEAD))),
        'w_oa': dense((L, MLA_HEADS * V_HEAD, D)),
        'mu_shift': jax.random.uniform(next(ks), (L, RWKV_COLS), f32),
        'w0': jax.random.uniform(next(ks), (L, RWKV_WIDTH), f32, -5.0, 0.0),
        'w_w2': nrm((L, DECAY_LORA, RWKV_WIDTH), 0.1 * DECAY_LORA ** -0.5),
        'a0': nrm((L, RWKV_WIDTH), 0.1),
        'w_a2': nrm((L, A_LORA, RWKV_WIDTH), 0.5 * A_LORA ** -0.5),
        'w_g2': dense((L, GATE_LORA, RWKV_WIDTH)),
        'k_k': 0.85 + nrm((L, RWKV_WIDTH), 0.05),
        'k_a': gain((L, RWKV_WIDTH)),
        'r_k': nrm((L, RWKV_HEADS, RWKV_HEAD), 0.1),
        'lnx_w': gain((L, RWKV_WIDTH)), 'lnx_b': nrm((L, RWKV_WIDTH), 0.02),
        'w_ob': dense((L, RWKV_WIDTH, D)),
        'w_o': dense((L, D, D)),
        'n_x_pre': gain((L, D)), 'n_x_post': gain((L, D)), 'n_mem': gain((L, D)),
        'w_cq': dense((L, D, MEM_HEADS * MEM_HEAD)),
        'w_ckv': dense((L, D, 2 * MEM_HEADS * MEM_HEAD)),
        'w_co': dense((L, MEM_HEADS * MEM_HEAD, D)),
        'n_ffn2_pre': gain((L, D)), 'n_ffn2_post': gain((L, D)),
        'w_ffn2_gate': dense((L, D, D_FF)), 'w_ffn2_up': dense((L, D, D_FF)),
        'w_ffn2_down': dense((L, D_FF, D)),
    }


def reference(x, mem, positions,
              n_ffn1_pre, n_ffn1_post, w_ffn1_gate, w_ffn1_up, w_ffn1_down,
              n_mix_pre, n_mix_post, w_in, b_gate,
              n_q_lat, w_uq, n_kv_lat, w_ukv, w_oa,
              mu_shift, w0, w_w2, a0, w_a2, w_g2, k_k, k_a, r_k, lnx_w, lnx_b, w_ob,
              w_o,
              n_x_pre, n_x_post, n_mem, w_cq, w_ckv, w_co,
              n_ffn2_pre, n_ffn2_post, w_ffn2_gate, w_ffn2_up, w_ffn2_down):
    B, S, _ = x.shape
    for l in range(DEPTH):
        h = rmsnorm(x, n_ffn1_pre[l])
        x = x + 0.5 * rmsnorm(swiglu(h, w_ffn1_gate[l], w_ffn1_up[l], w_ffn1_down[l]), n_ffn1_post[l])

        h = rmsnorm(x, n_mix_pre[l])
        proj = h @ w_in[l]
        c_q, c_kv, k_rope, rwkv_in, gate_logits = jnp.split(
            proj, [Q_LORA, Q_LORA + KV_LORA, MLA_COLS, MLA_COLS + RWKV_COLS], axis=-1)
        y_a = mla_branch(c_q, c_kv, k_rope, positions, n_q_lat[l], w_uq[l],
                         n_kv_lat[l], w_ukv[l], w_oa[l])
        y_b = rwkv7_branch(rwkv_in, mu_shift[l], w0[l], w_w2[l], a0[l], w_a2[l], w_g2[l],
                           k_k[l], k_a[l], r_k[l], lnx_w[l], lnx_b[l], w_ob[l])
        gates = jax.nn.sigmoid(gate_logits + b_gate[l]).reshape(B, S, N_BRANCH, D_MODEL)
        merged = gates[:, :, 0] * y_a + gates[:, :, 1] * y_b
        x = x + rmsnorm(merged @ w_o[l], n_mix_post[l])

        h = rmsnorm(x, n_x_pre[l])
        x = x + rmsnorm(memory_xattn(h, mem, n_mem[l], w_cq[l], w_ckv[l], w_co[l]), n_x_post[l])

        h = rmsnorm(x, n_ffn2_pre[l])
        x = x + 0.5 * rmsnorm(swiglu(h, w_ffn2_gate[l], w_ffn2_up[l], w_ffn2_down[l]), n_ffn2_post[l])
    return x
```

```python
import functools

import jax
import jax.numpy as jnp
from jax import lax
from jax.experimental import pallas as pl
from jax.experimental.pallas import tpu as pltpu

F32 = jnp.float32
BF16 = jnp.bfloat16

D_MODEL = 2048
D_FF = 5504
CHUNK = 64
EPS = 1e-6

MLA_HEADS = 8
QK_NOPE = 128
QK_ROPE = 64
V_HEAD = 128
Q_LORA = 512
KV_LORA = 256
ROPE_THETA = 10000.0

RWKV_HEADS = 16
RWKV_HEAD = 64
RWKV_WIDTH = RWKV_HEADS * RWKV_HEAD
DECAY_LORA = 64
A_LORA = 64
GATE_LORA = 128
LNX_EPS = 64e-5

MEM_HEADS = 4
MEM_HEAD = 256

MLA_COLS = Q_LORA + KV_LORA + QK_ROPE
RWKV_COLS = 3 * RWKV_WIDTH + DECAY_LORA + A_LORA + GATE_LORA

LANES = 128
MXU_DIM = 256
VMEM_LIMIT_BYTES = 56 * 1024 * 1024

COL_GATE_A = 0
COL_GATE_B = 2048
COL_R = 4096
COL_K = 5120
COL_V = 6144
COL_CQ = 7168
COL_CKV = 7680
COL_SMALL = 7936
COL_KROPE = 8192
IN_COLS_PADDED = 8448

NN = (((1,), (0,)), ((), ()))
NT = (((1,), (1,)), ((), ()))
TN = (((0,), (0,)), ((), ()))


def _dot(a, b, dims=NN):
    return lax.dot_general(a.astype(BF16), b.astype(BF16), dims, preferred_element_type=F32)


def _split2(x):
    hi = x.astype(BF16)
    lo = (x - hi.astype(F32)).astype(BF16)
    return hi, lo


def _dot_x3(a, b, dims=NN):
    ah, al = _split2(a)
    bh, bl = _split2(b)
    f = functools.partial(lax.dot_general, dimension_numbers=dims, preferred_element_type=F32)
    return f(ah, bh) + (f(ah, bl) + f(al, bh))


def _dot_exact_rhs(a, b_exact, dims=NN):
    ah, al = _split2(a)
    f = functools.partial(lax.dot_general, dimension_numbers=dims, preferred_element_type=F32)
    return f(ah, b_exact) + f(al, b_exact)


def _rms(xf, g, eps=EPS):
    return xf * lax.rsqrt(jnp.mean(xf * xf, axis=-1, keepdims=True) + eps) * g


def _sigmoid(x):
    return 1.0 / (1.0 + jnp.exp(-x))


def _params(*sem):
    return pltpu.CompilerParams(dimension_semantics=sem, vmem_limit_bytes=VMEM_LIMIT_BYTES)


def _norm_matmul_kernel(x_ref, g_ref, w_ref, o_ref, h_ref):
    @pl.when(pl.program_id(1) == 0)
    def _():
        h_ref[...] = _rms(x_ref[...].astype(F32), g_ref[...]).astype(BF16)

    o_ref[...] = jnp.dot(h_ref[...], w_ref[...], preferred_element_type=F32).astype(o_ref.dtype)


def _norm_matmul(x, col_block, k_dim, g, w, *, tm, tn, out_dtype=BF16):
    m = x.shape[0]
    n = w.shape[1]
    tm = min(tm, m)
    return pl.pallas_call(
        _norm_matmul_kernel,
        grid=(m // tm, n // tn),
        in_specs=[
            pl.BlockSpec((tm, k_dim), lambda i, j: (i, col_block)),
            pl.BlockSpec((1, k_dim), lambda i, j: (0, 0)),
            pl.BlockSpec((k_dim, tn), lambda i, j: (0, j)),
        ],
        out_specs=pl.BlockSpec((tm, tn), lambda i, j: (i, j)),
        out_shape=jax.ShapeDtypeStruct((m, n), out_dtype),
        scratch_shapes=[pltpu.VMEM((tm, k_dim), BF16)],
        compiler_params=_params("parallel", "arbitrary"),
        name="norm_matmul",
    )(x, g, w)


def _ffn_kernel(x_ref, gpre_ref, wg_ref, wu_ref, wd_ref, gpost_ref, o_ref, h_ref, acc_ref):
    j = pl.program_id(1)

    @pl.when(j == 0)
    def _():
        h_ref[...] = _rms(x_ref[...], gpre_ref[...]).astype(BF16)
        acc_ref[...] = jnp.zeros_like(acc_ref)

    h = h_ref[...]
    gate = jnp.dot(h, wg_ref[...], preferred_element_type=F32)
    up = jnp.dot(h, wu_ref[...], preferred_element_type=F32)
    act = (gate * _sigmoid(gate) * up).astype(BF16)
    acc_ref[...] += jnp.dot(act, wd_ref[...], preferred_element_type=F32)

    @pl.when(j == pl.num_programs(1) - 1)
    def _():
        o_ref[...] = x_ref[...] + 0.5 * _rms(acc_ref[...], gpost_ref[...])


def _ffn(x, g_pre, wg, wu, wd, g_post, *, tm=512, tf=512):
    m, d = x.shape
    fp = wg.shape[1]
    tm = min(tm, m)
    return pl.pallas_call(
        _ffn_kernel,
        grid=(m // tm, fp // tf),
        in_specs=[
            pl.BlockSpec((tm, d), lambda i, j: (i, 0)),
            pl.BlockSpec((1, d), lambda i, j: (0, 0)),
            pl.BlockSpec((d, tf), lambda i, j: (0, j)),
            pl.BlockSpec((d, tf), lambda i, j: (0, j)),
            pl.BlockSpec((tf, d), lambda i, j: (j, 0)),
            pl.BlockSpec((1, d), lambda i, j: (0, 0)),
        ],
        out_specs=pl.BlockSpec((tm, d), lambda i, j: (i, 0)),
        out_shape=jax.ShapeDtypeStruct((m, d), F32),
        scratch_shapes=[pltpu.VMEM((tm, d), BF16), pltpu.VMEM((tm, d), F32)],
        compiler_params=_params("parallel", "arbitrary"),
        name="ffn",
    )(x, g_pre, wg, wu, wd, g_post)


def _mla_attn_kernel(qn_ref, qr_ref, kn_ref, kr_ref, v_ref, o_ref, *, tq):
    qi = pl.program_id(2)
    qn = qn_ref[...]
    qr = qr_ref[0, 0]

    def scores(start):
        kn = kn_ref[pl.ds(start, tq), :]
        kr = kr_ref[0, pl.ds(start, tq), :]
        return (lax.dot_general(qn, kn, NT, preferred_element_type=F32)
                + lax.dot_general(qr, kr, NT, preferred_element_type=F32))

    def update(carry, s, start):
        m, l, acc = carry
        m_new = jnp.maximum(m, jnp.max(s, axis=-1, keepdims=True))
        alpha = jnp.exp(m - m_new)
        p = jnp.exp(s - m_new)
        l = alpha * l + jnp.sum(p, axis=-1, keepdims=True)
        acc = alpha * acc + jnp.dot(p.astype(BF16), v_ref[pl.ds(start, tq), :],
                                    preferred_element_type=F32)
        return m_new, l, acc

    def body(kj, carry):
        start = pl.multiple_of(kj * tq, tq)
        return update(carry, scores(start), start)

    init = (jnp.full((tq, 1), -1e30, F32), jnp.zeros((tq, 1), F32), jnp.zeros((tq, V_HEAD), F32))
    carry = lax.fori_loop(0, qi, body, init)

    start = pl.multiple_of(qi * tq, tq)
    s = scores(start)
    q_chunk = lax.broadcasted_iota(jnp.int32, (tq, tq), 0) // CHUNK
    k_chunk = lax.broadcasted_iota(jnp.int32, (tq, tq), 1) // CHUNK
    s = jnp.where(k_chunk <= q_chunk, s, -1e30)
    m, l, acc = update(carry, s, start)
    o_ref[...] = (acc / l).astype(o_ref.dtype)


def _mla_attention(q_up, qr, kv_up, kr, *, batch, seq, tq=256):
    nq = seq // tq
    return pl.pallas_call(
        functools.partial(_mla_attn_kernel, tq=tq),
        grid=(batch, MLA_HEADS, nq),
        in_specs=[
            pl.BlockSpec((tq, QK_NOPE), lambda b, h, i: (b * nq + i, h)),
            pl.BlockSpec((1, 1, tq, QK_ROPE), lambda b, h, i: (b, h, i, 0)),
            pl.BlockSpec((seq, QK_NOPE), lambda b, h, i: (b, 2 * h)),
            pl.BlockSpec((1, seq, QK_ROPE), lambda b, h, i: (b, 0, 0)),
            pl.BlockSpec((seq, V_HEAD), lambda b, h, i: (b, 2 * h + 1)),
        ],
        out_specs=pl.BlockSpec((tq, V_HEAD), lambda b, h, i: (b * nq + i, h)),
        out_shape=jax.ShapeDtypeStruct((batch * seq, MLA_HEADS * V_HEAD), BF16),
        compiler_params=_params("parallel", "parallel", "arbitrary"),
        name="mla_attention",
    )(q_up, qr, kv_up, kr, kv_up)


def _head_sum(x, e_ref, et_ref):
    s = _dot_exact_rhs(x, e_ref[...])
    return _dot_exact_rhs(s, et_ref[...])


def _shift(y, prev_row, mu, first):
    rows = lax.broadcasted_iota(jnp.int32, y.shape, 0)
    prev0 = jnp.where(first, 0.0, prev_row)
    y_prev = jnp.where(rows == 0, prev0, pltpu.roll(y, 1, 0))
    return y + (y_prev - y) * mu


def _rwkv_prep_kernel(r_ref, k_ref, v_ref, s_ref, rp_ref, kp_ref, vp_ref, sp_ref,
                      mu_rkv_ref, mu_s_ref, w0_ref, ww2_ref, a0_ref, wa2_ref, wg2_ref,
                      kk_ref, ka_ref, e_ref, et_ref,
                      ro_ref, lw_ref, ko_ref, vo_ref, ao_ref, bo_ref, go_ref, *, tiles_per_seq):
    first = (pl.program_id(0) % tiles_per_seq) == 0
    last8 = slice(7, 8)

    def shifted(ref, pref, mu):
        return _shift(ref[...].astype(F32), pref[last8, :].astype(F32), mu, first)

    r = shifted(r_ref, rp_ref, mu_rkv_ref[0:1, :])
    k = shifted(k_ref, kp_ref, mu_rkv_ref[1:2, :])
    v = shifted(v_ref, vp_ref, mu_rkv_ref[2:3, :])
    small = shifted(s_ref, sp_ref, mu_s_ref[...])
    dw = small[:, 0:DECAY_LORA]
    da = small[:, DECAY_LORA:DECAY_LORA + A_LORA]
    dg = small[:, DECAY_LORA + A_LORA:]

    z = -(w0_ref[...] + _dot(jnp.tanh(dw), ww2_ref[...]))
    softplus = jnp.maximum(z, 0.0) + jnp.log(1.0 + jnp.exp(-jnp.abs(z)))
    w = -softplus - 0.5
    lw_ref[...] = -jnp.exp(w)
    a = _sigmoid(a0_ref[...] + _dot(da, wa2_ref[...]))
    go_ref[...] = _dot(_sigmoid(dg), wg2_ref[...]).astype(go_ref.dtype)

    kk = k * kk_ref[...]
    norm = jnp.maximum(jnp.sqrt(_head_sum(kk * kk, e_ref, et_ref)), 1e-12)
    kk = kk / norm
    ro_ref[...] = r.astype(ro_ref.dtype)
    ko_ref[...] = (k * (1.0 + (a - 1.0) * ka_ref[...])).astype(ko_ref.dtype)
    vo_ref[...] = v.astype(vo_ref.dtype)
    ao_ref[...] = (-kk).astype(ao_ref.dtype)
    bo_ref[...] = (kk * a).astype(bo_ref.dtype)


def _rwkv_prep(proj, mu_rkv, mu_s, w0, ww2, a0, wa2, wg2, k_k, k_a, e, et, *, seq, tm=256):
    t = proj.shape[0]
    c = RWKV_WIDTH
    small_w = DECAY_LORA + A_LORA + GATE_LORA

    def cur(width, col):
        return pl.BlockSpec((tm, width), lambda i: (i, col // width))

    def prev(width, col):
        return pl.BlockSpec((8, width), lambda i: (jnp.maximum(i * (tm // 8) - 1, 0), col // width))

    def const(shape):
        return pl.BlockSpec(shape, lambda i: (0, 0))

    out = jax.ShapeDtypeStruct((t, c), BF16)
    return pl.pallas_call(
        functools.partial(_rwkv_prep_kernel, tiles_per_seq=seq // tm),
        grid=(t // tm,),
        in_specs=[cur(c, COL_R), cur(c, COL_K), cur(c, COL_V), cur(small_w, COL_SMALL),
                  prev(c, COL_R), prev(c, COL_K), prev(c, COL_V), prev(small_w, COL_SMALL),
                  const((3, c)), const((1, small_w)), const((1, c)), const((DECAY_LORA, c)),
                  const((1, c)), const((A_LORA, c)), const((GATE_LORA, c)),
                  const((1, c)), const((1, c)), const((c, LANES)), const((LANES, c))],
        out_specs=[pl.BlockSpec((tm, c), lambda i: (i, 0))] * 7,
        out_shape=[out, jax.ShapeDtypeStruct((t, c), F32), out, out, out, out, out],
        compiler_params=_params("parallel"),
        name="rwkv_prep",
    )(proj, proj, proj, proj, proj, proj, proj, proj,
      mu_rkv, mu_s, w0, ww2, a0, wa2, wg2, k_k, k_a, e, et)


def _block_diag(x):
    lane = lax.broadcasted_iota(jnp.int32, x.shape, 1)
    zero = jnp.zeros_like(x)
    return jnp.concatenate([jnp.where(lane < RWKV_HEAD, x, zero),
                            jnp.where(lane >= RWKV_HEAD, x, zero)], axis=0)


def _rwkv_chunk(r, lw, k, v, a, b, state, consts):
    ltri, strict, incl, eye, diag_blocks = consts
    c = CHUNK
    lw_hi = lw.astype(BF16)
    rem = lw - lw_hi.astype(F32)
    lw_mid = rem.astype(BF16)
    lw_lo = (rem - lw_mid.astype(F32)).astype(BF16)
    cum = functools.partial(jnp.dot, ltri, preferred_element_type=F32)
    cl = cum(lw_hi) + (cum(lw_mid) + cum(lw_lo))
    cl_end = cl[c - 1:c, :]
    p_incl = jnp.exp(cl)
    p_inv = jnp.exp(-cl)
    p_tail = jnp.exp(cl_end - cl)
    a_t = a * jnp.exp(cl - lw)
    r_t = r * p_incl
    b_t = b * p_inv
    k_t = k * p_inv

    ar = jnp.concatenate([a_t, r_t], axis=0)
    g_b = _dot(ar, _block_diag(b_t), NT)
    g_k = _dot(ar, _block_diag(k_t), NT)
    a_ab = jnp.where(strict, g_b[:c], 0.0)
    a_rb = jnp.where(incl, g_b[c:], 0.0)
    a_ak = jnp.where(strict, g_k[:c], 0.0)
    a_rk = jnp.where(incl, g_k[c:], 0.0)

    x = a_ab
    t_inv = jnp.where(eye, 1.0, 0.0) + x
    for _ in range(5):
        x = _dot_x3(x, _block_diag(x))
        t_inv = t_inv + _dot_x3(t_inv, _block_diag(x))

    av = _dot(jnp.concatenate([a_ak, a_rk], axis=0), _block_diag(v))
    rhs = jnp.concatenate([_block_diag(a_t), _block_diag(av[:c])], axis=1)
    tw = _dot_x3(t_inv, rhs)
    a_eff, w_eff = tw[:, :LANES], tw[:, LANES:]

    s_bd = state
    uo = _dot(jnp.concatenate([a_eff, r_t], axis=0), s_bd, NT)
    u = uo[:c] + w_eff
    o = uo[c:] + _dot(a_rb, _block_diag(u)) + av[c:]
    upd = _dot(jnp.concatenate([u, v], axis=0),
               jnp.concatenate([b * p_tail, k * p_tail], axis=0), TN)
    new_state = s_bd * jnp.exp(cl_end) + jnp.where(diag_blocks, upd, 0.0)
    return o, new_state


def _rwkv_kernel(r_ref, lw_ref, k_ref, v_ref, a_ref, b_ref, o_ref, s_ref, *, n_chunks, n_pairs):
    @pl.when(pl.program_id(2) == 0)
    def _():
        s_ref[...] = jnp.zeros_like(s_ref)

    c = CHUNK
    row = lax.broadcasted_iota(jnp.int32, (c, LANES), 0)
    col = lax.broadcasted_iota(jnp.int32, (c, LANES), 1) % RWKV_HEAD
    ltri = jnp.where(lax.broadcasted_iota(jnp.int32, (c, c), 1) <= lax.broadcasted_iota(jnp.int32, (c, c), 0),
                     1.0, 0.0).astype(BF16)
    rr = lax.broadcasted_iota(jnp.int32, (LANES, LANES), 0) // RWKV_HEAD
    cc = lax.broadcasted_iota(jnp.int32, (LANES, LANES), 1) // RWKV_HEAD
    consts = (ltri, col < row, col <= row, col == row, rr == cc)

    def chunk(ci, carry):
        r0 = pl.multiple_of(ci * c, c)
        for p in range(n_pairs):
            sl = (pl.ds(r0, c), slice(p * LANES, (p + 1) * LANES))
            o, s_new = _rwkv_chunk(r_ref[sl].astype(F32), lw_ref[sl], k_ref[sl].astype(F32),
                                   v_ref[sl].astype(F32), a_ref[sl].astype(F32), b_ref[sl].astype(F32),
                                   s_ref[p], consts)
            s_ref[p] = s_new
            o_ref[sl] = o
        return carry

    lax.fori_loop(0, n_chunks, chunk, 0)


def _rwkv_recurrence(r, lw, k, v, a, b, *, batch, seq, tm=512, n_pairs=2):
    t, c = r.shape
    width = n_pairs * LANES
    tm = min(tm, seq)
    nt = seq // tm
    spec = pl.BlockSpec((tm, width), lambda bi, g, i: (bi * nt + i, g))
    return pl.pallas_call(
        functools.partial(_rwkv_kernel, n_chunks=tm // CHUNK, n_pairs=n_pairs),
        grid=(batch, c // width, nt),
        in_specs=[spec] * 6,
        out_specs=spec,
        out_shape=jax.ShapeDtypeStruct((t, c), F32),
        scratch_shapes=[pltpu.VMEM((n_pairs, LANES, LANES), F32)],
        compiler_params=_params("parallel", "parallel", "arbitrary"),
        name="rwkv_recurrence",
    )(r, lw, k, v, a, b)


def _mix_tail_kernel(x_ref, ga_ref, gb_ref, bg_ref, oa_ref, woa_ref,
                     o_ref, r_ref, k_ref, v_ref, g_ref, rk_ref, lw_ref, lb_ref, e_ref, et_ref,
                     wob_ref, wo_ref, gpost_ref, out_ref):
    inv_n = 1.0 / RWKV_HEAD
    o = o_ref[...]
    mean = _head_sum(o, e_ref, et_ref) * inv_n
    cen = o - mean
    var = _head_sum(cen * cen, e_ref, et_ref) * inv_n
    on = cen * lax.rsqrt(var + LNX_EPS) * lw_ref[...] + lb_ref[...]
    v = v_ref[...].astype(F32)
    rk = r_ref[...].astype(F32) * k_ref[...].astype(F32) * rk_ref[...]
    bonus = _head_sum(rk, e_ref, et_ref) * v
    ob = ((on + bonus) * g_ref[...].astype(F32)).astype(BF16)

    y_a = jnp.dot(oa_ref[...], woa_ref[...], preferred_element_type=F32)
    y_b = jnp.dot(ob, wob_ref[...], preferred_element_type=F32)
    gate_a = _sigmoid(ga_ref[...].astype(F32) + bg_ref[0:1, :])
    gate_b = _sigmoid(gb_ref[...].astype(F32) + bg_ref[1:2, :])
    merged = (gate_a * y_a + gate_b * y_b).astype(BF16)
    z = jnp.dot(merged, wo_ref[...], preferred_element_type=F32)
    out_ref[...] = x_ref[...] + _rms(z, gpost_ref[...])


def _resident(shape):
    return pl.BlockSpec(shape, lambda *_: (0,) * len(shape), pipeline_mode=pl.Buffered(1))


def _mix_tail(x, proj, b_gate, o_a, w_oa, o_raw, r, k, v, g, r_k, lnx_w, lnx_b, e, et, w_ob, w_o, g_post, *, tm=256):
    t, d = x.shape
    c = RWKV_WIDTH
    row = lambda width, col=0: pl.BlockSpec((tm, width), lambda i: (i, col // width))
    return pl.pallas_call(
        _mix_tail_kernel,
        grid=(t // tm,),
        in_specs=[row(d), row(d, COL_GATE_A), row(d, COL_GATE_B), _resident((2, d)),
                  row(c), _resident((c, d)),
                  row(c), row(c), row(c), row(c), row(c),
                  _resident((1, c)), _resident((1, c)), _resident((1, c)),
                  _resident((c, LANES)), _resident((LANES, c)),
                  _resident((c, d)), _resident((d, d)), _resident((1, d))],
        out_specs=row(d),
        out_shape=jax.ShapeDtypeStruct((t, d), F32),
        compiler_params=_params("parallel"),
        name="mix_tail",
    )(x, proj, proj, b_gate, o_a, w_oa, o_raw, r, k, v, g, r_k, lnx_w, lnx_b, e, et, w_ob, w_o, g_post)


def _xattn_kernel(x_ref, gpre_ref, wq_ref, kv_ref, wo_ref, gpost_ref, out_ref):
    x = x_ref[...]
    h = _rms(x, gpre_ref[...]).astype(BF16)
    q = jnp.dot(h, wq_ref[...], preferred_element_type=F32).astype(BF16)
    width = MEM_HEADS * MEM_HEAD
    heads = []
    for hd in range(MEM_HEADS):
        lo = hd * MEM_HEAD
        k = kv_ref[:, lo:lo + MEM_HEAD]
        v = kv_ref[:, width + lo:width + lo + MEM_HEAD]
        s = lax.dot_general(q[:, lo:lo + MEM_HEAD], k, NT, preferred_element_type=F32)
        p = jnp.exp(s - jnp.max(s, axis=-1, keepdims=True))
        p = p / jnp.sum(p, axis=-1, keepdims=True)
        heads.append(jnp.dot(p.astype(BF16), v, preferred_element_type=F32).astype(BF16))
    o = jnp.concatenate(heads, axis=1)
    z = jnp.dot(o, wo_ref[...], preferred_element_type=F32)
    out_ref[...] = x + _rms(z, gpost_ref[...])


def _xattn(x, g_pre, w_cq, kv_mem, w_co, g_post, *, batch, seq, n_mem, tq=512):
    t, d = x.shape
    tq = min(tq, seq)
    nq = seq // tq
    width = MEM_HEADS * MEM_HEAD
    return pl.pallas_call(
        _xattn_kernel,
        grid=(batch, nq),
        in_specs=[pl.BlockSpec((tq, d), lambda b, i: (b * nq + i, 0)),
                  _resident((1, d)), _resident((d, width)),
                  pl.BlockSpec((n_mem, 2 * width), lambda b, i: (b, 0)),
                  _resident((width, d)), _resident((1, d))],
        out_specs=pl.BlockSpec((tq, d), lambda b, i: (b * nq + i, 0)),
        out_shape=jax.ShapeDtypeStruct((t, d), F32),
        compiler_params=_params("parallel", "arbitrary"),
        name="xattn",
    )(x, g_pre, w_cq, kv_mem, w_co, g_post)


def _pad_cols(w, n):
    return jnp.pad(w, ((0, 0), (0, n - w.shape[1])))


def _rope_tables(positions):
    half = QK_ROPE // 2
    inv = ROPE_THETA ** (-jnp.arange(half, dtype=F32) / half)
    ang = positions.astype(F32)[..., None] * inv
    return jnp.cos(ang), jnp.sin(ang)


def _rotate(x, cos, sin):
    half = QK_ROPE // 2
    x1, x2 = x[..., :half], x[..., half:]
    return jnp.concatenate([x1 * cos - x2 * sin, x1 * sin + x2 * cos], axis=-1)


def _layer(x, mem2d, positions, p, *, batch, seq, n_mem):
    row = lambda a: a.reshape(1, -1).astype(F32)
    c = RWKV_WIDTH
    d = D_MODEL
    ffp = -(-D_FF // (2 * MXU_DIM)) * (2 * MXU_DIM)

    def ffn(x, pre, gate, up, down, post):
        wg = _pad_cols(gate, ffp).astype(BF16)
        wu = _pad_cols(up, ffp).astype(BF16)
        wd = jnp.pad(down, ((0, ffp - D_FF), (0, 0))).astype(BF16)
        return _ffn(x, row(pre), wg, wu, wd, row(post))

    x = ffn(x, p['n_ffn1_pre'], p['w_ffn1_gate'], p['w_ffn1_up'], p['w_ffn1_down'], p['n_ffn1_post'])

    w_in = p['w_in']
    o_rwkv = MLA_COLS
    o_gate = MLA_COLS + RWKV_COLS
    zeros64 = jnp.zeros((d, QK_ROPE), F32)
    w_in_r = jnp.concatenate([
        w_in[:, o_gate:],
        w_in[:, o_rwkv:o_rwkv + 3 * c],
        w_in[:, :Q_LORA + KV_LORA],
        w_in[:, o_rwkv + 3 * c:o_gate],
        w_in[:, Q_LORA + KV_LORA:MLA_COLS], zeros64,
    ], axis=1)
    w_in_r = _pad_cols(w_in_r, IN_COLS_PADDED).astype(BF16)
    proj = _norm_matmul(x, 0, d, row(p['n_mix_pre']), w_in_r, tm=512, tn=768)

    scale = (QK_NOPE + QK_ROPE) ** -0.5
    w_uq = p['w_uq'].reshape(Q_LORA, MLA_HEADS, QK_NOPE + QK_ROPE) * scale
    w_uq_r = jnp.concatenate([w_uq[:, :, :QK_NOPE].reshape(Q_LORA, -1),
                              w_uq[:, :, QK_NOPE:].reshape(Q_LORA, -1)], axis=1).astype(BF16)
    q_up = _norm_matmul(proj, COL_CQ // Q_LORA, Q_LORA, row(p['n_q_lat']), w_uq_r, tm=512, tn=768)
    kv_up = _norm_matmul(proj, COL_CKV // KV_LORA, KV_LORA, row(p['n_kv_lat']), p['w_ukv'].astype(BF16),
                         tm=512, tn=1024)
    cos, sin = _rope_tables(positions)
    q_rope = q_up[:, MLA_HEADS * QK_NOPE:].astype(F32).reshape(batch, seq, MLA_HEADS, QK_ROPE)
    q_rope = _rotate(q_rope, cos[:, :, None, :], sin[:, :, None, :]).astype(BF16).transpose(0, 2, 1, 3)
    k_rope = proj[:, COL_KROPE:COL_KROPE + QK_ROPE].astype(F32).reshape(batch, seq, QK_ROPE)
    k_rope = _rotate(k_rope, cos, sin).astype(BF16)
    o_a = _mla_attention(q_up, q_rope, kv_up, k_rope, batch=batch, seq=seq)

    head_of = jnp.arange(c) // RWKV_HEAD
    e = (head_of[:, None] == jnp.arange(LANES)[None, :]).astype(BF16)
    mu = p['mu_shift']
    r, lw, k, v, a_in, b_in, g = _rwkv_prep(
        proj, mu[:3 * c].reshape(3, c), row(mu[3 * c:]), row(p['w0']), p['w_w2'].astype(BF16),
        row(p['a0']), p['w_a2'].astype(BF16), p['w_g2'].astype(BF16), row(p['k_k']), row(p['k_a']),
        e, e.T, seq=seq)
    o_raw = _rwkv_recurrence(r, lw, k, v, a_in, b_in, batch=batch, seq=seq)

    x = _mix_tail(x, proj, p['b_gate'].reshape(2, d), o_a, p['w_oa'].astype(BF16), o_raw, r, k, v, g,
                  row(p['r_k']), row(p['lnx_w']), row(p['lnx_b']), e, e.T,
                  p['w_ob'].astype(BF16), p['w_o'].astype(BF16), row(p['n_mix_post']))

    width = MEM_HEADS * MEM_HEAD
    w_ckv = p['w_ckv'].reshape(d, MEM_HEADS, 2, MEM_HEAD).transpose(0, 2, 1, 3).reshape(d, 2 * width)
    kv_mem = _norm_matmul(mem2d, 0, d, row(p['n_mem']), w_ckv.astype(BF16), tm=512, tn=1024)
    w_cq = (p['w_cq'] * MEM_HEAD ** -0.5).astype(BF16)
    x = _xattn(x, row(p['n_x_pre']), w_cq, kv_mem, p['w_co'].astype(BF16), row(p['n_x_post']),
               batch=batch, seq=seq, n_mem=n_mem)

    return ffn(x, p['n_ffn2_pre'], p['w_ffn2_gate'], p['w_ffn2_up'], p['w_ffn2_down'], p['n_ffn2_post'])


_PARAM_NAMES = (
    'n_ffn1_pre', 'n_ffn1_post', 'w_ffn1_gate', 'w_ffn1_up', 'w_ffn1_down',
    'n_mix_pre', 'n_mix_post', 'w_in', 'b_gate',
    'n_q_lat', 'w_uq', 'n_kv_lat', 'w_ukv', 'w_oa',
    'mu_shift', 'w0', 'w_w2', 'a0', 'w_a2', 'w_g2', 'k_k', 'k_a', 'r_k', 'lnx_w', 'lnx_b', 'w_ob',
    'w_o',
    'n_x_pre', 'n_x_post', 'n_mem', 'w_cq', 'w_ckv', 'w_co',
    'n_ffn2_pre', 'n_ffn2_post', 'w_ffn2_gate', 'w_ffn2_up', 'w_ffn2_down')


def kernel(x, mem, positions, n_ffn1_pre, n_ffn1_post, w_ffn1_gate, w_ffn1_up, w_ffn1_down, n_mix_pre, n_mix_post, w_in, b_gate, n_q_lat, w_uq, n_kv_lat, w_ukv, w_oa, mu_shift, w0, w_w2, a0, w_a2, w_g2, k_k, k_a, r_k, lnx_w, lnx_b, w_ob, w_o, n_x_pre, n_x_post, n_mem, w_cq, w_ckv, w_co, n_ffn2_pre, n_ffn2_post, w_ffn2_gate, w_ffn2_up, w_ffn2_down):
    stacked = (n_ffn1_pre, n_ffn1_post, w_ffn1_gate, w_ffn1_up, w_ffn1_down, n_mix_pre, n_mix_post, w_in, b_gate,
               n_q_lat, w_uq, n_kv_lat, w_ukv, w_oa, mu_shift, w0, w_w2, a0, w_a2, w_g2, k_k, k_a, r_k, lnx_w,
               lnx_b, w_ob, w_o, n_x_pre, n_x_post, n_mem, w_cq, w_ckv, w_co, n_ffn2_pre, n_ffn2_post,
               w_ffn2_gate, w_ffn2_up, w_ffn2_down)
    batch, seq, d = x.shape
    n_mem_tokens = mem.shape[1]
    x2d = x.reshape(batch * seq, d)
    mem2d = mem.reshape(batch * n_mem_tokens, d)
    for layer in range(n_ffn1_pre.shape[0]):
        p = {name: arr[layer] for name, arr in zip(_PARAM_NAMES, stacked)}
        x2d = _layer(x2d, mem2d, positions, p, batch=batch, seq=seq, n_mem=n_mem_tokens)
    return x2d.reshape(batch, seq, d)
```

```python
import functools

import jax
import jax.numpy as jnp
from jax import lax
from jax.experimental import pallas as pl
from jax.experimental.pallas import tpu as pltpu

F32 = jnp.float32
BF16 = jnp.bfloat16

D_MODEL = 2048
D_FF = 5504
CHUNK = 64
EPS = 1e-6

MLA_HEADS = 8
QK_NOPE = 128
QK_ROPE = 64
V_HEAD = 128
Q_LORA = 512
KV_LORA = 256
ROPE_THETA = 10000.0

RWKV_HEADS = 16
RWKV_HEAD = 64
RWKV_WIDTH = RWKV_HEADS * RWKV_HEAD
DECAY_LORA = 64
A_LORA = 64
GATE_LORA = 128
LNX_EPS = 64e-5

MEM_HEADS = 4
MEM_HEAD = 256

MLA_COLS = Q_LORA + KV_LORA + QK_ROPE
RWKV_COLS = 3 * RWKV_WIDTH + DECAY_LORA + A_LORA + GATE_LORA

LANES = 128
MXU_DIM = 256
VMEM_LIMIT_BYTES = 56 * 1024 * 1024

COL_GATE_A = 0
COL_GATE_B = 2048
COL_R = 4096
COL_K = 5120
COL_V = 6144
COL_CQ = 7168
COL_CKV = 7680
COL_SMALL = 7936
COL_KROPE = 8192
IN_COLS_PADDED = 8448

NN = (((1,), (0,)), ((), ()))
NT = (((1,), (1,)), ((), ()))
TN = (((0,), (0,)), ((), ()))


def _dot(a, b, dims=NN):
    return lax.dot_general(a.astype(BF16), b.astype(BF16), dims, preferred_element_type=F32)


def _each(f, *cols):
    return [f(*xs) for xs in zip(*cols)]


def _split2(x):
    hi = x.astype(BF16)
    lo = (x - hi.astype(F32)).astype(BF16)
    return hi, lo


def _dot_x3(a, b, dims=NN):
    ah, al = _split2(a)
    bh, bl = _split2(b)
    f = functools.partial(lax.dot_general, dimension_numbers=dims, preferred_element_type=F32)
    return f(ah, bh) + (f(ah, bl) + f(al, bh))


def _dot_exact_rhs(a, b_exact, dims=NN):
    ah, al = _split2(a)
    f = functools.partial(lax.dot_general, dimension_numbers=dims, preferred_element_type=F32)
    return f(ah, b_exact) + f(al, b_exact)


def _rms(xf, g, eps=EPS):
    return xf * lax.rsqrt(jnp.mean(xf * xf, axis=-1, keepdims=True) + eps) * g


def _sigmoid(x):
    return 1.0 / (1.0 + jnp.exp(-x))


def _params(*sem):
    return pltpu.CompilerParams(dimension_semantics=sem, vmem_limit_bytes=VMEM_LIMIT_BYTES)


def _norm_matmul_kernel(x_ref, g_ref, w_ref, o_ref, h_ref):
    @pl.when(pl.program_id(1) == 0)
    def _():
        h_ref[...] = _rms(x_ref[...].astype(F32), g_ref[...]).astype(BF16)

    o_ref[...] = jnp.dot(h_ref[...], w_ref[...], preferred_element_type=F32).astype(o_ref.dtype)


def _norm_matmul(x, col_block, k_dim, g, w, *, tm, tn, out_dtype=BF16):
    m = x.shape[0]
    n = w.shape[1]
    tm = min(tm, m)
    return pl.pallas_call(
        _norm_matmul_kernel,
        grid=(m // tm, n // tn),
        in_specs=[
            pl.BlockSpec((tm, k_dim), lambda i, j: (i, col_block)),
            pl.BlockSpec((1, k_dim), lambda i, j: (0, 0)),
            pl.BlockSpec((k_dim, tn), lambda i, j: (0, j)),
        ],
        out_specs=pl.BlockSpec((tm, tn), lambda i, j: (i, j)),
        out_shape=jax.ShapeDtypeStruct((m, n), out_dtype),
        scratch_shapes=[pltpu.VMEM((tm, k_dim), BF16)],
        compiler_params=_params("parallel", "arbitrary"),
        name="norm_matmul",
    )(x, g, w)


def _ffn_kernel(x_ref, gpre_ref, wg_ref, wu_ref, wd_ref, gpost_ref, o_ref, h_ref, acc_ref):
    j = pl.program_id(1)

    @pl.when(j == 0)
    def _():
        h_ref[...] = _rms(x_ref[...], gpre_ref[...]).astype(BF16)
        acc_ref[...] = jnp.zeros_like(acc_ref)

    h = h_ref[...]
    gate = jnp.dot(h, wg_ref[...], preferred_element_type=F32)
    up = jnp.dot(h, wu_ref[...], preferred_element_type=F32)
    act = (gate * _sigmoid(gate) * up).astype(BF16)
    acc_ref[...] += jnp.dot(act, wd_ref[...], preferred_element_type=F32)

    @pl.when(j == pl.num_programs(1) - 1)
    def _():
        o_ref[...] = x_ref[...] + 0.5 * _rms(acc_ref[...], gpost_ref[...])


def _ffn(x, g_pre, wg, wu, wd, g_post, *, tm=512, tf=512):
    m, d = x.shape
    fp = wg.shape[1]
    tm = min(tm, m)
    return pl.pallas_call(
        _ffn_kernel,
        grid=(m // tm, fp // tf),
        in_specs=[
            pl.BlockSpec((tm, d), lambda i, j: (i, 0)),
            pl.BlockSpec((1, d), lambda i, j: (0, 0)),
            pl.BlockSpec((d, tf), lambda i, j: (0, j)),
            pl.BlockSpec((d, tf), lambda i, j: (0, j)),
            pl.BlockSpec((tf, d), lambda i, j: (j, 0)),
            pl.BlockSpec((1, d), lambda i, j: (0, 0)),
        ],
        out_specs=pl.BlockSpec((tm, d), lambda i, j: (i, 0)),
        out_shape=jax.ShapeDtypeStruct((m, d), F32),
        scratch_shapes=[pltpu.VMEM((tm, d), BF16), pltpu.VMEM((tm, d), F32)],
        compiler_params=_params("parallel", "arbitrary"),
        name="ffn",
    )(x, g_pre, wg, wu, wd, g_post)


def _mla_attn_kernel(qn_ref, qr_ref, kv_ref, kr_ref, o_ref, *, tq, n_heads):
    qi = pl.program_id(2)
    heads = list(range(n_heads))
    kv_width = QK_NOPE + V_HEAD
    qn = [qn_ref[:, h * QK_NOPE:(h + 1) * QK_NOPE] for h in heads]
    qr = [qr_ref[0, h] for h in heads]

    def scores(start):
        kr = kr_ref[0, pl.ds(start, tq), :]
        return _each(lambda h, qn_, qr_: (
            lax.dot_general(qn_, kv_ref[pl.ds(start, tq), h * kv_width:h * kv_width + QK_NOPE], NT,
                            preferred_element_type=F32)
            + lax.dot_general(qr_, kr, NT, preferred_element_type=F32)), heads, qn, qr)

    def update(carry, s, start):
        m, l, acc = carry
        m_new = _each(lambda m_, s_: jnp.maximum(m_, jnp.max(s_, axis=-1, keepdims=True)), m, s)
        alpha = _each(lambda m_, n_: jnp.exp(m_ - n_), m, m_new)
        p = _each(lambda s_, n_: jnp.exp(s_ - n_), s, m_new)
        l = _each(lambda a_, l_, p_: a_ * l_ + jnp.sum(p_, axis=-1, keepdims=True), alpha, l, p)
        pv = _each(lambda h, p_: jnp.dot(
            p_.astype(BF16), kv_ref[pl.ds(start, tq), h * kv_width + QK_NOPE:(h + 1) * kv_width],
            preferred_element_type=F32), heads, p)
        acc = _each(lambda a_, acc_, pv_: a_ * acc_ + pv_, alpha, acc, pv)
        return m_new, l, acc

    def body(kj, carry):
        start = pl.multiple_of(kj * tq, tq)
        return update(carry, scores(start), start)

    init = ([jnp.full((tq, 1), -1e30, F32) for _ in heads], [jnp.zeros((tq, 1), F32) for _ in heads],
            [jnp.zeros((tq, V_HEAD), F32) for _ in heads])
    carry = lax.fori_loop(0, qi, body, init)

    start = pl.multiple_of(qi * tq, tq)
    q_chunk = lax.broadcasted_iota(jnp.int32, (tq, tq), 0) // CHUNK
    k_chunk = lax.broadcasted_iota(jnp.int32, (tq, tq), 1) // CHUNK
    visible = k_chunk <= q_chunk
    s = _each(lambda s_: jnp.where(visible, s_, -1e30), scores(start))
    m, l, acc = update(carry, s, start)
    for h in heads:
        o_ref[:, h * V_HEAD:(h + 1) * V_HEAD] = (acc[h] / l[h]).astype(o_ref.dtype)


def _mla_attention(q_up, qr, kv_up, kr, *, batch, seq, tq=256, n_heads=4):
    nq = seq // tq
    return pl.pallas_call(
        functools.partial(_mla_attn_kernel, tq=tq, n_heads=n_heads),
        grid=(batch, MLA_HEADS // n_heads, nq),
        in_specs=[
            pl.BlockSpec((tq, n_heads * QK_NOPE), lambda b, g, i: (b * nq + i, g)),
            pl.BlockSpec((1, n_heads, tq, QK_ROPE), lambda b, g, i: (b, g, i, 0)),
            pl.BlockSpec((seq, n_heads * (QK_NOPE + V_HEAD)), lambda b, g, i: (b, g)),
            pl.BlockSpec((1, seq, QK_ROPE), lambda b, g, i: (b, 0, 0)),
        ],
        out_specs=pl.BlockSpec((tq, n_heads * V_HEAD), lambda b, g, i: (b * nq + i, g)),
        out_shape=jax.ShapeDtypeStruct((batch * seq, MLA_HEADS * V_HEAD), BF16),
        compiler_params=_params("parallel", "parallel", "arbitrary"),
        name="mla_attention",
    )(q_up, qr, kv_up, kr)


def _head_sum(x, e_ref, et_ref):
    s = _dot_exact_rhs(x, e_ref[...])
    return _dot_exact_rhs(s, et_ref[...])


def _shift(y, prev_row, mu, first):
    rows = lax.broadcasted_iota(jnp.int32, y.shape, 0)
    prev0 = jnp.where(first, 0.0, prev_row)
    y_prev = jnp.where(rows == 0, prev0, pltpu.roll(y, 1, 0))
    return y + (y_prev - y) * mu


def _rwkv_prep_kernel(r_ref, k_ref, v_ref, s_ref, rp_ref, kp_ref, vp_ref, sp_ref,
                      mu_rkv_ref, mu_s_ref, w0_ref, ww2_ref, a0_ref, wa2_ref, wg2_ref,
                      kk_ref, ka_ref, e_ref, et_ref,
                      ro_ref, lw_ref, ko_ref, vo_ref, ao_ref, bo_ref, go_ref, *, tiles_per_seq):
    first = (pl.program_id(0) % tiles_per_seq) == 0
    last8 = slice(7, 8)

    def shifted(ref, pref, mu):
        return _shift(ref[...].astype(F32), pref[last8, :].astype(F32), mu, first)

    r = shifted(r_ref, rp_ref, mu_rkv_ref[0:1, :])
    k = shifted(k_ref, kp_ref, mu_rkv_ref[1:2, :])
    v = shifted(v_ref, vp_ref, mu_rkv_ref[2:3, :])
    small = shifted(s_ref, sp_ref, mu_s_ref[...])
    dw = small[:, 0:DECAY_LORA]
    da = small[:, DECAY_LORA:DECAY_LORA + A_LORA]
    dg = small[:, DECAY_LORA + A_LORA:]

    z = -(w0_ref[...] + _dot(jnp.tanh(dw), ww2_ref[...]))
    softplus = jnp.maximum(z, 0.0) + jnp.log(1.0 + jnp.exp(-jnp.abs(z)))
    w = -softplus - 0.5
    lw_ref[...] = -jnp.exp(w)
    a = _sigmoid(a0_ref[...] + _dot(da, wa2_ref[...]))
    go_ref[...] = _dot(_sigmoid(dg), wg2_ref[...]).astype(go_ref.dtype)

    kk = k * kk_ref[...]
    norm = jnp.maximum(jnp.sqrt(_head_sum(kk * kk, e_ref, et_ref)), 1e-12)
    kk = kk / norm
    ro_ref[...] = r.astype(ro_ref.dtype)
    ko_ref[...] = (k * (1.0 + (a - 1.0) * ka_ref[...])).astype(ko_ref.dtype)
    vo_ref[...] = v.astype(vo_ref.dtype)
    ao_ref[...] = (-kk).astype(ao_ref.dtype)
    bo_ref[...] = (kk * a).astype(bo_ref.dtype)


def _rwkv_prep(proj, mu_rkv, mu_s, w0, ww2, a0, wa2, wg2, k_k, k_a, e, et, *, seq, tm=256):
    t = proj.shape[0]
    c = RWKV_WIDTH
    small_w = DECAY_LORA + A_LORA + GATE_LORA

    def cur(width, col):
        return pl.BlockSpec((tm, width), lambda i: (i, col // width))

    def prev(width, col):
        return pl.BlockSpec((8, width), lambda i: (jnp.maximum(i * (tm // 8) - 1, 0), col // width))

    def const(shape):
        return pl.BlockSpec(shape, lambda i: (0, 0))

    out = jax.ShapeDtypeStruct((t, c), BF16)
    return pl.pallas_call(
        functools.partial(_rwkv_prep_kernel, tiles_per_seq=seq // tm),
        grid=(t // tm,),
        in_specs=[cur(c, COL_R), cur(c, COL_K), cur(c, COL_V), cur(small_w, COL_SMALL),
                  prev(c, COL_R), prev(c, COL_K), prev(c, COL_V), prev(small_w, COL_SMALL),
                  const((3, c)), const((1, small_w)), const((1, c)), const((DECAY_LORA, c)),
                  const((1, c)), const((A_LORA, c)), const((GATE_LORA, c)),
                  const((1, c)), const((1, c)), const((c, LANES)), const((LANES, c))],
        out_specs=[pl.BlockSpec((tm, c), lambda i: (i, 0))] * 7,
        out_shape=[out, jax.ShapeDtypeStruct((t, c), F32), out, out, out, out, out],
        compiler_params=_params("parallel"),
        name="rwkv_prep",
    )(proj, proj, proj, proj, proj, proj, proj, proj,
      mu_rkv, mu_s, w0, ww2, a0, wa2, wg2, k_k, k_a, e, et)


def _block_diag(x):
    lane = lax.broadcasted_iota(jnp.int32, x.shape, 1)
    zero = jnp.zeros_like(x)
    return jnp.concatenate([jnp.where(lane < RWKV_HEAD, x, zero),
                            jnp.where(lane >= RWKV_HEAD, x, zero)], axis=0)


def _unit_lower_inverse(a, eye, level_masks):
    square = lambda x: _dot(x, _block_diag(x))
    times = lambda t, x: t + _dot(t, _block_diag(x))
    base = _each(lambda x: x * level_masks[0], a)
    x2 = _each(square, base)
    x4 = _each(square, x2)
    t = _each(lambda x: eye + x, base)
    t = _each(times, t, x2)
    t = _each(times, t, x4)
    for mask in level_masks[1:]:
        ta = _each(lambda t_, a_: _dot(t_, _block_diag(a_ * mask)), t, a)
        t = _each(lambda t_, ta_: t_ + _dot(ta_, _block_diag(t_)), t, ta)
    return t


def _rwkv_chunk(r, lw, k, v, a, b, state, consts):
    ltri, strict, incl, eye, diag_blocks, level_masks = consts
    c = CHUNK
    rows = lambda x, y: jnp.concatenate([x, y], axis=0)

    def cumulative(x):
        hi, lo = _split2(x)
        return jnp.dot(ltri, hi, preferred_element_type=F32) + jnp.dot(ltri, lo, preferred_element_type=F32)

    cl = _each(cumulative, lw)
    cl_end = _each(lambda x: x[c - 1:c, :], cl)
    p_inv = _each(lambda x: jnp.exp(-x), cl)
    a_t = _each(lambda a_, cl_, lw_: a_ * jnp.exp(cl_ - lw_), a, cl, lw)
    r_t = _each(lambda r_, cl_: r_ * jnp.exp(cl_), r, cl)
    b_t = _each(jnp.multiply, b, p_inv)
    k_t = _each(jnp.multiply, k, p_inv)

    ar = _each(rows, a_t, r_t)
    g_b = _each(lambda x, y: _dot(x, _block_diag(y), NT), ar, b_t)
    g_k = _each(lambda x, y: _dot(x, _block_diag(y), NT), ar, k_t)
    a_ab = _each(lambda g: jnp.where(strict, g[:c], 0.0), g_b)
    a_rb = _each(lambda g: jnp.where(incl, g[c:], 0.0), g_b)
    a_k = _each(lambda g: rows(jnp.where(strict, g[:c], 0.0), jnp.where(incl, g[c:], 0.0)), g_k)

    t_inv = _unit_lower_inverse(a_ab, eye, level_masks)

    av = _each(lambda x, v_: _dot(x, _block_diag(v_)), a_k, v)
    tw = _each(lambda t_, a_, av_: _dot(t_, jnp.concatenate([_block_diag(a_), _block_diag(av_[:c])], axis=1)),
               t_inv, a_t, av)

    uo = _each(lambda tw_, r_, s_: _dot(rows(tw_[:, :LANES], r_), s_, NT), tw, r_t, state)
    u = _each(lambda uo_, tw_: uo_[:c] + tw_[:, LANES:], uo, tw)
    o = _each(lambda uo_, arb_, u_, av_: uo_[c:] + _dot(arb_, _block_diag(u_)) + av_[c:], uo, a_rb, u, av)
    p_tail = _each(lambda e_, cl_: jnp.exp(e_ - cl_), cl_end, cl)
    upd = _each(lambda u_, v_, b_, k_, p_: _dot(rows(u_, v_), rows(b_ * p_, k_ * p_), TN), u, v, b, k, p_tail)
    new_state = _each(lambda s_, e_, upd_: s_ * jnp.exp(e_) + jnp.where(diag_blocks, upd_, 0.0),
                      state, cl_end, upd)
    return o, new_state


def _rwkv_kernel(r_ref, lw_ref, k_ref, v_ref, a_ref, b_ref, o_ref, s_ref, *, n_chunks, n_pairs):
    @pl.when(pl.program_id(2) == 0)
    def _():
        s_ref[...] = jnp.zeros_like(s_ref)

    c = CHUNK
    row = lax.broadcasted_iota(jnp.int32, (c, LANES), 0)
    col = lax.broadcasted_iota(jnp.int32, (c, LANES), 1) % RWKV_HEAD
    ltri = jnp.where(lax.broadcasted_iota(jnp.int32, (c, c), 1) <= lax.broadcasted_iota(jnp.int32, (c, c), 0),
                     1.0, 0.0).astype(BF16)
    rr = lax.broadcasted_iota(jnp.int32, (LANES, LANES), 0) // RWKV_HEAD
    cc = lax.broadcasted_iota(jnp.int32, (LANES, LANES), 1) // RWKV_HEAD
    def same(block):
        return (row // block) == (col // block)
    level_masks = [jnp.where(same(8), 1.0, 0.0)]
    for block in (8, 16, 32):
        level_masks.append(jnp.where(same(2 * block), 1.0, 0.0) - jnp.where(same(block), 1.0, 0.0))
    eye = jnp.where(col == row, 1.0, 0.0)
    consts = (ltri, col < row, col <= row, eye, rr == cc, level_masks)

    def chunk(ci, carry):
        r0 = pl.multiple_of(ci * c, c)
        tiles = [(pl.ds(r0, c), slice(p * LANES, (p + 1) * LANES)) for p in range(n_pairs)]
        load = lambda ref: [ref[sl].astype(F32) for sl in tiles]
        o, s_new = _rwkv_chunk(load(r_ref), load(lw_ref), load(k_ref), load(v_ref), load(a_ref), load(b_ref),
                               [s_ref[p] for p in range(n_pairs)], consts)
        for p in range(n_pairs):
            s_ref[p] = s_new[p]
            o_ref[tiles[p]] = o[p]
        return carry

    lax.fori_loop(0, n_chunks, chunk, 0)


def _rwkv_recurrence(r, lw, k, v, a, b, *, batch, seq, tm=512, n_pairs=8):
    t, c = r.shape
    width = n_pairs * LANES
    tm = min(tm, seq)
    nt = seq // tm
    spec = pl.BlockSpec((tm, width), lambda bi, g, i: (bi * nt + i, g))
    return pl.pallas_call(
        functools.partial(_rwkv_kernel, n_chunks=tm // CHUNK, n_pairs=n_pairs),
        grid=(batch, c // width, nt),
        in_specs=[spec] * 6,
        out_specs=spec,
        out_shape=jax.ShapeDtypeStruct((t, c), F32),
        scratch_shapes=[pltpu.VMEM((n_pairs, LANES, LANES), F32)],
        compiler_params=_params("parallel", "parallel", "arbitrary"),
        name="rwkv_recurrence",
    )(r, lw, k, v, a, b)


def _mix_tail_kernel(x_ref, ga_ref, gb_ref, bg_ref, oa_ref, woa_ref,
                     o_ref, r_ref, k_ref, v_ref, g_ref, rk_ref, lw_ref, lb_ref, e_ref, et_ref,
                     wob_ref, wo_ref, gpost_ref, out_ref):
    inv_n = 1.0 / RWKV_HEAD
    o = o_ref[...]
    mean = _head_sum(o, e_ref, et_ref) * inv_n
    cen = o - mean
    var = _head_sum(cen * cen, e_ref, et_ref) * inv_n
    on = cen * lax.rsqrt(var + LNX_EPS) * lw_ref[...] + lb_ref[...]
    v = v_ref[...].astype(F32)
    rk = r_ref[...].astype(F32) * k_ref[...].astype(F32) * rk_ref[...]
    bonus = _head_sum(rk, e_ref, et_ref) * v
    ob = ((on + bonus) * g_ref[...].astype(F32)).astype(BF16)

    y_a = jnp.dot(oa_ref[...], woa_ref[...], preferred_element_type=F32)
    y_b = jnp.dot(ob, wob_ref[...], preferred_element_type=F32)
    gate_a = _sigmoid(ga_ref[...].astype(F32) + bg_ref[0:1, :])
    gate_b = _sigmoid(gb_ref[...].astype(F32) + bg_ref[1:2, :])
    merged = (gate_a * y_a + gate_b * y_b).astype(BF16)
    z = jnp.dot(merged, wo_ref[...], preferred_element_type=F32)
    out_ref[...] = x_ref[...] + _rms(z, gpost_ref[...])


def _resident(shape):
    return pl.BlockSpec(shape, lambda *_: (0,) * len(shape), pipeline_mode=pl.Buffered(1))


def _mix_tail(x, proj, b_gate, o_a, w_oa, o_raw, r, k, v, g, r_k, lnx_w, lnx_b, e, et, w_ob, w_o, g_post, *, tm=256):
    t, d = x.shape
    c = RWKV_WIDTH
    row = lambda width, col=0: pl.BlockSpec((tm, width), lambda i: (i, col // width))
    return pl.pallas_call(
        _mix_tail_kernel,
        grid=(t // tm,),
        in_specs=[row(d), row(d, COL_GATE_A), row(d, COL_GATE_B), _resident((2, d)),
                  row(c), _resident((c, d)),
                  row(c), row(c), row(c), row(c), row(c),
                  _resident((1, c)), _resident((1, c)), _resident((1, c)),
                  _resident((c, LANES)), _resident((LANES, c)),
                  _resident((c, d)), _resident((d, d)), _resident((1, d))],
        out_specs=row(d),
        out_shape=jax.ShapeDtypeStruct((t, d), F32),
        compiler_params=_params("parallel"),
        name="mix_tail",
    )(x, proj, proj, b_gate, o_a, w_oa, o_raw, r, k, v, g, r_k, lnx_w, lnx_b, e, et, w_ob, w_o, g_post)


def _xattn_kernel(x_ref, gpre_ref, wq_ref, kv_ref, wo_ref, gpost_ref, out_ref):
    x = x_ref[...]
    h = _rms(x, gpre_ref[...]).astype(BF16)
    q = jnp.dot(h, wq_ref[...], preferred_element_type=F32).astype(BF16)
    width = MEM_HEADS * MEM_HEAD
    heads = []
    for hd in range(MEM_HEADS):
        lo = hd * MEM_HEAD
        k = kv_ref[:, lo:lo + MEM_HEAD]
        v = kv_ref[:, width + lo:width + lo + MEM_HEAD]
        s = lax.dot_general(q[:, lo:lo + MEM_HEAD], k, NT, preferred_element_type=F32)
        p = jnp.exp(s - jnp.max(s, axis=-1, keepdims=True))
        p = p / jnp.sum(p, axis=-1, keepdims=True)
        heads.append(jnp.dot(p.astype(BF16), v, preferred_element_type=F32).astype(BF16))
    o = jnp.concatenate(heads, axis=1)
    z = jnp.dot(o, wo_ref[...], preferred_element_type=F32)
    out_ref[...] = x + _rms(z, gpost_ref[...])


def _xattn(x, g_pre, w_cq, kv_mem, w_co, g_post, *, batch, seq, n_mem, tq=512):
    t, d = x.shape
    tq = min(tq, seq)
    nq = seq // tq
    width = MEM_HEADS * MEM_HEAD
    return pl.pallas_call(
        _xattn_kernel,
        grid=(batch, nq),
        in_specs=[pl.BlockSpec((tq, d), lambda b, i: (b * nq + i, 0)),
                  _resident((1, d)), _resident((d, width)),
                  pl.BlockSpec((n_mem, 2 * width), lambda b, i: (b, 0)),
                  _resident((width, d)), _resident((1, d))],
        out_specs=pl.BlockSpec((tq, d), lambda b, i: (b * nq + i, 0)),
        out_shape=jax.ShapeDtypeStruct((t, d), F32),
        compiler_params=_params("parallel", "arbitrary"),
        name="xattn",
    )(x, g_pre, w_cq, kv_mem, w_co, g_post)


def _pad_cols(w, n):
    return jnp.pad(w, ((0, 0), (0, n - w.shape[1])))


def _rope_tables(positions):
    half = QK_ROPE // 2
    inv = ROPE_THETA ** (-jnp.arange(half, dtype=F32) / half)
    ang = positions.astype(F32)[..., None] * inv
    return jnp.cos(ang), jnp.sin(ang)


def _rotate(x, cos, sin):
    half = QK_ROPE // 2
    x1, x2 = x[..., :half], x[..., half:]
    return jnp.concatenate([x1 * cos - x2 * sin, x1 * sin + x2 * cos], axis=-1)


def _layer(x, mem2d, positions, p, *, batch, seq, n_mem):
    row = lambda a: a.reshape(1, -1).astype(F32)
    c = RWKV_WIDTH
    d = D_MODEL
    ffp = -(-D_FF // (2 * MXU_DIM)) * (2 * MXU_DIM)

    def ffn(x, pre, gate, up, down, post):
        wg = _pad_cols(gate, ffp).astype(BF16)
        wu = _pad_cols(up, ffp).astype(BF16)
        wd = jnp.pad(down, ((0, ffp - D_FF), (0, 0))).astype(BF16)
        return _ffn(x, row(pre), wg, wu, wd, row(post))

    x = ffn(x, p['n_ffn1_pre'], p['w_ffn1_gate'], p['w_ffn1_up'], p['w_ffn1_down'], p['n_ffn1_post'])

    w_in = p['w_in']
    o_rwkv = MLA_COLS
    o_gate = MLA_COLS + RWKV_COLS
    zeros64 = jnp.zeros((d, QK_ROPE), F32)
    w_in_r = jnp.concatenate([
        w_in[:, o_gate:],
        w_in[:, o_rwkv:o_rwkv + 3 * c],
        w_in[:, :Q_LORA + KV_LORA],
        w_in[:, o_rwkv + 3 * c:o_gate],
        w_in[:, Q_LORA + KV_LORA:MLA_COLS], zeros64,
    ], axis=1)
    w_in_r = _pad_cols(w_in_r, IN_COLS_PADDED).astype(BF16)
    proj = _norm_matmul(x, 0, d, row(p['n_mix_pre']), w_in_r, tm=512, tn=768)

    scale = (QK_NOPE + QK_ROPE) ** -0.5
    w_uq = p['w_uq'].reshape(Q_LORA, MLA_HEADS, QK_NOPE + QK_ROPE) * scale
    w_uq_r = jnp.concatenate([w_uq[:, :, :QK_NOPE].reshape(Q_LORA, -1),
                              w_uq[:, :, QK_NOPE:].reshape(Q_LORA, -1)], axis=1).astype(BF16)
    q_up = _norm_matmul(proj, COL_CQ // Q_LORA, Q_LORA, row(p['n_q_lat']), w_uq_r, tm=512, tn=768)
    kv_up = _norm_matmul(proj, COL_CKV // KV_LORA, KV_LORA, row(p['n_kv_lat']), p['w_ukv'].astype(BF16),
                         tm=512, tn=1024)
    cos, sin = _rope_tables(positions)
    q_rope = q_up[:, MLA_HEADS * QK_NOPE:].astype(F32).reshape(batch, seq, MLA_HEADS, QK_ROPE)
    q_rope = _rotate(q_rope, cos[:, :, None, :], sin[:, :, None, :]).astype(BF16).transpose(0, 2, 1, 3)
    k_rope = proj[:, COL_KROPE:COL_KROPE + QK_ROPE].astype(F32).reshape(batch, seq, QK_ROPE)
    k_rope = _rotate(k_rope, cos, sin).astype(BF16)
    o_a = _mla_attention(q_up, q_rope, kv_up, k_rope, batch=batch, seq=seq)

    head_of = jnp.arange(c) // RWKV_HEAD
    e = (head_of[:, None] == jnp.arange(LANES)[None, :]).astype(BF16)
    mu = p['mu_shift']
    r, lw, k, v, a_in, b_in, g = _rwkv_prep(
        proj, mu[:3 * c].reshape(3, c), row(mu[3 * c:]), row(p['w0']), p['w_w2'].astype(BF16),
        row(p['a0']), p['w_a2'].astype(BF16), p['w_g2'].astype(BF16), row(p['k_k']), row(p['k_a']),
        e, e.T, seq=seq)
    o_raw = _rwkv_recurrence(r, lw, k, v, a_in, b_in, batch=batch, seq=seq)

    x = _mix_tail(x, proj, p['b_gate'].reshape(2, d), o_a, p['w_oa'].astype(BF16), o_raw, r, k, v, g,
                  row(p['r_k']), row(p['lnx_w']), row(p['lnx_b']), e, e.T,
                  p['w_ob'].astype(BF16), p['w_o'].astype(BF16), row(p['n_mix_post']))

    width = MEM_HEADS * MEM_HEAD
    w_ckv = p['w_ckv'].reshape(d, MEM_HEADS, 2, MEM_HEAD).transpose(0, 2, 1, 3).reshape(d, 2 * width)
    kv_mem = _norm_matmul(mem2d, 0, d, row(p['n_mem']), w_ckv.astype(BF16), tm=512, tn=1024)
    w_cq = (p['w_cq'] * MEM_HEAD ** -0.5).astype(BF16)
    x = _xattn(x, row(p['n_x_pre']), w_cq, kv_mem, p['w_co'].astype(BF16), row(p['n_x_post']),
               batch=batch, seq=seq, n_mem=n_mem)

    return ffn(x, p['n_ffn2_pre'], p['w_ffn2_gate'], p['w_ffn2_up'], p['w_ffn2_down'], p['n_ffn2_post'])


_PARAM_NAMES = (
    'n_ffn1_pre', 'n_ffn1_post', 'w_ffn1_gate', 'w_ffn1_up', 'w_ffn1_down',
    'n_mix_pre', 'n_mix_post', 'w_in', 'b_gate',
    'n_q_lat', 'w_uq', 'n_kv_lat', 'w_ukv', 'w_oa',
    'mu_shift', 'w0', 'w_w2', 'a0', 'w_a2', 'w_g2', 'k_k', 'k_a', 'r_k', 'lnx_w', 'lnx_b', 'w_ob',
    'w_o',
    'n_x_pre', 'n_x_post', 'n_mem', 'w_cq', 'w_ckv', 'w_co',
    'n_ffn2_pre', 'n_ffn2_post', 'w_ffn2_gate', 'w_ffn2_up', 'w_ffn2_down')


def kernel(x, mem, positions, n_ffn1_pre, n_ffn1_post, w_ffn1_gate, w_ffn1_up, w_ffn1_down, n_mix_pre, n_mix_post, w_in, b_gate, n_q_lat, w_uq, n_kv_lat, w_ukv, w_oa, mu_shift, w0, w_w2, a0, w_a2, w_g2, k_k, k_a, r_k, lnx_w, lnx_b, w_ob, w_o, n_x_pre, n_x_post, n_mem, w_cq, w_ckv, w_co, n_ffn2_pre, n_ffn2_post, w_ffn2_gate, w_ffn2_up, w_ffn2_down):
    stacked = (n_ffn1_pre, n_ffn1_post, w_ffn1_gate, w_ffn1_up, w_ffn1_down, n_mix_pre, n_mix_post, w_in, b_gate,
               n_q_lat, w_uq, n_kv_lat, w_ukv, w_oa, mu_shift, w0, w_w2, a0, w_a2, w_g2, k_k, k_a, r_k, lnx_w,
               lnx_b, w_ob, w_o, n_x_pre, n_x_post, n_mem, w_cq, w_ckv, w_co, n_ffn2_pre, n_ffn2_post,
               w_ffn2_gate, w_ffn2_up, w_ffn2_down)
    batch, seq, d = x.shape
    n_mem_tokens = mem.shape[1]
    x2d = x.reshape(batch * seq, d)
    mem2d = mem.reshape(batch * n_mem_tokens, d)
    for layer in range(n_ffn1_pre.shape[0]):
        p = {name: arr[layer] for name, arr in zip(_PARAM_NAMES, stacked)}
        x2d = _layer(x2d, mem2d, positions, p, batch=batch, seq=seq, n_mem=n_mem_tokens)
    return x2d.reshape(batch, seq, d)
```

```python
import functools

import jax
import jax.numpy as jnp
from jax import lax
from jax.experimental import pallas as pl
from jax.experimental.pallas import tpu as pltpu

F32 = jnp.float32
BF16 = jnp.bfloat16

D_MODEL = 2048
D_FF = 5504
CHUNK = 64
EPS = 1e-6

MLA_HEADS = 8
QK_NOPE = 128
QK_ROPE = 64
V_HEAD = 128
Q_LORA = 512
KV_LORA = 256
ROPE_THETA = 10000.0

RWKV_HEADS = 16
RWKV_HEAD = 64
RWKV_WIDTH = RWKV_HEADS * RWKV_HEAD
DECAY_LORA = 64
A_LORA = 64
GATE_LORA = 128
LNX_EPS = 64e-5

MEM_HEADS = 4
MEM_HEAD = 256

MLA_COLS = Q_LORA + KV_LORA + QK_ROPE
RWKV_COLS = 3 * RWKV_WIDTH + DECAY_LORA + A_LORA + GATE_LORA

LANES = 128
MXU_DIM = 256
VMEM_LIMIT_BYTES = 56 * 1024 * 1024

COL_GATE_A = 0
COL_GATE_B = 2048
COL_R = 4096
COL_K = 5120
COL_V = 6144
COL_CQ = 7168
COL_CKV = 7680
COL_SMALL = 7936
COL_KROPE = 8192
COL_KROPE_SWAP = 8320
IN_COLS_PADDED = 8448

NN = (((1,), (0,)), ((), ()))
NT = (((1,), (1,)), ((), ()))
TN = (((0,), (0,)), ((), ()))


def _dot(a, b, dims=NN):
    return lax.dot_general(a.astype(BF16), b.astype(BF16), dims, preferred_element_type=F32)


def _each(f, *cols):
    return [f(*xs) for xs in zip(*cols)]


def _split2(x):
    hi = x.astype(BF16)
    lo = (x - hi.astype(F32)).astype(BF16)
    return hi, lo


def _dot_x3(a, b, dims=NN):
    ah, al = _split2(a)
    bh, bl = _split2(b)
    f = functools.partial(lax.dot_general, dimension_numbers=dims, preferred_element_type=F32)
    return f(ah, bh) + (f(ah, bl) + f(al, bh))


def _dot_exact_rhs(a, b_exact, dims=NN):
    ah, al = _split2(a)
    f = functools.partial(lax.dot_general, dimension_numbers=dims, preferred_element_type=F32)
    return f(ah, b_exact) + f(al, b_exact)


def _rms(xf, g, eps=EPS):
    return xf * lax.rsqrt(jnp.mean(xf * xf, axis=-1, keepdims=True) + eps) * g


def _sigmoid(x):
    return 1.0 / (1.0 + jnp.exp(-x))


def _params(*sem):
    return pltpu.CompilerParams(dimension_semantics=sem, vmem_limit_bytes=VMEM_LIMIT_BYTES)


def _resident(shape):
    return pl.BlockSpec(shape, lambda *_: (0,) * len(shape), pipeline_mode=pl.Buffered(1))


def _norm_matmul_kernel(x_ref, g_ref, w_ref, o_ref, h_ref):
    @pl.when(pl.program_id(1) == 0)
    def _():
        h_ref[...] = _rms(x_ref[...].astype(F32), g_ref[...]).astype(BF16)

    o_ref[...] = jnp.dot(h_ref[...], w_ref[...], preferred_element_type=F32).astype(o_ref.dtype)


def _norm_matmul(x, col_block, k_dim, g, w, *, tm, tn, out_dtype=BF16):
    m = x.shape[0]
    n = w.shape[1]
    tm = min(tm, m)
    return pl.pallas_call(
        _norm_matmul_kernel,
        grid=(m // tm, n // tn),
        in_specs=[
            pl.BlockSpec((tm, k_dim), lambda i, j: (i, col_block)),
            pl.BlockSpec((1, k_dim), lambda i, j: (0, 0)),
            pl.BlockSpec((k_dim, tn), lambda i, j: (0, j)),
        ],
        out_specs=pl.BlockSpec((tm, tn), lambda i, j: (i, j)),
        out_shape=jax.ShapeDtypeStruct((m, n), out_dtype),
        scratch_shapes=[pltpu.VMEM((tm, k_dim), BF16)],
        compiler_params=_params("parallel", "arbitrary"),
        name="norm_matmul",
    )(x, g, w)


IN_TN = 768
CQ_TILE, CQ_OFF = divmod(COL_CQ, IN_TN)
CKV_TILE, CKV_OFF = divmod(COL_CKV, IN_TN)
assert CQ_OFF + Q_LORA <= IN_TN and CKV_OFF + KV_LORA <= IN_TN and IN_COLS_PADDED % IN_TN == 0


def _in_proj_kernel(x_ref, g_ref, w_ref, gq_ref, wq_ref, gkv_ref, wkv_ref, o_ref, q_ref, kv_ref, h_ref):
    j = pl.program_id(1)

    @pl.when(j == 0)
    def _():
        h_ref[...] = _rms(x_ref[...], g_ref[...]).astype(BF16)

    y = jnp.dot(h_ref[...], w_ref[...], preferred_element_type=F32)
    o_ref[...] = y.astype(o_ref.dtype)

    def up_project(latent, gain_ref, weight_ref, out_ref):
        out_ref[...] = jnp.dot(_rms(latent, gain_ref[...]).astype(BF16), weight_ref[...],
                               preferred_element_type=F32).astype(out_ref.dtype)

    @pl.when(j == CQ_TILE)
    def _():
        up_project(y[:, CQ_OFF:CQ_OFF + Q_LORA], gq_ref, wq_ref, q_ref)

    @pl.when(j == CKV_TILE)
    def _():
        up_project(y[:, CKV_OFF:CKV_OFF + KV_LORA], gkv_ref, wkv_ref, kv_ref)


def _in_proj(x, g, w, g_q, w_uq, g_kv, w_ukv, *, tm=1024):
    m, d = x.shape
    tm = min(tm, m)
    nq, nkv = w_uq.shape[1], w_ukv.shape[1]
    return pl.pallas_call(
        _in_proj_kernel,
        grid=(m // tm, IN_COLS_PADDED // IN_TN),
        in_specs=[
            pl.BlockSpec((tm, d), lambda i, j: (i, 0)),
            _resident((1, d)),
            pl.BlockSpec((d, IN_TN), lambda i, j: (0, j)),
            _resident((1, Q_LORA)), _resident((Q_LORA, nq)),
            _resident((1, KV_LORA)), _resident((KV_LORA, nkv)),
        ],
        out_specs=[pl.BlockSpec((tm, IN_TN), lambda i, j: (i, j)),
                   pl.BlockSpec((tm, nq), lambda i, j: (i, 0)),
                   pl.BlockSpec((tm, nkv), lambda i, j: (i, 0))],
        out_shape=[jax.ShapeDtypeStruct((m, IN_COLS_PADDED), BF16),
                   jax.ShapeDtypeStruct((m, nq), BF16),
                   jax.ShapeDtypeStruct((m, nkv), BF16)],
        scratch_shapes=[pltpu.VMEM((tm, d), BF16)],
        compiler_params=_params("parallel", "arbitrary"),
        name="in_proj",
    )(x, g, w, g_q, w_uq, g_kv, w_ukv)


def _ffn_kernel(x_ref, gpre_ref, wg_ref, wu_ref, wd_ref, gpost_ref, o_ref, h_ref, acc_ref):
    j = pl.program_id(1)

    @pl.when(j == 0)
    def _():
        h_ref[...] = _rms(x_ref[...], gpre_ref[...]).astype(BF16)
        acc_ref[...] = jnp.zeros_like(acc_ref)

    h = h_ref[...]
    gate = jnp.dot(h, wg_ref[...], preferred_element_type=F32)
    up = jnp.dot(h, wu_ref[...], preferred_element_type=F32)
    act = (gate * _sigmoid(gate) * up).astype(BF16)
    acc_ref[...] += jnp.dot(act, wd_ref[...], preferred_element_type=F32)

    @pl.when(j == pl.num_programs(1) - 1)
    def _():
        o_ref[...] = x_ref[...] + 0.5 * _rms(acc_ref[...], gpost_ref[...])


def _ffn(x, g_pre, wg, wu, wd, g_post, *, tm=512, tf=512):
    m, d = x.shape
    fp = wg.shape[1]
    tm = min(tm, m)
    return pl.pallas_call(
        _ffn_kernel,
        grid=(m // tm, fp // tf),
        in_specs=[
            pl.BlockSpec((tm, d), lambda i, j: (i, 0)),
            pl.BlockSpec((1, d), lambda i, j: (0, 0)),
            pl.BlockSpec((d, tf), lambda i, j: (0, j)),
            pl.BlockSpec((d, tf), lambda i, j: (0, j)),
            pl.BlockSpec((tf, d), lambda i, j: (j, 0)),
            pl.BlockSpec((1, d), lambda i, j: (0, 0)),
        ],
        out_specs=pl.BlockSpec((tm, d), lambda i, j: (i, 0)),
        out_shape=jax.ShapeDtypeStruct((m, d), F32),
        scratch_shapes=[pltpu.VMEM((tm, d), BF16), pltpu.VMEM((tm, d), F32)],
        compiler_params=_params("parallel", "arbitrary"),
        name="ffn",
    )(x, g_pre, wg, wu, wd, g_post)


def _mla_attn_kernel(qn_ref, qr_ref, qs_ref, cq_ref, sq_ref, kv_ref, kr_ref, ks_ref, ck_ref, sk_ref,
                     o_ref, krot_ref, *, tq, tk, n_heads):
    qi = pl.program_id(2)
    heads = list(range(n_heads))
    kv_width = QK_NOPE + V_HEAD

    @pl.when(qi == 0)
    def _():
        krot_ref[...] = (kr_ref[...].astype(F32) * ck_ref[...]
                         + ks_ref[...].astype(F32) * sk_ref[...]).astype(BF16)

    cq, sq = cq_ref[...], sq_ref[...]
    q_rot = [qr_ref[:, p * LANES:(p + 1) * LANES].astype(F32) * cq
             + qs_ref[:, p * LANES:(p + 1) * LANES].astype(F32) * sq for p in range(n_heads // 2)]
    lane = lax.broadcasted_iota(jnp.int32, (tq, LANES), 1)
    own_half = [lane < QK_ROPE, lane >= QK_ROPE]
    q_cat = [jnp.concatenate([qn_ref[:, h * QK_NOPE:(h + 1) * QK_NOPE],
                              jnp.where(own_half[h % 2], q_rot[h // 2], 0.0).astype(BF16)], axis=1)
             for h in heads]

    def scores(start):
        k_rot = krot_ref[pl.ds(start, tk), :]
        return _each(lambda h, q_: lax.dot_general(
            q_, jnp.concatenate([kv_ref[pl.ds(start, tk), h * kv_width:h * kv_width + QK_NOPE], k_rot], axis=1),
            NT, preferred_element_type=F32), heads, q_cat)

    def update(carry, s, start):
        m, l, acc = carry
        m_new = _each(lambda m_, s_: jnp.maximum(m_, jnp.max(s_, axis=-1, keepdims=True)), m, s)
        alpha = _each(lambda m_, n_: jnp.exp(m_ - n_), m, m_new)
        p = _each(lambda s_, n_: jnp.exp(s_ - n_), s, m_new)
        l = _each(lambda a_, l_, p_: a_ * l_ + jnp.sum(p_, axis=-1, keepdims=True), alpha, l, p)
        pv = _each(lambda h, p_: jnp.dot(
            p_.astype(BF16), kv_ref[pl.ds(start, tk), h * kv_width + QK_NOPE:(h + 1) * kv_width],
            preferred_element_type=F32), heads, p)
        acc = _each(lambda a_, acc_, pv_: a_ * acc_ + pv_, alpha, acc, pv)
        return m_new, l, acc

    def body(kj, carry):
        start = pl.multiple_of(kj * tk, tk)
        return update(carry, scores(start), start)

    init = ([jnp.full((tq, 1), -1e30, F32) for _ in heads], [jnp.zeros((tq, 1), F32) for _ in heads],
            [jnp.zeros((tq, V_HEAD), F32) for _ in heads])
    carry = lax.fori_loop(0, qi * (tq // tk), body, init)

    q_chunk = lax.broadcasted_iota(jnp.int32, (tq, tk), 0) // CHUNK
    for sub in range(tq // tk):
        start = pl.multiple_of(qi * tq + sub * tk, tk)
        k_chunk = (lax.broadcasted_iota(jnp.int32, (tq, tk), 1) + sub * tk) // CHUNK
        visible = k_chunk <= q_chunk
        s = _each(lambda s_: jnp.where(visible, s_, -1e30), scores(start))
        carry = update(carry, s, start)
    m, l, acc = carry
    for h in heads:
        o_ref[:, h * V_HEAD:(h + 1) * V_HEAD] = (acc[h] / l[h]).astype(o_ref.dtype)


def _mla_attention(q_up, kv_up, proj, cos_tab, sin_tab, *, batch, seq, tq=256, tk=256, n_heads=4):
    nq = seq // tq
    rope_w = n_heads * QK_ROPE
    rope0 = MLA_HEADS * QK_NOPE // rope_w
    swap0 = (MLA_HEADS * QK_NOPE + MLA_HEADS * QK_ROPE) // rope_w
    q_rows = lambda width, col0: pl.BlockSpec((tq, width), lambda b, g, i: (b * nq + i, col0 + g))
    q_tab = pl.BlockSpec((tq, LANES), lambda b, g, i: (b * nq + i, 0))
    k_tab = pl.BlockSpec((seq, LANES), lambda b, g, i: (b, 0))
    k_cols = lambda col: pl.BlockSpec((seq, LANES), lambda b, g, i: (b, col // LANES))
    return pl.pallas_call(
        functools.partial(_mla_attn_kernel, tq=tq, tk=tk, n_heads=n_heads),
        grid=(batch, MLA_HEADS // n_heads, nq),
        in_specs=[
            q_rows(n_heads * QK_NOPE, 0), q_rows(rope_w, rope0), q_rows(rope_w, swap0), q_tab, q_tab,
            pl.BlockSpec((seq, n_heads * (QK_NOPE + V_HEAD)), lambda b, g, i: (b, g)),
            k_cols(COL_KROPE), k_cols(COL_KROPE_SWAP), k_tab, k_tab,
        ],
        out_specs=pl.BlockSpec((tq, n_heads * V_HEAD), lambda b, g, i: (b * nq + i, g)),
        out_shape=jax.ShapeDtypeStruct((batch * seq, MLA_HEADS * V_HEAD), BF16),
        scratch_shapes=[pltpu.VMEM((seq, LANES), BF16)],
        compiler_params=_params("parallel", "parallel", "arbitrary"),
        name="mla_attention",
    )(q_up, q_up, q_up, cos_tab, sin_tab, kv_up, proj, proj, cos_tab, sin_tab)


def _head_sum(x, e_ref, et_ref):
    s = _dot(x, e_ref[...])
    return _dot_exact_rhs(s, et_ref[...])


def _shift(y, prev_row, mu, first):
    rows = lax.broadcasted_iota(jnp.int32, y.shape, 0)
    prev0 = jnp.where(first, 0.0, prev_row)
    y_prev = jnp.where(rows == 0, prev0, pltpu.roll(y, 1, 0))
    return y + (y_prev - y) * mu


def _rwkv_prep_kernel(r_ref, k_ref, v_ref, s_ref, rp_ref, kp_ref, vp_ref, sp_ref,
                      mu_rkv_ref, mu_s_ref, w0_ref, ww2_ref, a0_ref, wa2_ref, wg2_ref,
                      kk_ref, ka_ref, e_ref, et_ref,
                      ro_ref, lw_ref, ko_ref, vo_ref, ao_ref, bo_ref, go_ref, *, tiles_per_seq):
    first = (pl.program_id(0) % tiles_per_seq) == 0
    last8 = slice(7, 8)

    def shifted(ref, pref, mu):
        return _shift(ref[...].astype(F32), pref[last8, :].astype(F32), mu, first)

    r = shifted(r_ref, rp_ref, mu_rkv_ref[0:1, :])
    k = shifted(k_ref, kp_ref, mu_rkv_ref[1:2, :])
    v = shifted(v_ref, vp_ref, mu_rkv_ref[2:3, :])
    small = shifted(s_ref, sp_ref, mu_s_ref[...])
    dw = small[:, 0:DECAY_LORA]
    da = small[:, DECAY_LORA:DECAY_LORA + A_LORA]
    dg = small[:, DECAY_LORA + A_LORA:]

    z = -(w0_ref[...] + _dot(jnp.tanh(dw), ww2_ref[...]))
    softplus = jnp.maximum(z, 0.0) + jnp.log(1.0 + jnp.exp(-jnp.abs(z)))
    w = -softplus - 0.5
    lw_ref[...] = -jnp.exp(w)
    a = _sigmoid(a0_ref[...] + _dot(da, wa2_ref[...]))
    go_ref[...] = _dot(_sigmoid(dg), wg2_ref[...]).astype(go_ref.dtype)

    kk = k * kk_ref[...]
    norm = jnp.maximum(jnp.sqrt(_head_sum(kk * kk, e_ref, et_ref)), 1e-12)
    kk = kk / norm
    ro_ref[...] = r.astype(ro_ref.dtype)
    ko_ref[...] = (k * (1.0 + (a - 1.0) * ka_ref[...])).astype(ko_ref.dtype)
    vo_ref[...] = v.astype(vo_ref.dtype)
    ao_ref[...] = (-kk).astype(ao_ref.dtype)
    bo_ref[...] = (kk * a).astype(bo_ref.dtype)


def _rwkv_prep(proj, mu_rkv, mu_s, w0, ww2, a0, wa2, wg2, k_k, k_a, e, et, *, seq, tm=256):
    t = proj.shape[0]
    c = RWKV_WIDTH
    small_w = DECAY_LORA + A_LORA + GATE_LORA

    def cur(width, col):
        return pl.BlockSpec((tm, width), lambda i: (i, col // width))

    def prev(width, col):
        return pl.BlockSpec((8, width), lambda i: (jnp.maximum(i * (tm // 8) - 1, 0), col // width))

    def const(shape):
        return pl.BlockSpec(shape, lambda i: (0, 0))

    out = jax.ShapeDtypeStruct((t, c), BF16)
    return pl.pallas_call(
        functools.partial(_rwkv_prep_kernel, tiles_per_seq=seq // tm),
        grid=(t // tm,),
        in_specs=[cur(c, COL_R), cur(c, COL_K), cur(c, COL_V), cur(small_w, COL_SMALL),
                  prev(c, COL_R), prev(c, COL_K), prev(c, COL_V), prev(small_w, COL_SMALL),
                  const((3, c)), const((1, small_w)), const((1, c)), const((DECAY_LORA, c)),
                  const((1, c)), const((A_LORA, c)), const((GATE_LORA, c)),
                  const((1, c)), const((1, c)), const((c, LANES)), const((LANES, c))],
        out_specs=[pl.BlockSpec((tm, c), lambda i: (i, 0))] * 7,
        out_shape=[out, jax.ShapeDtypeStruct((t, c), F32), out, out, out, out, out],
        compiler_params=_params("parallel"),
        name="rwkv_prep",
    )(proj, proj, proj, proj, proj, proj, proj, proj,
      mu_rkv, mu_s, w0, ww2, a0, wa2, wg2, k_k, k_a, e, et)


def _block_diag(x):
    lane = lax.broadcasted_iota(jnp.int32, x.shape, 1)
    zero = jnp.zeros_like(x)
    return jnp.concatenate([jnp.where(lane < RWKV_HEAD, x, zero),
                            jnp.where(lane >= RWKV_HEAD, x, zero)], axis=0)


def _unit_lower_inverse(a, eye, level_masks):
    square = lambda x: _dot(x, _block_diag(x))
    times = lambda t, x: t + _dot(t, _block_diag(x))
    base = _each(lambda x: x * level_masks[0], a)
    x2 = _each(square, base)
    x4 = _each(square, x2)
    t = _each(lambda x: eye + x, base)
    t = _each(times, t, x2)
    t = _each(times, t, x4)
    for mask in level_masks[1:]:
        ta = _each(lambda t_, a_: _dot(t_, _block_diag(a_ * mask)), t, a)
        t = _each(lambda t_, ta_: t_ + _dot(ta_, _block_diag(t_)), t, ta)
    return t


def _rwkv_chunk(r, lw, k, v, a, b, state, consts):
    ltri, strict, incl, eye, diag_blocks, level_masks = consts
    c = CHUNK
    rows = lambda x, y: jnp.concatenate([x, y], axis=0)

    def cumulative(x):
        hi, lo = _split2(x)
        return jnp.dot(ltri, hi, preferred_element_type=F32) + jnp.dot(ltri, lo, preferred_element_type=F32)

    cl = _each(cumulative, lw)
    cl_end = _each(lambda x: x[c - 1:c, :], cl)
    p_inv = _each(lambda x: jnp.exp(-x), cl)
    a_t = _each(lambda a_, cl_, lw_: a_ * jnp.exp(cl_ - lw_), a, cl, lw)
    r_t = _each(lambda r_, cl_: r_ * jnp.exp(cl_), r, cl)
    b_t = _each(jnp.multiply, b, p_inv)
    k_t = _each(jnp.multiply, k, p_inv)

    ar = _each(rows, a_t, r_t)
    g_b = _each(lambda x, y: _dot(x, _block_diag(y), NT), ar, b_t)
    g_k = _each(lambda x, y: _dot(x, _block_diag(y), NT), ar, k_t)
    a_ab = _each(lambda g: jnp.where(strict, g[:c], 0.0), g_b)
    a_rb = _each(lambda g: jnp.where(incl, g[c:], 0.0), g_b)
    a_k = _each(lambda g: rows(jnp.where(strict, g[:c], 0.0), jnp.where(incl, g[c:], 0.0)), g_k)

    t_inv = _unit_lower_inverse(a_ab, eye, level_masks)

    av = _each(lambda x, v_: _dot(x, _block_diag(v_)), a_k, v)
    tw = _each(lambda t_, a_, av_: _dot(t_, jnp.concatenate([_block_diag(a_), _block_diag(av_[:c])], axis=1)),
               t_inv, a_t, av)

    uo = _each(lambda tw_, r_, s_: _dot(rows(tw_[:, :LANES], r_), s_, NT), tw, r_t, state)
    u = _each(lambda uo_, tw_: uo_[:c] + tw_[:, LANES:], uo, tw)
    o = _each(lambda uo_, arb_, u_, av_: uo_[c:] + _dot(arb_, _block_diag(u_)) + av_[c:], uo, a_rb, u, av)
    p_tail = _each(lambda e_, cl_: jnp.exp(e_ - cl_), cl_end, cl)
    upd = _each(lambda u_, v_, b_, k_, p_: _dot(rows(u_, v_), rows(b_ * p_, k_ * p_), TN), u, v, b, k, p_tail)
    new_state = _each(lambda s_, e_, upd_: s_ * jnp.exp(e_) + jnp.where(diag_blocks, upd_, 0.0),
                      state, cl_end, upd)
    return o, new_state


def _rwkv_kernel(r_ref, lw_ref, k_ref, v_ref, a_ref, b_ref, o_ref, s_ref, *, n_chunks, n_pairs):
    @pl.when(pl.program_id(2) == 0)
    def _():
        s_ref[...] = jnp.zeros_like(s_ref)

    c = CHUNK
    row = lax.broadcasted_iota(jnp.int32, (c, LANES), 0)
    col = lax.broadcasted_iota(jnp.int32, (c, LANES), 1) % RWKV_HEAD
    ltri = jnp.where(lax.broadcasted_iota(jnp.int32, (c, c), 1) <= lax.broadcasted_iota(jnp.int32, (c, c), 0),
                     1.0, 0.0).astype(BF16)
    rr = lax.broadcasted_iota(jnp.int32, (LANES, LANES), 0) // RWKV_HEAD
    cc = lax.broadcasted_iota(jnp.int32, (LANES, LANES), 1) // RWKV_HEAD
    def same(block):
        return (row // block) == (col // block)
    level_masks = [jnp.where(same(8), 1.0, 0.0)]
    for block in (8, 16, 32):
        level_masks.append(jnp.where(same(2 * block), 1.0, 0.0) - jnp.where(same(block), 1.0, 0.0))
    eye = jnp.where(col == row, 1.0, 0.0)
    consts = (ltri, col < row, col <= row, eye, rr == cc, level_masks)

    def chunk(ci, carry):
        r0 = pl.multiple_of(ci * c, c)
        tiles = [(pl.ds(r0, c), slice(p * LANES, (p + 1) * LANES)) for p in range(n_pairs)]
        load = lambda ref: [ref[sl].astype(F32) for sl in tiles]
        o, s_new = _rwkv_chunk(load(r_ref), load(lw_ref), load(k_ref), load(v_ref), load(a_ref), load(b_ref),
                               [s_ref[p] for p in range(n_pairs)], consts)
        for p in range(n_pairs):
            s_ref[p] = s_new[p]
            o_ref[tiles[p]] = o[p]
        return carry

    lax.fori_loop(0, n_chunks, chunk, 0)


def _rwkv_recurrence(r, lw, k, v, a, b, *, batch, seq, tm=512, n_pairs=8):
    t, c = r.shape
    width = n_pairs * LANES
    tm = min(tm, seq)
    nt = seq // tm
    spec = pl.BlockSpec((tm, width), lambda bi, g, i: (bi * nt + i, g))
    return pl.pallas_call(
        functools.partial(_rwkv_kernel, n_chunks=tm // CHUNK, n_pairs=n_pairs),
        grid=(batch, c // width, nt),
        in_specs=[spec] * 6,
        out_specs=spec,
        out_shape=jax.ShapeDtypeStruct((t, c), F32),
        scratch_shapes=[pltpu.VMEM((n_pairs, LANES, LANES), F32)],
        compiler_params=_params("parallel", "parallel", "arbitrary"),
        name="rwkv_recurrence",
    )(r, lw, k, v, a, b)


def _mix_tail_kernel(x_ref, ga_ref, gb_ref, bg_ref, oa_ref, woa_ref,
                     o_ref, r_ref, k_ref, v_ref, g_ref, rk_ref, lw_ref, lb_ref, e_ref, et_ref,
                     wob_ref, wo_ref, gpost_ref, out_ref):
    inv_n = 1.0 / RWKV_HEAD
    o = o_ref[...]
    mean = _head_sum(o, e_ref, et_ref) * inv_n
    cen = o - mean
    var = _head_sum(cen * cen, e_ref, et_ref) * inv_n
    on = cen * lax.rsqrt(var + LNX_EPS) * lw_ref[...] + lb_ref[...]
    v = v_ref[...].astype(F32)
    rk = r_ref[...].astype(F32) * k_ref[...].astype(F32) * rk_ref[...]
    bonus = _head_sum(rk, e_ref, et_ref) * v
    ob = ((on + bonus) * g_ref[...].astype(F32)).astype(BF16)

    y_a = jnp.dot(oa_ref[...], woa_ref[...], preferred_element_type=F32)
    y_b = jnp.dot(ob, wob_ref[...], preferred_element_type=F32)
    gate_a = _sigmoid(ga_ref[...].astype(F32) + bg_ref[0:1, :])
    gate_b = _sigmoid(gb_ref[...].astype(F32) + bg_ref[1:2, :])
    merged = (gate_a * y_a + gate_b * y_b).astype(BF16)
    z = jnp.dot(merged, wo_ref[...], preferred_element_type=F32)
    out_ref[...] = x_ref[...] + _rms(z, gpost_ref[...])


def _mix_tail(x, proj, b_gate, o_a, w_oa, o_raw, r, k, v, g, r_k, lnx_w, lnx_b, e, et, w_ob, w_o, g_post, *, tm=256):
    t, d = x.shape
    c = RWKV_WIDTH
    row = lambda width, col=0: pl.BlockSpec((tm, width), lambda i: (i, col // width))
    return pl.pallas_call(
        _mix_tail_kernel,
        grid=(t // tm,),
        in_specs=[row(d), row(d, COL_GATE_A), row(d, COL_GATE_B), _resident((2, d)),
                  row(c), _resident((c, d)),
                  row(c), row(c), row(c), row(c), row(c),
                  _resident((1, c)), _resident((1, c)), _resident((1, c)),
                  _resident((c, LANES)), _resident((LANES, c)),
                  _resident((c, d)), _resident((d, d)), _resident((1, d))],
        out_specs=row(d),
        out_shape=jax.ShapeDtypeStruct((t, d), F32),
        compiler_params=_params("parallel"),
        name="mix_tail",
    )(x, proj, proj, b_gate, o_a, w_oa, o_raw, r, k, v, g, r_k, lnx_w, lnx_b, e, et, w_ob, w_o, g_post)


def _xattn_kernel(x_ref, gpre_ref, wq_ref, kv_ref, wo_ref, gpost_ref, out_ref):
    x = x_ref[...]
    h = _rms(x, gpre_ref[...]).astype(BF16)
    q = jnp.dot(h, wq_ref[...], preferred_element_type=F32).astype(BF16)
    width = MEM_HEADS * MEM_HEAD
    heads = []
    for hd in range(MEM_HEADS):
        lo = hd * MEM_HEAD
        k = kv_ref[:, lo:lo + MEM_HEAD]
        v = kv_ref[:, width + lo:width + lo + MEM_HEAD]
        s = lax.dot_general(q[:, lo:lo + MEM_HEAD], k, NT, preferred_element_type=F32)
        p = jnp.exp(s - jnp.max(s, axis=-1, keepdims=True))
        p = p / jnp.sum(p, axis=-1, keepdims=True)
        heads.append(jnp.dot(p.astype(BF16), v, preferred_element_type=F32).astype(BF16))
    o = jnp.concatenate(heads, axis=1)
    z = jnp.dot(o, wo_ref[...], preferred_element_type=F32)
    out_ref[...] = x + _rms(z, gpost_ref[...])


def _xattn(x, g_pre, w_cq, kv_mem, w_co, g_post, *, batch, seq, n_mem, tq=512):
    t, d = x.shape
    tq = min(tq, seq)
    nq = seq // tq
    width = MEM_HEADS * MEM_HEAD
    return pl.pallas_call(
        _xattn_kernel,
        grid=(batch, nq),
        in_specs=[pl.BlockSpec((tq, d), lambda b, i: (b * nq + i, 0)),
                  _resident((1, d)), _resident((d, width)),
                  pl.BlockSpec((n_mem, 2 * width), lambda b, i: (b, 0)),
                  _resident((width, d)), _resident((1, d))],
        out_specs=pl.BlockSpec((tq, d), lambda b, i: (b * nq + i, 0)),
        out_shape=jax.ShapeDtypeStruct((t, d), F32),
        compiler_params=_params("parallel", "arbitrary"),
        name="xattn",
    )(x, g_pre, w_cq, kv_mem, w_co, g_post)


def _pad_cols(w, n):
    return jnp.pad(w, ((0, 0), (0, n - w.shape[1])))


def _rope_tables(positions):
    half = QK_ROPE // 2
    inv = ROPE_THETA ** (-jnp.arange(half, dtype=F32) / half)
    ang = positions.astype(F32)[..., None] * inv
    return jnp.cos(ang), jnp.sin(ang)


def _layer(x, mem2d, positions, p, *, batch, seq, n_mem):
    row = lambda a: a.reshape(1, -1).astype(F32)
    c = RWKV_WIDTH
    d = D_MODEL
    ffp = -(-D_FF // (2 * MXU_DIM)) * (2 * MXU_DIM)

    def ffn(x, pre, gate, up, down, post):
        wg = _pad_cols(gate, ffp).astype(BF16)
        wu = _pad_cols(up, ffp).astype(BF16)
        wd = jnp.pad(down, ((0, ffp - D_FF), (0, 0))).astype(BF16)
        return _ffn(x, row(pre), wg, wu, wd, row(post))

    x = ffn(x, p['n_ffn1_pre'], p['w_ffn1_gate'], p['w_ffn1_up'], p['w_ffn1_down'], p['n_ffn1_post'])

    w_in = p['w_in']
    o_rwkv = MLA_COLS
    o_gate = MLA_COLS + RWKV_COLS
    half = QK_ROPE // 2
    w_kr = w_in[:, Q_LORA + KV_LORA:MLA_COLS]
    w_kr_swap = jnp.concatenate([w_kr[:, half:], w_kr[:, :half]], axis=1)
    w_in_r = jnp.concatenate([
        w_in[:, o_gate:],
        w_in[:, o_rwkv:o_rwkv + 3 * c],
        w_in[:, :Q_LORA + KV_LORA],
        w_in[:, o_rwkv + 3 * c:o_gate],
        w_kr, w_kr, w_kr_swap, w_kr_swap,
    ], axis=1).astype(BF16)
    scale = (QK_NOPE + QK_ROPE) ** -0.5
    w_uq = p['w_uq'].reshape(Q_LORA, MLA_HEADS, QK_NOPE + QK_ROPE) * scale
    w_qr = w_uq[:, :, QK_NOPE:]
    w_qr_swap = jnp.concatenate([w_qr[:, :, half:], w_qr[:, :, :half]], axis=2)
    w_uq_r = jnp.concatenate([w_uq[:, :, :QK_NOPE].reshape(Q_LORA, -1), w_qr.reshape(Q_LORA, -1),
                              w_qr_swap.reshape(Q_LORA, -1)], axis=1).astype(BF16)
    proj, q_up, kv_up = _in_proj(x, row(p['n_mix_pre']), w_in_r, row(p['n_q_lat']), w_uq_r,
                                 row(p['n_kv_lat']), p['w_ukv'].astype(BF16))

    cos, sin = _rope_tables(positions)
    cos_tab = jnp.concatenate([cos] * 4, axis=-1).reshape(batch * seq, LANES)
    sin_tab = jnp.concatenate([-sin, sin, -sin, sin], axis=-1).reshape(batch * seq, LANES)
    o_a = _mla_attention(q_up, kv_up, proj, cos_tab, sin_tab, batch=batch, seq=seq)

    head_of = jnp.arange(c) // RWKV_HEAD
    e = (head_of[:, None] == jnp.arange(LANES)[None, :]).astype(BF16)
    mu = p['mu_shift']
    r, lw, k, v, a_in, b_in, g = _rwkv_prep(
        proj, mu[:3 * c].reshape(3, c), row(mu[3 * c:]), row(p['w0']), p['w_w2'].astype(BF16),
        row(p['a0']), p['w_a2'].astype(BF16), p['w_g2'].astype(BF16), row(p['k_k']), row(p['k_a']),
        e, e.T, seq=seq)
    o_raw = _rwkv_recurrence(r, lw, k, v, a_in, b_in, batch=batch, seq=seq)

    x = _mix_tail(x, proj, p['b_gate'].reshape(2, d), o_a, p['w_oa'].astype(BF16), o_raw, r, k, v, g,
                  row(p['r_k']), row(p['lnx_w']), row(p['lnx_b']), e, e.T,
                  p['w_ob'].astype(BF16), p['w_o'].astype(BF16), row(p['n_mix_post']))

    width = MEM_HEADS * MEM_HEAD
    w_ckv = p['w_ckv'].reshape(d, MEM_HEADS, 2, MEM_HEAD).transpose(0, 2, 1, 3).reshape(d, 2 * width)
    kv_mem = _norm_matmul(mem2d, 0, d, row(p['n_mem']), w_ckv.astype(BF16), tm=512, tn=1024)
    w_cq = (p['w_cq'] * MEM_HEAD ** -0.5).astype(BF16)
    x = _xattn(x, row(p['n_x_pre']), w_cq, kv_mem, p['w_co'].astype(BF16), row(p['n_x_post']),
               batch=batch, seq=seq, n_mem=n_mem)

    return ffn(x, p['n_ffn2_pre'], p['w_ffn2_gate'], p['w_ffn2_up'], p['w_ffn2_down'], p['n_ffn2_post'])


_PARAM_NAMES = (
    'n_ffn1_pre', 'n_ffn1_post', 'w_ffn1_gate', 'w_ffn1_up', 'w_ffn1_down',
    'n_mix_pre', 'n_mix_post', 'w_in', 'b_gate',
    'n_q_lat', 'w_uq', 'n_kv_lat', 'w_ukv', 'w_oa',
    'mu_shift', 'w0', 'w_w2', 'a0', 'w_a2', 'w_g2', 'k_k', 'k_a', 'r_k', 'lnx_w', 'lnx_b', 'w_ob',
    'w_o',
    'n_x_pre', 'n_x_post', 'n_mem', 'w_cq', 'w_ckv', 'w_co',
    'n_ffn2_pre', 'n_ffn2_post', 'w_ffn2_gate', 'w_ffn2_up', 'w_ffn2_down')


def kernel(x, mem, positions, n_ffn1_pre, n_ffn1_post, w_ffn1_gate, w_ffn1_up, w_ffn1_down, n_mix_pre, n_mix_post, w_in, b_gate, n_q_lat, w_uq, n_kv_lat, w_ukv, w_oa, mu_shift, w0, w_w2, a0, w_a2, w_g2, k_k, k_a, r_k, lnx_w, lnx_b, w_ob, w_o, n_x_pre, n_x_post, n_mem, w_cq, w_ckv, w_co, n_ffn2_pre, n_ffn2_post, w_ffn2_gate, w_ffn2_up, w_ffn2_down):
    stacked = (n_ffn1_pre, n_ffn1_post, w_ffn1_gate, w_ffn1_up, w_ffn1_down, n_mix_pre, n_mix_post, w_in, b_gate,
               n_q_lat, w_uq, n_kv_lat, w_ukv, w_oa, mu_shift, w0, w_w2, a0, w_a2, w_g2, k_k, k_a, r_k, lnx_w,
               lnx_b, w_ob, w_o, n_x_pre, n_x_post, n_mem, w_cq, w_ckv, w_co, n_ffn2_pre, n_ffn2_post,
               w_ffn2_gate, w_ffn2_up, w_ffn2_down)
    batch, seq, d = x.shape
    n_mem_tokens = mem.shape[1]
    x2d = x.reshape(batch * seq, d)
    mem2d = mem.reshape(batch * n_mem_tokens, d)
    for layer in range(n_ffn1_pre.shape[0]):
        p = {name: arr[layer] for name, arr in zip(_PARAM_NAMES, stacked)}
        x2d = _layer(x2d, mem2d, positions, p, batch=batch, seq=seq, n_mem=n_mem_tokens)
    return x2d.reshape(batch, seq, d)
```

```python
import functools

import jax
import jax.numpy as jnp
from jax import lax
from jax.experimental import pallas as pl
from jax.experimental.pallas import tpu as pltpu

F32 = jnp.float32
BF16 = jnp.bfloat16

D_MODEL = 2048
D_FF = 5504
CHUNK = 64
EPS = 1e-6

MLA_HEADS = 8
QK_NOPE = 128
QK_ROPE = 64
V_HEAD = 128
Q_LORA = 512
KV_LORA = 256
ROPE_THETA = 10000.0

RWKV_HEADS = 16
RWKV_HEAD = 64
RWKV_WIDTH = RWKV_HEADS * RWKV_HEAD
DECAY_LORA = 64
A_LORA = 64
GATE_LORA = 128
LNX_EPS = 64e-5

MEM_HEADS = 4
MEM_HEAD = 256

MLA_COLS = Q_LORA + KV_LORA + QK_ROPE
RWKV_COLS = 3 * RWKV_WIDTH + DECAY_LORA + A_LORA + GATE_LORA

LANES = 128
MXU_DIM = 256
VMEM_LIMIT_BYTES = 56 * 1024 * 1024

COL_GATE_A = 0
COL_GATE_B = 2048
COL_R = 4096
COL_K = 5120
COL_V = 6144
COL_CQ = 7168
COL_CKV = 7680
COL_SMALL = 7936
COL_KROPE = 8192
COL_KROPE_SWAP = 8320
IN_COLS_PADDED = 8448

NN = (((1,), (0,)), ((), ()))
NT = (((1,), (1,)), ((), ()))
TN = (((0,), (0,)), ((), ()))


def _dot(a, b, dims=NN):
    return lax.dot_general(a.astype(BF16), b.astype(BF16), dims, preferred_element_type=F32)


def _each(f, *cols):
    return [f(*xs) for xs in zip(*cols)]


def _split2(x):
    hi = x.astype(BF16)
    lo = (x - hi.astype(F32)).astype(BF16)
    return hi, lo


def _dot_x3(a, b, dims=NN):
    ah, al = _split2(a)
    bh, bl = _split2(b)
    f = functools.partial(lax.dot_general, dimension_numbers=dims, preferred_element_type=F32)
    return f(ah, bh) + (f(ah, bl) + f(al, bh))


def _dot_exact_rhs(a, b_exact, dims=NN):
    ah, al = _split2(a)
    f = functools.partial(lax.dot_general, dimension_numbers=dims, preferred_element_type=F32)
    return f(ah, b_exact) + f(al, b_exact)


def _rms(xf, g, eps=EPS):
    return xf * lax.rsqrt(jnp.mean(xf * xf, axis=-1, keepdims=True) + eps) * g


def _sigmoid(x):
    return 1.0 / (1.0 + jnp.exp(-x))


def _params(*sem):
    return pltpu.CompilerParams(dimension_semantics=sem, vmem_limit_bytes=VMEM_LIMIT_BYTES)


def _resident(shape):
    return pl.BlockSpec(shape, lambda *_: (0,) * len(shape), pipeline_mode=pl.Buffered(1))


def _norm_matmul_kernel(x_ref, g_ref, w_ref, o_ref, h_ref):
    @pl.when(pl.program_id(1) == 0)
    def _():
        h_ref[...] = _rms(x_ref[...].astype(F32), g_ref[...]).astype(BF16)

    o_ref[...] = jnp.dot(h_ref[...], w_ref[...], preferred_element_type=F32).astype(o_ref.dtype)


def _norm_matmul(x, col_block, k_dim, g, w, *, tm, tn, out_dtype=BF16):
    m = x.shape[0]
    n = w.shape[1]
    tm = min(tm, m)
    return pl.pallas_call(
        _norm_matmul_kernel,
        grid=(m // tm, n // tn),
        in_specs=[
            pl.BlockSpec((tm, k_dim), lambda i, j: (i, col_block)),
            pl.BlockSpec((1, k_dim), lambda i, j: (0, 0)),
            pl.BlockSpec((k_dim, tn), lambda i, j: (0, j)),
        ],
        out_specs=pl.BlockSpec((tm, tn), lambda i, j: (i, j)),
        out_shape=jax.ShapeDtypeStruct((m, n), out_dtype),
        scratch_shapes=[pltpu.VMEM((tm, k_dim), BF16)],
        compiler_params=_params("parallel", "arbitrary"),
        name="norm_matmul",
    )(x, g, w)


IN_TN = 768
CQ_TILE, CQ_OFF = divmod(COL_CQ, IN_TN)
CKV_TILE, CKV_OFF = divmod(COL_CKV, IN_TN)
assert CQ_OFF + Q_LORA <= IN_TN and CKV_OFF + KV_LORA <= IN_TN and IN_COLS_PADDED % IN_TN == 0


def _in_proj_kernel(x_ref, g_ref, w_ref, gq_ref, wq_ref, gkv_ref, wkv_ref, o_ref, q_ref, kv_ref, h_ref):
    j = pl.program_id(1)

    @pl.when(j == 0)
    def _():
        h_ref[...] = _rms(x_ref[...], g_ref[...]).astype(BF16)

    y = jnp.dot(h_ref[...], w_ref[...], preferred_element_type=F32)
    o_ref[...] = y.astype(o_ref.dtype)

    def up_project(latent, gain_ref, weight_ref, out_ref):
        out_ref[...] = jnp.dot(_rms(latent, gain_ref[...]).astype(BF16), weight_ref[...],
                               preferred_element_type=F32).astype(out_ref.dtype)

    @pl.when(j == CQ_TILE)
    def _():
        up_project(y[:, CQ_OFF:CQ_OFF + Q_LORA], gq_ref, wq_ref, q_ref)

    @pl.when(j == CKV_TILE)
    def _():
        up_project(y[:, CKV_OFF:CKV_OFF + KV_LORA], gkv_ref, wkv_ref, kv_ref)


def _in_proj(x, g, w, g_q, w_uq, g_kv, w_ukv, *, tm=1024):
    m, d = x.shape
    tm = min(tm, m)
    nq, nkv = w_uq.shape[1], w_ukv.shape[1]
    return pl.pallas_call(
        _in_proj_kernel,
        grid=(m // tm, IN_COLS_PADDED // IN_TN),
        in_specs=[
            pl.BlockSpec((tm, d), lambda i, j: (i, 0)),
            _resident((1, d)),
            pl.BlockSpec((d, IN_TN), lambda i, j: (0, j)),
            _resident((1, Q_LORA)), _resident((Q_LORA, nq)),
            _resident((1, KV_LORA)), _resident((KV_LORA, nkv)),
        ],
        out_specs=[pl.BlockSpec((tm, IN_TN), lambda i, j: (i, j)),
                   pl.BlockSpec((tm, nq), lambda i, j: (i, 0)),
                   pl.BlockSpec((tm, nkv), lambda i, j: (i, 0))],
        out_shape=[jax.ShapeDtypeStruct((m, IN_COLS_PADDED), BF16),
                   jax.ShapeDtypeStruct((m, nq), BF16),
                   jax.ShapeDtypeStruct((m, nkv), BF16)],
        scratch_shapes=[pltpu.VMEM((tm, d), BF16)],
        compiler_params=_params("parallel", "arbitrary"),
        name="in_proj",
    )(x, g, w, g_q, w_uq, g_kv, w_ukv)


def _ffn_kernel(x_ref, gpre_ref, wg_ref, wu_ref, wd_ref, gpost_ref, o_ref, h_ref, acc_ref):
    j = pl.program_id(1)

    @pl.when(j == 0)
    def _():
        h_ref[...] = _rms(x_ref[...], gpre_ref[...]).astype(BF16)
        acc_ref[...] = jnp.zeros_like(acc_ref)

    h = h_ref[...]
    gate = jnp.dot(h, wg_ref[...], preferred_element_type=F32)
    up = jnp.dot(h, wu_ref[...], preferred_element_type=F32)
    act = (gate * _sigmoid(gate) * up).astype(BF16)
    acc_ref[...] += jnp.dot(act, wd_ref[...], preferred_element_type=F32)

    @pl.when(j == pl.num_programs(1) - 1)
    def _():
        o_ref[...] = x_ref[...] + 0.5 * _rms(acc_ref[...], gpost_ref[...])


def _ffn(x, g_pre, wg, wu, wd, g_post, *, tm=512, tf=512):
    m, d = x.shape
    fp = wg.shape[1]
    tm = min(tm, m)
    return pl.pallas_call(
        _ffn_kernel,
        grid=(m // tm, fp // tf),
        in_specs=[
            pl.BlockSpec((tm, d), lambda i, j: (i, 0)),
            pl.BlockSpec((1, d), lambda i, j: (0, 0)),
            pl.BlockSpec((d, tf), lambda i, j: (0, j)),
            pl.BlockSpec((d, tf), lambda i, j: (0, j)),
            pl.BlockSpec((tf, d), lambda i, j: (j, 0)),
            pl.BlockSpec((1, d), lambda i, j: (0, 0)),
        ],
        out_specs=pl.BlockSpec((tm, d), lambda i, j: (i, 0)),
        out_shape=jax.ShapeDtypeStruct((m, d), F32),
        scratch_shapes=[pltpu.VMEM((tm, d), BF16), pltpu.VMEM((tm, d), F32)],
        compiler_params=_params("parallel", "arbitrary"),
        name="ffn",
    )(x, g_pre, wg, wu, wd, g_post)


def _mla_attn_kernel(qn_ref, qr_ref, qs_ref, cq_ref, sq_ref, kv_ref, kr_ref, ks_ref, ck_ref, sk_ref,
                     o_ref, krot_ref, *, tq, tk, n_heads):
    qi = pl.program_id(2)
    heads = list(range(n_heads))
    kv_width = QK_NOPE + V_HEAD

    @pl.when(qi == 0)
    def _():
        krot_ref[...] = (kr_ref[...].astype(F32) * ck_ref[...]
                         + ks_ref[...].astype(F32) * sk_ref[...]).astype(BF16)

    cq, sq = cq_ref[...], sq_ref[...]
    q_rot = [qr_ref[:, p * LANES:(p + 1) * LANES].astype(F32) * cq
             + qs_ref[:, p * LANES:(p + 1) * LANES].astype(F32) * sq for p in range(n_heads // 2)]
    lane = lax.broadcasted_iota(jnp.int32, (tq, LANES), 1)
    own_half = [lane < QK_ROPE, lane >= QK_ROPE]
    q_cat = [jnp.concatenate([qn_ref[:, h * QK_NOPE:(h + 1) * QK_NOPE],
                              jnp.where(own_half[h % 2], q_rot[h // 2], 0.0).astype(BF16)], axis=1)
             for h in heads]

    def scores(start):
        k_rot = krot_ref[pl.ds(start, tk), :]
        return _each(lambda h, q_: lax.dot_general(
            jnp.concatenate([kv_ref[pl.ds(start, tk), h * kv_width:h * kv_width + QK_NOPE], k_rot], axis=1), q_,
            NT, preferred_element_type=F32), heads, q_cat)

    def update(carry, s, start):
        m, l, acc = carry
        m_new = _each(lambda m_, s_: jnp.maximum(m_, jnp.max(s_, axis=0, keepdims=True)), m, s)
        alpha = _each(lambda m_, n_: jnp.exp(m_ - n_), m, m_new)
        p = _each(lambda s_, n_: jnp.exp(s_ - n_), s, m_new)
        l = _each(lambda a_, l_, p_: a_ * l_ + jnp.sum(p_, axis=0, keepdims=True), alpha, l, p)
        pv = _each(lambda h, p_: lax.dot_general(
            kv_ref[pl.ds(start, tk), h * kv_width + QK_NOPE:(h + 1) * kv_width], p_.astype(BF16), TN,
            preferred_element_type=F32), heads, p)
        acc = _each(lambda a_, acc_, pv_: a_ * acc_ + pv_, alpha, acc, pv)
        return m_new, l, acc

    def body(kj, carry):
        start = pl.multiple_of(kj * tk, tk)
        return update(carry, scores(start), start)

    init = ([jnp.full((1, tq), -1e30, F32) for _ in heads], [jnp.zeros((1, tq), F32) for _ in heads],
            [jnp.zeros((V_HEAD, tq), F32) for _ in heads])
    carry = lax.fori_loop(0, qi * (tq // tk), body, init)

    q_chunk = lax.broadcasted_iota(jnp.int32, (tk, tq), 1) // CHUNK
    for sub in range(tq // tk):
        start = pl.multiple_of(qi * tq + sub * tk, tk)
        k_chunk = (lax.broadcasted_iota(jnp.int32, (tk, tq), 0) + sub * tk) // CHUNK
        visible = k_chunk <= q_chunk
        s = _each(lambda s_: jnp.where(visible, s_, -1e30), scores(start))
        carry = update(carry, s, start)
    m, l, acc = carry
    for h in heads:
        o_ref[:, h * V_HEAD:(h + 1) * V_HEAD] = (acc[h] / l[h]).T.astype(o_ref.dtype)


def _mla_attention(q_up, kv_up, proj, cos_tab, sin_tab, *, batch, seq, tq=256, tk=256, n_heads=MLA_HEADS):
    nq = seq // tq
    rope_w = n_heads * QK_ROPE
    rope0 = MLA_HEADS * QK_NOPE // rope_w
    swap0 = (MLA_HEADS * QK_NOPE + MLA_HEADS * QK_ROPE) // rope_w
    q_rows = lambda width, col0: pl.BlockSpec((tq, width), lambda b, g, i: (b * nq + i, col0 + g))
    q_tab = pl.BlockSpec((tq, LANES), lambda b, g, i: (b * nq + i, 0))
    k_tab = pl.BlockSpec((seq, LANES), lambda b, g, i: (b, 0))
    k_cols = lambda col: pl.BlockSpec((seq, LANES), lambda b, g, i: (b, col // LANES))
    return pl.pallas_call(
        functools.partial(_mla_attn_kernel, tq=tq, tk=tk, n_heads=n_heads),
        grid=(batch, MLA_HEADS // n_heads, nq),
        in_specs=[
            q_rows(n_heads * QK_NOPE, 0), q_rows(rope_w, rope0), q_rows(rope_w, swap0), q_tab, q_tab,
            pl.BlockSpec((seq, n_heads * (QK_NOPE + V_HEAD)), lambda b, g, i: (b, g)),
            k_cols(COL_KROPE), k_cols(COL_KROPE_SWAP), k_tab, k_tab,
        ],
        out_specs=pl.BlockSpec((tq, n_heads * V_HEAD), lambda b, g, i: (b * nq + i, g)),
        out_shape=jax.ShapeDtypeStruct((batch * seq, MLA_HEADS * V_HEAD), BF16),
        scratch_shapes=[pltpu.VMEM((seq, LANES), BF16)],
        compiler_params=_params("parallel", "parallel", "arbitrary"),
        name="mla_attention",
    )(q_up, q_up, q_up, cos_tab, sin_tab, kv_up, proj, proj, cos_tab, sin_tab)


def _head_sum(x, e_ref, et_ref):
    s = _dot(x, e_ref[...])
    return _dot_exact_rhs(s, et_ref[...])


def _shift(y, prev_row, mu, first):
    rows = lax.broadcasted_iota(jnp.int32, y.shape, 0)
    prev0 = jnp.where(first, 0.0, prev_row)
    y_prev = jnp.where(rows == 0, prev0, pltpu.roll(y, 1, 0))
    return y + (y_prev - y) * mu


def _rwkv_prep_kernel(r_ref, k_ref, v_ref, s_ref, rp_ref, kp_ref, vp_ref, sp_ref,
                      mu_rkv_ref, mu_s_ref, w0_ref, ww2_ref, a0_ref, wa2_ref, wg2_ref,
                      kk_ref, ka_ref, e_ref, et_ref,
                      ro_ref, lw_ref, ko_ref, vo_ref, ao_ref, bo_ref, go_ref, *, tiles_per_seq):
    first = (pl.program_id(0) % tiles_per_seq) == 0
    last8 = slice(7, 8)

    def shifted(ref, pref, mu):
        return _shift(ref[...].astype(F32), pref[last8, :].astype(F32), mu, first)

    r = shifted(r_ref, rp_ref, mu_rkv_ref[0:1, :])
    k = shifted(k_ref, kp_ref, mu_rkv_ref[1:2, :])
    v = shifted(v_ref, vp_ref, mu_rkv_ref[2:3, :])
    small = shifted(s_ref, sp_ref, mu_s_ref[...])
    dw = small[:, 0:DECAY_LORA]
    da = small[:, DECAY_LORA:DECAY_LORA + A_LORA]
    dg = small[:, DECAY_LORA + A_LORA:]

    z = -(w0_ref[...] + _dot(jnp.tanh(dw), ww2_ref[...]))
    softplus = jnp.maximum(z, 0.0) + jnp.log(1.0 + jnp.exp(-jnp.abs(z)))
    w = -softplus - 0.5
    lw_ref[...] = -jnp.exp(w)
    a = _sigmoid(a0_ref[...] + _dot(da, wa2_ref[...]))
    go_ref[...] = _dot(_sigmoid(dg), wg2_ref[...]).astype(go_ref.dtype)

    kk = k * kk_ref[...]
    norm = jnp.maximum(jnp.sqrt(_head_sum(kk * kk, e_ref, et_ref)), 1e-12)
    kk = kk / norm
    ro_ref[...] = r.astype(ro_ref.dtype)
    ko_ref[...] = (k * (1.0 + (a - 1.0) * ka_ref[...])).astype(ko_ref.dtype)
    vo_ref[...] = v.astype(vo_ref.dtype)
    ao_ref[...] = (-kk).astype(ao_ref.dtype)
    bo_ref[...] = (kk * a).astype(bo_ref.dtype)


def _rwkv_prep(proj, mu_rkv, mu_s, w0, ww2, a0, wa2, wg2, k_k, k_a, e, et, *, seq, tm=256):
    t = proj.shape[0]
    c = RWKV_WIDTH
    small_w = DECAY_LORA + A_LORA + GATE_LORA

    def cur(width, col):
        return pl.BlockSpec((tm, width), lambda i: (i, col // width))

    def prev(width, col):
        return pl.BlockSpec((8, width), lambda i: (jnp.maximum(i * (tm // 8) - 1, 0), col // width))

    def const(shape):
        return pl.BlockSpec(shape, lambda i: (0, 0))

    out = jax.ShapeDtypeStruct((t, c), BF16)
    return pl.pallas_call(
        functools.partial(_rwkv_prep_kernel, tiles_per_seq=seq // tm),
        grid=(t // tm,),
        in_specs=[cur(c, COL_R), cur(c, COL_K), cur(c, COL_V), cur(small_w, COL_SMALL),
                  prev(c, COL_R), prev(c, COL_K), prev(c, COL_V), prev(small_w, COL_SMALL),
                  const((3, c)), const((1, small_w)), const((1, c)), const((DECAY_LORA, c)),
                  const((1, c)), const((A_LORA, c)), const((GATE_LORA, c)),
                  const((1, c)), const((1, c)), const((c, LANES)), const((LANES, c))],
        out_specs=[pl.BlockSpec((tm, c), lambda i: (i, 0))] * 7,
        out_shape=[out, jax.ShapeDtypeStruct((t, c), F32), out, out, out, out, out],
        compiler_params=_params("parallel"),
        name="rwkv_prep",
    )(proj, proj, proj, proj, proj, proj, proj, proj,
      mu_rkv, mu_s, w0, ww2, a0, wa2, wg2, k_k, k_a, e, et)


def _block_diag(x):
    lane = lax.broadcasted_iota(jnp.int32, x.shape, 1)
    zero = jnp.zeros_like(x)
    return jnp.concatenate([jnp.where(lane < RWKV_HEAD, x, zero),
                            jnp.where(lane >= RWKV_HEAD, x, zero)], axis=0)


def _unit_lower_inverse(a, eye, level_masks):
    square = lambda x: _dot(x, _block_diag(x))
    times = lambda t, x: t + _dot(t, _block_diag(x))
    base = _each(lambda x: x * level_masks[0], a)
    x2 = _each(square, base)
    x4 = _each(square, x2)
    t = _each(lambda x: eye + x, base)
    t = _each(times, t, x2)
    t = _each(times, t, x4)
    for mask in level_masks[1:]:
        ta = _each(lambda t_, a_: _dot(t_, _block_diag(a_ * mask)), t, a)
        t = _each(lambda t_, ta_: t_ + _dot(ta_, _block_diag(t_)), t, ta)
    return t


def _chunk_prepare(r, lw, k, v, a, b, consts):
    ltri, strict, incl, eye, _, level_masks = consts
    c = CHUNK
    rows = lambda x, y: jnp.concatenate([x, y], axis=0)

    def cumulative(x):
        hi, lo = _split2(x)
        return jnp.dot(ltri, hi, preferred_element_type=F32) + jnp.dot(ltri, lo, preferred_element_type=F32)

    cl = _each(cumulative, lw)
    cl_end = _each(lambda x: x[c - 1:c, :], cl)
    p_inv = _each(lambda x: jnp.exp(-x), cl)
    a_t = _each(lambda a_, cl_, lw_: a_ * jnp.exp(cl_ - lw_), a, cl, lw)
    r_t = _each(lambda r_, cl_: r_ * jnp.exp(cl_), r, cl)
    b_t = _each(jnp.multiply, b, p_inv)
    k_t = _each(jnp.multiply, k, p_inv)

    ar = _each(rows, a_t, r_t)
    g_b = _each(lambda x, y: _dot(x, _block_diag(y), NT), ar, b_t)
    g_k = _each(lambda x, y: _dot(x, _block_diag(y), NT), ar, k_t)
    a_ab = _each(lambda g: jnp.where(strict, g[:c], 0.0), g_b)
    a_rb = _each(lambda g: jnp.where(incl, g[c:], 0.0), g_b)
    a_k = _each(lambda g: rows(jnp.where(strict, g[:c], 0.0), jnp.where(incl, g[c:], 0.0)), g_k)

    t_inv = _unit_lower_inverse(a_ab, eye, level_masks)

    av = _each(lambda x, v_: _dot(x, _block_diag(v_)), a_k, v)
    tw = _each(lambda t_, a_, av_: _dot(t_, jnp.concatenate([_block_diag(a_), _block_diag(av_[:c])], axis=1)),
               t_inv, a_t, av)
    lhs = _each(lambda tw_, r_: rows(tw_[:, :LANES], r_).astype(BF16), tw, r_t)
    w_eff = _each(lambda tw_: tw_[:, LANES:], tw)
    o_local = _each(lambda av_: av_[c:], av)
    p_tail = _each(lambda e_, cl_: jnp.exp(e_ - cl_), cl_end, cl)
    bk_tail = _each(lambda b_, k_, p_: rows(b_ * p_, k_ * p_).astype(BF16), b, k, p_tail)
    p_end = _each(jnp.exp, cl_end)
    return lhs, w_eff, a_rb, o_local, v, bk_tail, p_end


def _chunk_apply(prepared, state, diag_blocks):
    c = CHUNK
    lhs, w_eff, a_rb, o_local, v, bk_tail, p_end = prepared
    uo = _each(lambda lhs_, s_: _dot(lhs_, s_, NT), lhs, state)
    u = _each(lambda uo_, w_: uo_[:c] + w_, uo, w_eff)
    o = _each(lambda uo_, arb_, u_, ol_: uo_[c:] + _dot(arb_, _block_diag(u_)) + ol_, uo, a_rb, u, o_local)
    upd = _each(lambda u_, v_, bk_: _dot(jnp.concatenate([u_, v_], axis=0), bk_, TN), u, v, bk_tail)
    new_state = _each(lambda s_, p_, upd_: s_ * p_ + jnp.where(diag_blocks, upd_, 0.0), state, p_end, upd)
    return o, new_state


def _rwkv_kernel(r_ref, lw_ref, k_ref, v_ref, a_ref, b_ref, o_ref, s_ref, *, n_chunks, n_pairs, unroll):
    @pl.when(pl.program_id(2) == 0)
    def _():
        s_ref[...] = jnp.zeros_like(s_ref)

    c = CHUNK
    row = lax.broadcasted_iota(jnp.int32, (c, LANES), 0)
    col = lax.broadcasted_iota(jnp.int32, (c, LANES), 1) % RWKV_HEAD
    ltri = jnp.where(lax.broadcasted_iota(jnp.int32, (c, c), 1) <= lax.broadcasted_iota(jnp.int32, (c, c), 0),
                     1.0, 0.0).astype(BF16)
    rr = lax.broadcasted_iota(jnp.int32, (LANES, LANES), 0) // RWKV_HEAD
    cc = lax.broadcasted_iota(jnp.int32, (LANES, LANES), 1) // RWKV_HEAD
    def same(block):
        return (row // block) == (col // block)
    level_masks = [jnp.where(same(8), 1.0, 0.0)]
    for block in (8, 16, 32):
        level_masks.append(jnp.where(same(2 * block), 1.0, 0.0) - jnp.where(same(block), 1.0, 0.0))
    eye = jnp.where(col == row, 1.0, 0.0)
    consts = (ltri, col < row, col <= row, eye, rr == cc, level_masks)

    def chunk_group(gi, carry):
        r0 = pl.multiple_of(gi * (unroll * c), unroll * c)
        tiles = [(pl.ds(r0 + u * c, c), slice(p * LANES, (p + 1) * LANES))
                 for u in range(unroll) for p in range(n_pairs)]
        load = lambda ref: [ref[sl].astype(F32) for sl in tiles]
        prepared = _chunk_prepare(load(r_ref), load(lw_ref), load(k_ref), load(v_ref), load(a_ref), load(b_ref),
                                  consts)
        state = [s_ref[p] for p in range(n_pairs)]
        for u in range(unroll):
            part = slice(u * n_pairs, (u + 1) * n_pairs)
            o, state = _chunk_apply([x[part] for x in prepared], state, consts[4])
            for p in range(n_pairs):
                o_ref[tiles[u * n_pairs + p]] = o[p]
        for p in range(n_pairs):
            s_ref[p] = state[p]
        return carry

    lax.fori_loop(0, n_chunks // unroll, chunk_group, 0)


def _rwkv_recurrence(r, lw, k, v, a, b, *, batch, seq, tm=512, n_pairs=8, unroll=2):
    t, c = r.shape
    width = n_pairs * LANES
    tm = min(tm, seq)
    nt = seq // tm
    assert (tm // CHUNK) % unroll == 0
    spec = pl.BlockSpec((tm, width), lambda bi, g, i: (bi * nt + i, g))
    return pl.pallas_call(
        functools.partial(_rwkv_kernel, n_chunks=tm // CHUNK, n_pairs=n_pairs, unroll=unroll),
        grid=(batch, c // width, nt),
        in_specs=[spec] * 6,
        out_specs=spec,
        out_shape=jax.ShapeDtypeStruct((t, c), F32),
        scratch_shapes=[pltpu.VMEM((n_pairs, LANES, LANES), F32)],
        compiler_params=_params("parallel", "parallel", "arbitrary"),
        name="rwkv_recurrence",
    )(r, lw, k, v, a, b)


def _mix_tail_kernel(x_ref, ga_ref, gb_ref, bg_ref, oa_ref, woa_ref,
                     o_ref, r_ref, k_ref, v_ref, g_ref, rk_ref, lw_ref, lb_ref, e_ref, et_ref,
                     wob_ref, wo_ref, gpost_ref, out_ref):
    inv_n = 1.0 / RWKV_HEAD
    o = o_ref[...]
    mean = _head_sum(o, e_ref, et_ref) * inv_n
    cen = o - mean
    var = _head_sum(cen * cen, e_ref, et_ref) * inv_n
    on = cen * lax.rsqrt(var + LNX_EPS) * lw_ref[...] + lb_ref[...]
    v = v_ref[...].astype(F32)
    rk = r_ref[...].astype(F32) * k_ref[...].astype(F32) * rk_ref[...]
    bonus = _head_sum(rk, e_ref, et_ref) * v
    ob = ((on + bonus) * g_ref[...].astype(F32)).astype(BF16)

    y_a = jnp.dot(oa_ref[...], woa_ref[...], preferred_element_type=F32)
    y_b = jnp.dot(ob, wob_ref[...], preferred_element_type=F32)
    gate_a = _sigmoid(ga_ref[...].astype(F32) + bg_ref[0:1, :])
    gate_b = _sigmoid(gb_ref[...].astype(F32) + bg_ref[1:2, :])
    merged = (gate_a * y_a + gate_b * y_b).astype(BF16)
    z = jnp.dot(merged, wo_ref[...], preferred_element_type=F32)
    out_ref[...] = x_ref[...] + _rms(z, gpost_ref[...])


def _mix_tail(x, proj, b_gate, o_a, w_oa, o_raw, r, k, v, g, r_k, lnx_w, lnx_b, e, et, w_ob, w_o, g_post, *, tm=256):
    t, d = x.shape
    c = RWKV_WIDTH
    row = lambda width, col=0: pl.BlockSpec((tm, width), lambda i: (i, col // width))
    return pl.pallas_call(
        _mix_tail_kernel,
        grid=(t // tm,),
        in_specs=[row(d), row(d, COL_GATE_A), row(d, COL_GATE_B), _resident((2, d)),
                  row(c), _resident((c, d)),
                  row(c), row(c), row(c), row(c), row(c),
                  _resident((1, c)), _resident((1, c)), _resident((1, c)),
                  _resident((c, LANES)), _resident((LANES, c)),
                  _resident((c, d)), _resident((d, d)), _resident((1, d))],
        out_specs=row(d),
        out_shape=jax.ShapeDtypeStruct((t, d), F32),
        compiler_params=_params("parallel"),
        name="mix_tail",
    )(x, proj, proj, b_gate, o_a, w_oa, o_raw, r, k, v, g, r_k, lnx_w, lnx_b, e, et, w_ob, w_o, g_post)


def _xattn_kernel(x_ref, gpre_ref, wq_ref, kv_ref, wo_ref, gpost_ref, out_ref):
    x = x_ref[...]
    h = _rms(x, gpre_ref[...]).astype(BF16)
    q = jnp.dot(h, wq_ref[...], preferred_element_type=F32).astype(BF16)
    width = MEM_HEADS * MEM_HEAD
    heads = []
    for hd in range(MEM_HEADS):
        lo = hd * MEM_HEAD
        k = kv_ref[:, lo:lo + MEM_HEAD]
        v = kv_ref[:, width + lo:width + lo + MEM_HEAD]
        s = lax.dot_general(q[:, lo:lo + MEM_HEAD], k, NT, preferred_element_type=F32)
        p = jnp.exp(s - jnp.max(s, axis=-1, keepdims=True))
        p = p / jnp.sum(p, axis=-1, keepdims=True)
        heads.append(jnp.dot(p.astype(BF16), v, preferred_element_type=F32).astype(BF16))
    o = jnp.concatenate(heads, axis=1)
    z = jnp.dot(o, wo_ref[...], preferred_element_type=F32)
    out_ref[...] = x + _rms(z, gpost_ref[...])


def _xattn(x, g_pre, w_cq, kv_mem, w_co, g_post, *, batch, seq, n_mem, tq=512):
    t, d = x.shape
    tq = min(tq, seq)
    nq = seq // tq
    width = MEM_HEADS * MEM_HEAD
    return pl.pallas_call(
        _xattn_kernel,
        grid=(batch, nq),
        in_specs=[pl.BlockSpec((tq, d), lambda b, i: (b * nq + i, 0)),
                  _resident((1, d)), _resident((d, width)),
                  pl.BlockSpec((n_mem, 2 * width), lambda b, i: (b, 0)),
                  _resident((width, d)), _resident((1, d))],
        out_specs=pl.BlockSpec((tq, d), lambda b, i: (b * nq + i, 0)),
        out_shape=jax.ShapeDtypeStruct((t, d), F32),
        compiler_params=_params("parallel", "arbitrary"),
        name="xattn",
    )(x, g_pre, w_cq, kv_mem, w_co, g_post)


def _pad_cols(w, n):
    return jnp.pad(w, ((0, 0), (0, n - w.shape[1])))


def _rope_tables(positions):
    half = QK_ROPE // 2
    inv = ROPE_THETA ** (-jnp.arange(half, dtype=F32) / half)
    ang = positions.astype(F32)[..., None] * inv
    return jnp.cos(ang), jnp.sin(ang)


def _layer(x, mem2d, positions, p, *, batch, seq, n_mem):
    row = lambda a: a.reshape(1, -1).astype(F32)
    c = RWKV_WIDTH
    d = D_MODEL
    ffp = -(-D_FF // (2 * MXU_DIM)) * (2 * MXU_DIM)

    def ffn(x, pre, gate, up, down, post):
        wg = _pad_cols(gate, ffp).astype(BF16)
        wu = _pad_cols(up, ffp).astype(BF16)
        wd = jnp.pad(down, ((0, ffp - D_FF), (0, 0))).astype(BF16)
        return _ffn(x, row(pre), wg, wu, wd, row(post))

    x = ffn(x, p['n_ffn1_pre'], p['w_ffn1_gate'], p['w_ffn1_up'], p['w_ffn1_down'], p['n_ffn1_post'])

    w_in = p['w_in']
    o_rwkv = MLA_COLS
    o_gate = MLA_COLS + RWKV_COLS
    half = QK_ROPE // 2
    w_kr = w_in[:, Q_LORA + KV_LORA:MLA_COLS]
    w_kr_swap = jnp.concatenate([w_kr[:, half:], w_kr[:, :half]], axis=1)
    w_in_r = jnp.concatenate([
        w_in[:, o_gate:],
        w_in[:, o_rwkv:o_rwkv + 3 * c],
        w_in[:, :Q_LORA + KV_LORA],
        w_in[:, o_rwkv + 3 * c:o_gate],
        w_kr, w_kr, w_kr_swap, w_kr_swap,
    ], axis=1).astype(BF16)
    scale = (QK_NOPE + QK_ROPE) ** -0.5
    w_uq = p['w_uq'].reshape(Q_LORA, MLA_HEADS, QK_NOPE + QK_ROPE) * scale
    w_qr = w_uq[:, :, QK_NOPE:]
    w_qr_swap = jnp.concatenate([w_qr[:, :, half:], w_qr[:, :, :half]], axis=2)
    w_uq_r = jnp.concatenate([w_uq[:, :, :QK_NOPE].reshape(Q_LORA, -1), w_qr.reshape(Q_LORA, -1),
                              w_qr_swap.reshape(Q_LORA, -1)], axis=1).astype(BF16)
    proj, q_up, kv_up = _in_proj(x, row(p['n_mix_pre']), w_in_r, row(p['n_q_lat']), w_uq_r,
                                 row(p['n_kv_lat']), p['w_ukv'].astype(BF16))

    cos, sin = _rope_tables(positions)
    cos_tab = jnp.concatenate([cos] * 4, axis=-1).reshape(batch * seq, LANES)
    sin_tab = jnp.concatenate([-sin, sin, -sin, sin], axis=-1).reshape(batch * seq, LANES)
    o_a = _mla_attention(q_up, kv_up, proj, cos_tab, sin_tab, batch=batch, seq=seq)

    head_of = jnp.arange(c) // RWKV_HEAD
    e = (head_of[:, None] == jnp.arange(LANES)[None, :]).astype(BF16)
    mu = p['mu_shift']
    r, lw, k, v, a_in, b_in, g = _rwkv_prep(
        proj, mu[:3 * c].reshape(3, c), row(mu[3 * c:]), row(p['w0']), p['w_w2'].astype(BF16),
        row(p['a0']), p['w_a2'].astype(BF16), p['w_g2'].astype(BF16), row(p['k_k']), row(p['k_a']),
        e, e.T, seq=seq)
    o_raw = _rwkv_recurrence(r, lw, k, v, a_in, b_in, batch=batch, seq=seq)

    x = _mix_tail(x, proj, p['b_gate'].reshape(2, d), o_a, p['w_oa'].astype(BF16), o_raw, r, k, v, g,
                  row(p['r_k']), row(p['lnx_w']), row(p['lnx_b']), e, e.T,
                  p['w_ob'].astype(BF16), p['w_o'].astype(BF16), row(p['n_mix_post']))

    width = MEM_HEADS * MEM_HEAD
    w_ckv = p['w_ckv'].reshape(d, MEM_HEADS, 2, MEM_HEAD).transpose(0, 2, 1, 3).reshape(d, 2 * width)
    kv_mem = _norm_matmul(mem2d, 0, d, row(p['n_mem']), w_ckv.astype(BF16), tm=512, tn=1024)
    w_cq = (p['w_cq'] * MEM_HEAD ** -0.5).astype(BF16)
    x = _xattn(x, row(p['n_x_pre']), w_cq, kv_mem, p['w_co'].astype(BF16), row(p['n_x_post']),
               batch=batch, seq=seq, n_mem=n_mem)

    return ffn(x, p['n_ffn2_pre'], p['w_ffn2_gate'], p['w_ffn2_up'], p['w_ffn2_down'], p['n_ffn2_post'])


_PARAM_NAMES = (
    'n_ffn1_pre', 'n_ffn1_post', 'w_ffn1_gate', 'w_ffn1_up', 'w_ffn1_down',
    'n_mix_pre', 'n_mix_post', 'w_in', 'b_gate',
    'n_q_lat', 'w_uq', 'n_kv_lat', 'w_ukv', 'w_oa',
    'mu_shift', 'w0', 'w_w2', 'a0', 'w_a2', 'w_g2', 'k_k', 'k_a', 'r_k', 'lnx_w', 'lnx_b', 'w_ob',
    'w_o',
    'n_x_pre', 'n_x_post', 'n_mem', 'w_cq', 'w_ckv', 'w_co',
    'n_ffn2_pre', 'n_ffn2_post', 'w_ffn2_gate', 'w_ffn2_up', 'w_ffn2_down')


def kernel(x, mem, positions, n_ffn1_pre, n_ffn1_post, w_ffn1_gate, w_ffn1_up, w_ffn1_down, n_mix_pre, n_mix_post, w_in, b_gate, n_q_lat, w_uq, n_kv_lat, w_ukv, w_oa, mu_shift, w0, w_w2, a0, w_a2, w_g2, k_k, k_a, r_k, lnx_w, lnx_b, w_ob, w_o, n_x_pre, n_x_post, n_mem, w_cq, w_ckv, w_co, n_ffn2_pre, n_ffn2_post, w_ffn2_gate, w_ffn2_up, w_ffn2_down):
    stacked = (n_ffn1_pre, n_ffn1_post, w_ffn1_gate, w_ffn1_up, w_ffn1_down, n_mix_pre, n_mix_post, w_in, b_gate,
               n_q_lat, w_uq, n_kv_lat, w_ukv, w_oa, mu_shift, w0, w_w2, a0, w_a2, w_g2, k_k, k_a, r_k, lnx_w,
               lnx_b, w_ob, w_o, n_x_pre, n_x_post, n_mem, w_cq, w_ckv, w_co, n_ffn2_pre, n_ffn2_post,
               w_ffn2_gate, w_ffn2_up, w_ffn2_down)
    batch, seq, d = x.shape
    n_mem_tokens = mem.shape[1]
    x2d = x.reshape(batch * seq, d)
    mem2d = mem.reshape(batch * n_mem_tokens, d)
    for layer in range(n_ffn1_pre.shape[0]):
        p = {name: arr[layer] for name, arr in zip(_PARAM_NAMES, stacked)}
        x2d = _layer(x2d, mem2d, positions, p, batch=batch, seq=seq, n_mem=n_mem_tokens)
    return x2d.reshape(batch, seq, d)
```

```python
import functools

import jax
import jax.numpy as jnp
from jax import lax
from jax.experimental import pallas as pl
from jax.experimental.pallas import tpu as pltpu

F32 = jnp.float32
BF16 = jnp.bfloat16

D_MODEL = 2048
D_FF = 5504
CHUNK = 64
EPS = 1e-6

MLA_HEADS = 8
QK_NOPE = 128
QK_ROPE = 64
V_HEAD = 128
Q_LORA = 512
KV_LORA = 256
ROPE_THETA = 10000.0

RWKV_HEADS = 16
RWKV_HEAD = 64
RWKV_WIDTH = RWKV_HEADS * RWKV_HEAD
DECAY_LORA = 64
A_LORA = 64
GATE_LORA = 128
LNX_EPS = 64e-5
DECAY_SCALE = 0.6065306597126334

MEM_HEADS = 4
MEM_HEAD = 256

MLA_COLS = Q_LORA + KV_LORA + QK_ROPE
RWKV_COLS = 3 * RWKV_WIDTH + DECAY_LORA + A_LORA + GATE_LORA

LANES = 128
VMEM_LIMIT_BYTES = 56 * 1024 * 1024

COL_GATE_A = 0
COL_GATE_B = 2048
COL_R = 4096
COL_K = 5120
COL_V = 6144
COL_CQ = 7168
COL_CKV = 7680
COL_SMALL = 7936
COL_KROPE = 8192
COL_KROPE_SWAP = 8320
IN_COLS_PADDED = 8448

NN = (((1,), (0,)), ((), ()))
NT = (((1,), (1,)), ((), ()))
TN = (((0,), (0,)), ((), ()))


def _dot(a, b, dims=NN):
    return lax.dot_general(a.astype(BF16), b.astype(BF16), dims, preferred_element_type=F32)


def _each(f, *cols):
    return [f(*xs) for xs in zip(*cols)]


def _split2(x):
    hi = x.astype(BF16)
    lo = (x - hi.astype(F32)).astype(BF16)
    return hi, lo


def _rms(xf, g, eps=EPS):
    return xf * lax.rsqrt(jnp.mean(xf * xf, axis=-1, keepdims=True) + eps) * g


def _sigmoid(x):
    return 1.0 / (1.0 + jnp.exp(-x))


def _params(*sem):
    return pltpu.CompilerParams(dimension_semantics=sem, vmem_limit_bytes=VMEM_LIMIT_BYTES)


def _resident(shape):
    return pl.BlockSpec(shape, lambda *_: (0,) * len(shape), pipeline_mode=pl.Buffered(1))


def _norm_matmul_kernel(x_ref, g_ref, w_ref, o_ref, h_ref):
    @pl.when(pl.program_id(1) == 0)
    def _():
        h_ref[...] = _rms(x_ref[...].astype(F32), g_ref[...]).astype(BF16)

    o_ref[...] = jnp.dot(h_ref[...], w_ref[...], preferred_element_type=F32).astype(o_ref.dtype)


def _norm_matmul(x, col_block, k_dim, g, w, *, tm, tn, out_dtype=BF16):
    m = x.shape[0]
    n = w.shape[1]
    tm = min(tm, m)
    return pl.pallas_call(
        _norm_matmul_kernel,
        grid=(m // tm, n // tn),
        in_specs=[
            pl.BlockSpec((tm, k_dim), lambda i, j: (i, col_block)),
            pl.BlockSpec((1, k_dim), lambda i, j: (0, 0)),
            pl.BlockSpec((k_dim, tn), lambda i, j: (0, j)),
        ],
        out_specs=pl.BlockSpec((tm, tn), lambda i, j: (i, j)),
        out_shape=jax.ShapeDtypeStruct((m, n), out_dtype),
        scratch_shapes=[pltpu.VMEM((tm, k_dim), BF16)],
        compiler_params=_params("parallel", "arbitrary"),
        name="norm_matmul",
    )(x, g, w)


IN_TN = 768
CQ_TILE, CQ_OFF = divmod(COL_CQ, IN_TN)
CKV_TILE, CKV_OFF = divmod(COL_CKV, IN_TN)
assert CQ_OFF + Q_LORA <= IN_TN and CKV_OFF + KV_LORA <= IN_TN and IN_COLS_PADDED % IN_TN == 0


def _in_proj_kernel(x_ref, g_ref, w_ref, gq_ref, wq_ref, gkv_ref, wkv_ref, o_ref, q_ref, kv_ref, h_ref):
    j = pl.program_id(1)

    @pl.when(j == 0)
    def _():
        h_ref[...] = _rms(x_ref[...], g_ref[...]).astype(BF16)

    y = jnp.dot(h_ref[...], w_ref[...], preferred_element_type=F32)
    o_ref[...] = y.astype(o_ref.dtype)

    def up_project(latent, gain_ref, weight_ref, out_ref):
        out_ref[...] = jnp.dot(_rms(latent, gain_ref[...]).astype(BF16), weight_ref[...],
                               preferred_element_type=F32).astype(out_ref.dtype)

    @pl.when(j == CQ_TILE)
    def _():
        up_project(y[:, CQ_OFF:CQ_OFF + Q_LORA], gq_ref, wq_ref, q_ref)

    @pl.when(j == CKV_TILE)
    def _():
        up_project(y[:, CKV_OFF:CKV_OFF + KV_LORA], gkv_ref, wkv_ref, kv_ref)


def _in_proj(x, g, w, g_q, w_uq, g_kv, w_ukv, *, tm=1024):
    m, d = x.shape
    tm = min(tm, m)
    nq, nkv = w_uq.shape[1], w_ukv.shape[1]
    return pl.pallas_call(
        _in_proj_kernel,
        grid=(m // tm, IN_COLS_PADDED // IN_TN),
        in_specs=[
            pl.BlockSpec((tm, d), lambda i, j: (i, 0)),
            _resident((1, d)),
            pl.BlockSpec((d, IN_TN), lambda i, j: (0, j)),
            _resident((1, Q_LORA)), _resident((Q_LORA, nq)),
            _resident((1, KV_LORA)), _resident((KV_LORA, nkv)),
        ],
        out_specs=[pl.BlockSpec((tm, IN_TN), lambda i, j: (i, j)),
                   pl.BlockSpec((tm, nq), lambda i, j: (i, 0)),
                   pl.BlockSpec((tm, nkv), lambda i, j: (i, 0))],
        out_shape=[jax.ShapeDtypeStruct((m, IN_COLS_PADDED), BF16),
                   jax.ShapeDtypeStruct((m, nq), BF16),
                   jax.ShapeDtypeStruct((m, nkv), BF16)],
        scratch_shapes=[pltpu.VMEM((tm, d), BF16)],
        compiler_params=_params("parallel", "arbitrary"),
        name="in_proj",
    )(x, g, w, g_q, w_uq, g_kv, w_ukv)


def _ffn_kernel(x_ref, gpre_ref, wg_ref, wu_ref, wd_ref, gpost_ref, o_ref, h_ref, acc_ref, *, tail):
    j = pl.program_id(1)
    last = pl.num_programs(1) - 1
    tf = wg_ref.shape[1]

    @pl.when(j == 0)
    def _():
        h_ref[...] = _rms(x_ref[...], gpre_ref[...]).astype(BF16)
        acc_ref[...] = jnp.zeros_like(acc_ref)

    def accumulate(valid):
        h = h_ref[...]
        gate = jnp.dot(h, wg_ref[...], preferred_element_type=F32)
        up = jnp.dot(h, wu_ref[...], preferred_element_type=F32)
        act = (gate * _sigmoid(gate) * up).astype(BF16)
        wd = wd_ref[...]
        if valid < tf:
            act = jnp.where(lax.broadcasted_iota(jnp.int32, act.shape, 1) < valid, act, jnp.zeros_like(act))
            wd = jnp.where(lax.broadcasted_iota(jnp.int32, wd.shape, 0) < valid, wd, jnp.zeros_like(wd))
        acc_ref[...] += jnp.dot(act, wd, preferred_element_type=F32)

    if tail == tf:
        accumulate(tf)
    else:
        pl.when(j < last)(lambda: accumulate(tf))
        pl.when(j == last)(lambda: accumulate(tail))

    @pl.when(j == last)
    def _():
        o_ref[...] = x_ref[...] + 0.5 * _rms(acc_ref[...], gpost_ref[...])


def _ffn(x, g_pre, wg, wu, wd, g_post, *, tm=512, tf=512):
    m, d = x.shape
    f = wg.shape[1]
    n_tiles = pl.cdiv(f, tf)
    tm = min(tm, m)
    return pl.pallas_call(
        functools.partial(_ffn_kernel, tail=f - (n_tiles - 1) * tf),
        grid=(m // tm, n_tiles),
        in_specs=[
            pl.BlockSpec((tm, d), lambda i, j: (i, 0)),
            pl.BlockSpec((1, d), lambda i, j: (0, 0)),
            pl.BlockSpec((d, tf), lambda i, j: (0, j)),
            pl.BlockSpec((d, tf), lambda i, j: (0, j)),
            pl.BlockSpec((tf, d), lambda i, j: (j, 0)),
            pl.BlockSpec((1, d), lambda i, j: (0, 0)),
        ],
        out_specs=pl.BlockSpec((tm, d), lambda i, j: (i, 0)),
        out_shape=jax.ShapeDtypeStruct((m, d), F32),
        scratch_shapes=[pltpu.VMEM((tm, d), BF16), pltpu.VMEM((tm, d), F32)],
        compiler_params=_params("parallel", "arbitrary"),
        name="ffn",
    )(x, g_pre, wg, wu, wd, g_post)


def _mla_attn_kernel(qn_ref, qr_ref, qs_ref, cq_ref, sq_ref, kv_ref, kr_ref, ks_ref, ck_ref, sk_ref,
                     o_ref, krot_ref, *, tq, tk, n_heads):
    qi = pl.program_id(2)
    heads = list(range(n_heads))
    kv_width = QK_NOPE + V_HEAD

    @pl.when(qi == 0)
    def _():
        krot_ref[...] = (kr_ref[...].astype(F32) * ck_ref[...]
                         + ks_ref[...].astype(F32) * sk_ref[...]).astype(BF16)

    cq, sq = cq_ref[...], sq_ref[...]
    q_rot = [qr_ref[:, p * LANES:(p + 1) * LANES].astype(F32) * cq
             + qs_ref[:, p * LANES:(p + 1) * LANES].astype(F32) * sq for p in range(n_heads // 2)]
    lane = lax.broadcasted_iota(jnp.int32, (tq, LANES), 1)
    own_half = [lane < QK_ROPE, lane >= QK_ROPE]
    q_cat = [jnp.concatenate([qn_ref[:, h * QK_NOPE:(h + 1) * QK_NOPE],
                              jnp.where(own_half[h % 2], q_rot[h // 2], 0.0).astype(BF16)], axis=1)
             for h in heads]

    def scores(start):
        k_rot = krot_ref[pl.ds(start, tk), :]
        return _each(lambda h, q_: lax.dot_general(
            jnp.concatenate([kv_ref[pl.ds(start, tk), h * kv_width:h * kv_width + QK_NOPE], k_rot], axis=1), q_,
            NT, preferred_element_type=F32), heads, q_cat)

    def update(carry, s, start):
        m, l, acc = carry
        m_new = _each(lambda m_, s_: jnp.maximum(m_, jnp.max(s_, axis=0, keepdims=True)), m, s)
        alpha = _each(lambda m_, n_: jnp.exp(m_ - n_), m, m_new)
        p = _each(lambda s_, n_: jnp.exp(s_ - n_), s, m_new)
        l = _each(lambda a_, l_, p_: a_ * l_ + jnp.sum(p_, axis=0, keepdims=True), alpha, l, p)
        pv = _each(lambda h, p_: lax.dot_general(
            kv_ref[pl.ds(start, tk), h * kv_width + QK_NOPE:(h + 1) * kv_width], p_.astype(BF16), TN,
            preferred_element_type=F32), heads, p)
        acc = _each(lambda a_, acc_, pv_: a_ * acc_ + pv_, alpha, acc, pv)
        return m_new, l, acc

    def body(kj, carry):
        start = pl.multiple_of(kj * tk, tk)
        return update(carry, scores(start), start)

    init = ([jnp.full((1, tq), -1e30, F32) for _ in heads], [jnp.zeros((1, tq), F32) for _ in heads],
            [jnp.zeros((V_HEAD, tq), F32) for _ in heads])
    carry = lax.fori_loop(0, qi * (tq // tk), body, init)

    q_chunk = lax.broadcasted_iota(jnp.int32, (tk, tq), 1) // CHUNK
    for sub in range(tq // tk):
        start = pl.multiple_of(qi * tq + sub * tk, tk)
        k_chunk = (lax.broadcasted_iota(jnp.int32, (tk, tq), 0) + sub * tk) // CHUNK
        visible = k_chunk <= q_chunk
        s = _each(lambda s_: jnp.where(visible, s_, -1e30), scores(start))
        carry = update(carry, s, start)
    m, l, acc = carry
    for h in heads:
        o_ref[:, h * V_HEAD:(h + 1) * V_HEAD] = (acc[h] / l[h]).T.astype(o_ref.dtype)


def _mla_attention(q_up, kv_up, proj, cos_tab, sin_tab, *, batch, seq, tq=256, tk=256, n_heads=MLA_HEADS):
    nq = seq // tq
    rope_w = n_heads * QK_ROPE
    rope0 = MLA_HEADS * QK_NOPE // rope_w
    swap0 = (MLA_HEADS * QK_NOPE + MLA_HEADS * QK_ROPE) // rope_w
    q_rows = lambda width, col0: pl.BlockSpec((tq, width), lambda b, g, i: (b * nq + i, col0 + g))
    q_tab = pl.BlockSpec((tq, LANES), lambda b, g, i: (b * nq + i, 0))
    k_tab = pl.BlockSpec((seq, LANES), lambda b, g, i: (b, 0))
    k_cols = lambda col: pl.BlockSpec((seq, LANES), lambda b, g, i: (b, col // LANES))
    return pl.pallas_call(
        functools.partial(_mla_attn_kernel, tq=tq, tk=tk, n_heads=n_heads),
        grid=(batch, MLA_HEADS // n_heads, nq),
        in_specs=[
            q_rows(n_heads * QK_NOPE, 0), q_rows(rope_w, rope0), q_rows(rope_w, swap0), q_tab, q_tab,
            pl.BlockSpec((seq, n_heads * (QK_NOPE + V_HEAD)), lambda b, g, i: (b, g)),
            k_cols(COL_KROPE), k_cols(COL_KROPE_SWAP), k_tab, k_tab,
        ],
        out_specs=pl.BlockSpec((tq, n_heads * V_HEAD), lambda b, g, i: (b * nq + i, g)),
        out_shape=jax.ShapeDtypeStruct((batch * seq, MLA_HEADS * V_HEAD), BF16),
        scratch_shapes=[pltpu.VMEM((seq, LANES), BF16)],
        compiler_params=_params("parallel", "parallel", "arbitrary"),
        name="mla_attention",
    )(q_up, q_up, q_up, cos_tab, sin_tab, kv_up, proj, proj, cos_tab, sin_tab)


def _head_sum(x, e_ref, et_ref):
    s = _dot(x, e_ref[...])
    return _dot(s, et_ref[...])


def _shift(y, prev_row, mu, first):
    rows = lax.broadcasted_iota(jnp.int32, y.shape, 0)
    prev0 = jnp.where(first, 0.0, prev_row)
    y_prev = jnp.where(rows == 0, prev0, pltpu.roll(y, 1, 0))
    return y + (y_prev - y) * mu


def _rwkv_prep_kernel(r_ref, k_ref, v_ref, s_ref, rp_ref, kp_ref, vp_ref, sp_ref,
                      mu_rkv_ref, mu_s_ref, w0_ref, ww2_ref, a0_ref, wa2_ref, wg2_ref,
                      kk_ref, ka_ref, e_ref, et_ref,
                      ro_ref, lw_ref, ko_ref, vo_ref, ao_ref, bo_ref, go_ref, *, tiles_per_seq):
    first = (pl.program_id(0) % tiles_per_seq) == 0
    last8 = slice(7, 8)

    def shifted(ref, pref, mu):
        return _shift(ref[...].astype(F32), pref[last8, :].astype(F32), mu, first)

    r = shifted(r_ref, rp_ref, mu_rkv_ref[0:1, :])
    k = shifted(k_ref, kp_ref, mu_rkv_ref[1:2, :])
    v = shifted(v_ref, vp_ref, mu_rkv_ref[2:3, :])
    small = shifted(s_ref, sp_ref, mu_s_ref[...])
    dw = small[:, 0:DECAY_LORA]
    da = small[:, DECAY_LORA:DECAY_LORA + A_LORA]
    dg = small[:, DECAY_LORA + A_LORA:]

    u = w0_ref[...] + _dot(jnp.tanh(dw), ww2_ref[...])
    lw_ref[...] = -DECAY_SCALE * _sigmoid(u)
    a = _sigmoid(a0_ref[...] + _dot(da, wa2_ref[...]))
    go_ref[...] = _dot(_sigmoid(dg), wg2_ref[...]).astype(go_ref.dtype)

    kk = k * kk_ref[...]
    kk = kk * lax.rsqrt(jnp.maximum(_head_sum(kk * kk, e_ref, et_ref), 1e-24))
    ro_ref[...] = r.astype(ro_ref.dtype)
    ko_ref[...] = (k * (1.0 + (a - 1.0) * ka_ref[...])).astype(ko_ref.dtype)
    vo_ref[...] = v.astype(vo_ref.dtype)
    ao_ref[...] = (-kk).astype(ao_ref.dtype)
    bo_ref[...] = (kk * a).astype(bo_ref.dtype)


def _rwkv_prep(proj, mu_rkv, mu_s, w0, ww2, a0, wa2, wg2, k_k, k_a, e, et, *, seq, tm=256):
    t = proj.shape[0]
    c = RWKV_WIDTH
    small_w = DECAY_LORA + A_LORA + GATE_LORA

    def cur(width, col):
        return pl.BlockSpec((tm, width), lambda i: (i, col // width))

    def prev(width, col):
        return pl.BlockSpec((8, width), lambda i: (jnp.maximum(i * (tm // 8) - 1, 0), col // width))

    def const(shape):
        return pl.BlockSpec(shape, lambda i: (0, 0))

    out = jax.ShapeDtypeStruct((t, c), BF16)
    return pl.pallas_call(
        functools.partial(_rwkv_prep_kernel, tiles_per_seq=seq // tm),
        grid=(t // tm,),
        in_specs=[cur(c, COL_R), cur(c, COL_K), cur(c, COL_V), cur(small_w, COL_SMALL),
                  prev(c, COL_R), prev(c, COL_K), prev(c, COL_V), prev(small_w, COL_SMALL),
                  const((3, c)), const((1, small_w)), const((1, c)), const((DECAY_LORA, c)),
                  const((1, c)), const((A_LORA, c)), const((GATE_LORA, c)),
                  const((1, c)), const((1, c)), const((c, LANES)), const((LANES, c))],
        out_specs=[pl.BlockSpec((tm, c), lambda i: (i, 0))] * 7,
        out_shape=[out, jax.ShapeDtypeStruct((t, c), F32), out, out, out, out, out],
        compiler_params=_params("parallel"),
        name="rwkv_prep",
    )(proj, proj, proj, proj, proj, proj, proj, proj,
      mu_rkv, mu_s, w0, ww2, a0, wa2, wg2, k_k, k_a, e, et)


def _block_diag(x):
    lane = lax.broadcasted_iota(jnp.int32, x.shape, 1)
    zero = jnp.zeros_like(x)
    return jnp.concatenate([jnp.where(lane < RWKV_HEAD, x, zero),
                            jnp.where(lane >= RWKV_HEAD, x, zero)], axis=0)


def _unit_lower_inverse(a, eye, level_masks):
    square = lambda x: _dot(x, _block_diag(x))
    times = lambda t, x: t + _dot(t, _block_diag(x))
    base = _each(lambda x: x * level_masks[0], a)
    x2 = _each(square, base)
    x4 = _each(square, x2)
    t = _each(lambda x: eye + x, base)
    t = _each(times, t, x2)
    t = _each(times, t, x4)
    for mask in level_masks[1:]:
        ta = _each(lambda t_, a_: _dot(t_, _block_diag(a_ * mask)), t, a)
        t = _each(lambda t_, ta_: t_ + _dot(ta_, _block_diag(t_)), t, ta)
    return t


def _chunk_prepare(r, lw, k, v, a, b, consts):
    ltri, strict, incl, eye, _, level_masks = consts
    c = CHUNK
    rows = lambda x, y: jnp.concatenate([x, y], axis=0)

    def cumulative(x):
        hi, lo = _split2(x)
        return jnp.dot(ltri, hi, preferred_element_type=F32) + jnp.dot(ltri, lo, preferred_element_type=F32)

    cl = _each(cumulative, lw)
    cl_end = _each(lambda x: x[c - 1:c, :], cl)
    p_inv = _each(lambda x: jnp.exp(-x), cl)
    a_t = _each(lambda a_, cl_, lw_: a_ * jnp.exp(cl_ - lw_), a, cl, lw)
    r_t = _each(lambda r_, cl_: r_ * jnp.exp(cl_), r, cl)
    b_t = _each(jnp.multiply, b, p_inv)
    k_t = _each(jnp.multiply, k, p_inv)

    ar = _each(rows, a_t, r_t)
    g_b = _each(lambda x, y: _dot(x, _block_diag(y), NT), ar, b_t)
    g_k = _each(lambda x, y: _dot(x, _block_diag(y), NT), ar, k_t)
    a_ab = _each(lambda g: jnp.where(strict, g[:c], 0.0), g_b)
    a_rb = _each(lambda g: jnp.where(incl, g[c:], 0.0), g_b)
    a_k = _each(lambda g: rows(jnp.where(strict, g[:c], 0.0), jnp.where(incl, g[c:], 0.0)), g_k)

    t_inv = _unit_lower_inverse(a_ab, eye, level_masks)

    av = _each(lambda x, v_: _dot(x, _block_diag(v_)), a_k, v)
    tw = _each(lambda t_, a_, av_: _dot(t_, jnp.concatenate([_block_diag(a_), _block_diag(av_[:c])], axis=1)),
               t_inv, a_t, av)
    lhs = _each(lambda tw_, r_: rows(tw_[:, :LANES], r_).astype(BF16), tw, r_t)
    w_eff = _each(lambda tw_: tw_[:, LANES:], tw)
    o_local = _each(lambda av_: av_[c:], av)
    p_tail = _each(lambda e_, cl_: jnp.exp(e_ - cl_), cl_end, cl)
    bk_tail = _each(lambda b_, k_, p_: rows(b_ * p_, k_ * p_).astype(BF16), b, k, p_tail)
    p_end = _each(jnp.exp, cl_end)
    return lhs, w_eff, a_rb, o_local, v, bk_tail, p_end


def _chunk_apply(prepared, state, diag_blocks):
    c = CHUNK
    lhs, w_eff, a_rb, o_local, v, bk_tail, p_end = prepared
    uo = _each(lambda lhs_, s_: _dot(lhs_, s_, NT), lhs, state)
    u = _each(lambda uo_, w_: uo_[:c] + w_, uo, w_eff)
    o = _each(lambda uo_, arb_, u_, ol_: uo_[c:] + _dot(arb_, _block_diag(u_)) + ol_, uo, a_rb, u, o_local)
    upd = _each(lambda u_, v_, bk_: _dot(jnp.concatenate([u_, v_], axis=0), bk_, TN), u, v, bk_tail)
    new_state = _each(lambda s_, p_, upd_: s_ * p_ + jnp.where(diag_blocks, upd_, 0.0), state, p_end, upd)
    return o, new_state


def _rwkv_kernel(r_ref, lw_ref, k_ref, v_ref, a_ref, b_ref, o_ref, s_ref, *, n_chunks, n_pairs, unroll):
    @pl.when(pl.program_id(2) == 0)
    def _():
        s_ref[...] = jnp.zeros_like(s_ref)

    c = CHUNK
    row = lax.broadcasted_iota(jnp.int32, (c, LANES), 0)
    col = lax.broadcasted_iota(jnp.int32, (c, LANES), 1) % RWKV_HEAD
    ltri = jnp.where(lax.broadcasted_iota(jnp.int32, (c, c), 1) <= lax.broadcasted_iota(jnp.int32, (c, c), 0),
                     1.0, 0.0).astype(BF16)
    rr = lax.broadcasted_iota(jnp.int32, (LANES, LANES), 0) // RWKV_HEAD
    cc = lax.broadcasted_iota(jnp.int32, (LANES, LANES), 1) // RWKV_HEAD
    def same(block):
        return (row // block) == (col // block)
    level_masks = [jnp.where(same(8), 1.0, 0.0)]
    for block in (8, 16, 32):
        level_masks.append(jnp.where(same(2 * block), 1.0, 0.0) - jnp.where(same(block), 1.0, 0.0))
    eye = jnp.where(col == row, 1.0, 0.0)
    consts = (ltri, col < row, col <= row, eye, rr == cc, level_masks)

    def chunk_group(gi, carry):
        r0 = pl.multiple_of(gi * (unroll * c), unroll * c)
        tiles = [(pl.ds(r0 + u * c, c), slice(p * LANES, (p + 1) * LANES))
                 for u in range(unroll) for p in range(n_pairs)]
        load = lambda ref: [ref[sl].astype(F32) for sl in tiles]
        prepared = _chunk_prepare(load(r_ref), load(lw_ref), load(k_ref), load(v_ref), load(a_ref), load(b_ref),
                                  consts)
        state = [s_ref[p] for p in range(n_pairs)]
        for u in range(unroll):
            part = slice(u * n_pairs, (u + 1) * n_pairs)
            o, state = _chunk_apply([x[part] for x in prepared], state, consts[4])
            for p in range(n_pairs):
                o_ref[tiles[u * n_pairs + p]] = o[p]
        for p in range(n_pairs):
            s_ref[p] = state[p]
        return carry

    lax.fori_loop(0, n_chunks // unroll, chunk_group, 0)


def _rwkv_recurrence(r, lw, k, v, a, b, *, batch, seq, tm=512, n_pairs=8, unroll=2):
    t, c = r.shape
    width = n_pairs * LANES
    tm = min(tm, seq)
    nt = seq // tm
    assert (tm // CHUNK) % unroll == 0
    spec = pl.BlockSpec((tm, width), lambda bi, g, i: (bi * nt + i, g))
    return pl.pallas_call(
        functools.partial(_rwkv_kernel, n_chunks=tm // CHUNK, n_pairs=n_pairs, unroll=unroll),
        grid=(batch, c // width, nt),
        in_specs=[spec] * 6,
        out_specs=spec,
        out_shape=jax.ShapeDtypeStruct((t, c), F32),
        scratch_shapes=[pltpu.VMEM((n_pairs, LANES, LANES), F32)],
        compiler_params=_params("parallel", "parallel", "arbitrary"),
        name="rwkv_recurrence",
    )(r, lw, k, v, a, b)


def _mix_tail_kernel(x_ref, ga_ref, gb_ref, bg_ref, oa_ref, woa_ref,
                     o_ref, r_ref, k_ref, v_ref, g_ref, rk_ref, lw_ref, lb_ref, e_ref, et_ref,
                     wob_ref, wo_ref, gpost_ref, out_ref):
    inv_n = 1.0 / RWKV_HEAD
    o = o_ref[...]
    mean = _head_sum(o, e_ref, et_ref) * inv_n
    cen = o - mean
    var = _head_sum(cen * cen, e_ref, et_ref) * inv_n
    on = cen * lax.rsqrt(var + LNX_EPS) * lw_ref[...] + lb_ref[...]
    v = v_ref[...].astype(F32)
    rk = r_ref[...].astype(F32) * k_ref[...].astype(F32) * rk_ref[...]
    bonus = _head_sum(rk, e_ref, et_ref) * v
    ob = ((on + bonus) * g_ref[...].astype(F32)).astype(BF16)

    y_a = jnp.dot(oa_ref[...], woa_ref[...], preferred_element_type=F32)
    y_b = jnp.dot(ob, wob_ref[...], preferred_element_type=F32)
    gate_a = _sigmoid(ga_ref[...].astype(F32) + bg_ref[0:1, :])
    gate_b = _sigmoid(gb_ref[...].astype(F32) + bg_ref[1:2, :])
    merged = (gate_a * y_a + gate_b * y_b).astype(BF16)
    z = jnp.dot(merged, wo_ref[...], preferred_element_type=F32)
    out_ref[...] = x_ref[...] + _rms(z, gpost_ref[...])


def _mix_tail(x, proj, b_gate, o_a, w_oa, o_raw, r, k, v, g, r_k, lnx_w, lnx_b, e, et, w_ob, w_o, g_post, *, tm=256):
    t, d = x.shape
    c = RWKV_WIDTH
    row = lambda width, col=0: pl.BlockSpec((tm, width), lambda i: (i, col // width))
    return pl.pallas_call(
        _mix_tail_kernel,
        grid=(t // tm,),
        in_specs=[row(d), row(d, COL_GATE_A), row(d, COL_GATE_B), _resident((2, d)),
                  row(c), _resident((c, d)),
                  row(c), row(c), row(c), row(c), row(c),
                  _resident((1, c)), _resident((1, c)), _resident((1, c)),
                  _resident((c, LANES)), _resident((LANES, c)),
                  _resident((c, d)), _resident((d, d)), _resident((1, d))],
        out_specs=row(d),
        out_shape=jax.ShapeDtypeStruct((t, d), F32),
        compiler_params=_params("parallel"),
        name="mix_tail",
    )(x, proj, proj, b_gate, o_a, w_oa, o_raw, r, k, v, g, r_k, lnx_w, lnx_b, e, et, w_ob, w_o, g_post)


def _xattn_kernel(x_ref, gpre_ref, wq_ref, kv_ref, wo_ref, gpost_ref, out_ref):
    x = x_ref[...]
    h = _rms(x, gpre_ref[...]).astype(BF16)
    q = jnp.dot(h, wq_ref[...], preferred_element_type=F32).astype(BF16)
    width = MEM_HEADS * MEM_HEAD
    heads = []
    for hd in range(MEM_HEADS):
        lo = hd * MEM_HEAD
        k = kv_ref[:, lo:lo + MEM_HEAD]
        v = kv_ref[:, width + lo:width + lo + MEM_HEAD]
        s = lax.dot_general(q[:, lo:lo + MEM_HEAD], k, NT, preferred_element_type=F32)
        p = jnp.exp(s - jnp.max(s, axis=-1, keepdims=True))
        p = p / jnp.sum(p, axis=-1, keepdims=True)
        heads.append(jnp.dot(p.astype(BF16), v, preferred_element_type=F32).astype(BF16))
    o = jnp.concatenate(heads, axis=1)
    z = jnp.dot(o, wo_ref[...], preferred_element_type=F32)
    out_ref[...] = x + _rms(z, gpost_ref[...])


def _xattn(x, g_pre, w_cq, kv_mem, w_co, g_post, *, batch, seq, n_mem, tq=512):
    t, d = x.shape
    tq = min(tq, seq)
    nq = seq // tq
    width = MEM_HEADS * MEM_HEAD
    return pl.pallas_call(
        _xattn_kernel,
        grid=(batch, nq),
        in_specs=[pl.BlockSpec((tq, d), lambda b, i: (b * nq + i, 0)),
                  _resident((1, d)), _resident((d, width)),
                  pl.BlockSpec((n_mem, 2 * width), lambda b, i: (b, 0)),
                  _resident((width, d)), _resident((1, d))],
        out_specs=pl.BlockSpec((tq, d), lambda b, i: (b * nq + i, 0)),
        out_shape=jax.ShapeDtypeStruct((t, d), F32),
        compiler_params=_params("parallel", "arbitrary"),
        name="xattn",
    )(x, g_pre, w_cq, kv_mem, w_co, g_post)


def _rope_tables(positions):
    half = QK_ROPE // 2
    inv = ROPE_THETA ** (-jnp.arange(half, dtype=F32) / half)
    ang = positions.astype(F32)[..., None] * inv
    return jnp.cos(ang), jnp.sin(ang)


def _layer(x, mem2d, positions, p, *, batch, seq, n_mem):
    row = lambda a: a.reshape(1, -1).astype(F32)
    c = RWKV_WIDTH
    d = D_MODEL

    def ffn(x, pre, gate, up, down, post):
        return _ffn(x, row(pre), gate.astype(BF16), up.astype(BF16), down.astype(BF16), row(post))

    x = ffn(x, p['n_ffn1_pre'], p['w_ffn1_gate'], p['w_ffn1_up'], p['w_ffn1_down'], p['n_ffn1_post'])

    w_in = p['w_in']
    o_rwkv = MLA_COLS
    o_gate = MLA_COLS + RWKV_COLS
    half = QK_ROPE // 2
    w_kr = w_in[:, Q_LORA + KV_LORA:MLA_COLS]
    w_kr_swap = jnp.concatenate([w_kr[:, half:], w_kr[:, :half]], axis=1)
    w_in_r = jnp.concatenate([
        w_in[:, o_gate:],
        w_in[:, o_rwkv:o_rwkv + 3 * c],
        w_in[:, :Q_LORA + KV_LORA],
        w_in[:, o_rwkv + 3 * c:o_gate],
        w_kr, w_kr, w_kr_swap, w_kr_swap,
    ], axis=1).astype(BF16)
    scale = (QK_NOPE + QK_ROPE) ** -0.5
    w_uq = p['w_uq'].reshape(Q_LORA, MLA_HEADS, QK_NOPE + QK_ROPE) * scale
    w_qr = w_uq[:, :, QK_NOPE:]
    w_qr_swap = jnp.concatenate([w_qr[:, :, half:], w_qr[:, :, :half]], axis=2)
    w_uq_r = jnp.concatenate([w_uq[:, :, :QK_NOPE].reshape(Q_LORA, -1), w_qr.reshape(Q_LORA, -1),
                              w_qr_swap.reshape(Q_LORA, -1)], axis=1).astype(BF16)
    proj, q_up, kv_up = _in_proj(x, row(p['n_mix_pre']), w_in_r, row(p['n_q_lat']), w_uq_r,
                                 row(p['n_kv_lat']), p['w_ukv'].astype(BF16))

    cos, sin = _rope_tables(positions)
    cos_tab = jnp.concatenate([cos] * 4, axis=-1).reshape(batch * seq, LANES)
    sin_tab = jnp.concatenate([-sin, sin, -sin, sin], axis=-1).reshape(batch * seq, LANES)
    o_a = _mla_attention(q_up, kv_up, proj, cos_tab, sin_tab, batch=batch, seq=seq)

    head_of = jnp.arange(c) // RWKV_HEAD
    e = (head_of[:, None] == jnp.arange(LANES)[None, :]).astype(BF16)
    mu = p['mu_shift']
    r, lw, k, v, a_in, b_in, g = _rwkv_prep(
        proj, mu[:3 * c].reshape(3, c), row(mu[3 * c:]), row(p['w0']), p['w_w2'].astype(BF16),
        row(p['a0']), p['w_a2'].astype(BF16), p['w_g2'].astype(BF16), row(p['k_k']), row(p['k_a']),
        e, e.T, seq=seq)
    o_raw = _rwkv_recurrence(r, lw, k, v, a_in, b_in, batch=batch, seq=seq)

    x = _mix_tail(x, proj, p['b_gate'].reshape(2, d), o_a, p['w_oa'].astype(BF16), o_raw, r, k, v, g,
                  row(p['r_k']), row(p['lnx_w']), row(p['lnx_b']), e, e.T,
                  p['w_ob'].astype(BF16), p['w_o'].astype(BF16), row(p['n_mix_post']))

    width = MEM_HEADS * MEM_HEAD
    w_ckv = p['w_ckv'].reshape(d, MEM_HEADS, 2, MEM_HEAD).transpose(0, 2, 1, 3).reshape(d, 2 * width)
    kv_mem = _norm_matmul(mem2d, 0, d, row(p['n_mem']), w_ckv.astype(BF16), tm=512, tn=1024)
    w_cq = (p['w_cq'] * MEM_HEAD ** -0.5).astype(BF16)
    x = _xattn(x, row(p['n_x_pre']), w_cq, kv_mem, p['w_co'].astype(BF16), row(p['n_x_post']),
               batch=batch, seq=seq, n_mem=n_mem)

    return ffn(x, p['n_ffn2_pre'], p['w_ffn2_gate'], p['w_ffn2_up'], p['w_ffn2_down'], p['n_ffn2_post'])


_PARAM_NAMES = (
    'n_ffn1_pre', 'n_ffn1_post', 'w_ffn1_gate', 'w_ffn1_up', 'w_ffn1_down',
    'n_mix_pre', 'n_mix_post', 'w_in', 'b_gate',
    'n_q_lat', 'w_uq', 'n_kv_lat', 'w_ukv', 'w_oa',
    'mu_shift', 'w0', 'w_w2', 'a0', 'w_a2', 'w_g2', 'k_k', 'k_a', 'r_k', 'lnx_w', 'lnx_b', 'w_ob',
    'w_o',
    'n_x_pre', 'n_x_post', 'n_mem', 'w_cq', 'w_ckv', 'w_co',
    'n_ffn2_pre', 'n_ffn2_post', 'w_ffn2_gate', 'w_ffn2_up', 'w_ffn2_down')


def kernel(x, mem, positions, n_ffn1_pre, n_ffn1_post, w_ffn1_gate, w_ffn1_up, w_ffn1_down, n_mix_pre, n_mix_post, w_in, b_gate, n_q_lat, w_uq, n_kv_lat, w_ukv, w_oa, mu_shift, w0, w_w2, a0, w_a2, w_g2, k_k, k_a, r_k, lnx_w, lnx_b, w_ob, w_o, n_x_pre, n_x_post, n_mem, w_cq, w_ckv, w_co, n_ffn2_pre, n_ffn2_post, w_ffn2_gate, w_ffn2_up, w_ffn2_down):
    stacked = (n_ffn1_pre, n_ffn1_post, w_ffn1_gate, w_ffn1_up, w_ffn1_down, n_mix_pre, n_mix_post, w_in, b_gate,
               n_q_lat, w_uq, n_kv_lat, w_ukv, w_oa, mu_shift, w0, w_w2, a0, w_a2, w_g2, k_k, k_a, r_k, lnx_w,
               lnx_b, w_ob, w_o, n_x_pre, n_x_post, n_mem, w_cq, w_ckv, w_co, n_ffn2_pre, n_ffn2_post,
               w_ffn2_gate, w_ffn2_up, w_ffn2_down)
    batch, seq, d = x.shape
    n_mem_tokens = mem.shape[1]
    x2d = x.reshape(batch * seq, d)
    mem2d = mem.reshape(batch * n_mem_tokens, d)
    for layer in range(n_ffn1_pre.shape[0]):
        p = {name: arr[layer] for name, arr in zip(_PARAM_NAMES, stacked)}
        x2d = _layer(x2d, mem2d, positions, p, batch=batch, seq=seq, n_mem=n_mem_tokens)
    return x2d.reshape(batch, seq, d)
```

```python
import functools

import jax
import jax.numpy as jnp
from jax import lax
from jax.experimental import pallas as pl
from jax.experimental.pallas import tpu as pltpu

F32 = jnp.float32
BF16 = jnp.bfloat16

D_MODEL = 2048
D_FF = 5504
CHUNK = 64
EPS = 1e-6

MLA_HEADS = 8
QK_NOPE = 128
QK_ROPE = 64
V_HEAD = 128
Q_LORA = 512
KV_LORA = 256
ROPE_THETA = 10000.0

RWKV_HEADS = 16
RWKV_HEAD = 64
RWKV_WIDTH = RWKV_HEADS * RWKV_HEAD
DECAY_LORA = 64
A_LORA = 64
GATE_LORA = 128
LNX_EPS = 64e-5
DECAY_SCALE = 0.6065306597126334

MEM_HEADS = 4
MEM_HEAD = 256

MLA_COLS = Q_LORA + KV_LORA + QK_ROPE
RWKV_COLS = 3 * RWKV_WIDTH + DECAY_LORA + A_LORA + GATE_LORA

LANES = 128
VMEM_LIMIT_BYTES = 56 * 1024 * 1024

COL_GATE_A = 0
COL_GATE_B = 2048
COL_R = 4096
COL_K = 5120
COL_V = 6144
COL_CQ = 7168
COL_CKV = 7680
COL_SMALL = 7936
COL_KROPE = 8192
COL_KROPE_SWAP = 8320
IN_COLS_PADDED = 8448

NN = (((1,), (0,)), ((), ()))
NT = (((1,), (1,)), ((), ()))
TN = (((0,), (0,)), ((), ()))


def _dot(a, b, dims=NN):
    return lax.dot_general(a.astype(BF16), b.astype(BF16), dims, preferred_element_type=F32)


def _each(f, *cols):
    return [f(*xs) for xs in zip(*cols)]


def _split2(x):
    hi = x.astype(BF16)
    lo = (x - hi.astype(F32)).astype(BF16)
    return hi, lo


def _rms(xf, g, eps=EPS):
    return xf * lax.rsqrt(jnp.mean(xf * xf, axis=-1, keepdims=True) + eps) * g


def _sigmoid(x):
    return 1.0 / (1.0 + jnp.exp(-x))


def _params(*sem):
    return pltpu.CompilerParams(dimension_semantics=sem, vmem_limit_bytes=VMEM_LIMIT_BYTES)


def _resident(shape):
    return pl.BlockSpec(shape, lambda *_: (0,) * len(shape), pipeline_mode=pl.Buffered(1))


def _norm_matmul_kernel(x_ref, g_ref, w_ref, o_ref, h_ref):
    @pl.when(pl.program_id(1) == 0)
    def _():
        h_ref[...] = _rms(x_ref[...].astype(F32), g_ref[...]).astype(BF16)

    o_ref[...] = jnp.dot(h_ref[...], w_ref[...], preferred_element_type=F32).astype(o_ref.dtype)


def _norm_matmul(x, col_block, k_dim, g, w, *, tm, tn, out_dtype=BF16):
    m = x.shape[0]
    n = w.shape[1]
    tm = min(tm, m)
    return pl.pallas_call(
        _norm_matmul_kernel,
        grid=(m // tm, n // tn),
        in_specs=[
            pl.BlockSpec((tm, k_dim), lambda i, j: (i, col_block)),
            pl.BlockSpec((1, k_dim), lambda i, j: (0, 0)),
            pl.BlockSpec((k_dim, tn), lambda i, j: (0, j)),
        ],
        out_specs=pl.BlockSpec((tm, tn), lambda i, j: (i, j)),
        out_shape=jax.ShapeDtypeStruct((m, n), out_dtype),
        scratch_shapes=[pltpu.VMEM((tm, k_dim), BF16)],
        compiler_params=_params("parallel", "arbitrary"),
        name="norm_matmul",
    )(x, g, w)


IN_TN = 768
CQ_TILE, CQ_OFF = divmod(COL_CQ, IN_TN)
CKV_TILE, CKV_OFF = divmod(COL_CKV, IN_TN)
assert CQ_OFF + Q_LORA <= IN_TN and CKV_OFF + KV_LORA <= IN_TN and IN_COLS_PADDED % IN_TN == 0


def _in_proj_kernel(x_ref, g_ref, w_ref, gq_ref, wq_ref, gkv_ref, wkv_ref, o_ref, q_ref, kv_ref, h_ref):
    j = pl.program_id(1)

    @pl.when(j == 0)
    def _():
        h_ref[...] = _rms(x_ref[...], g_ref[...]).astype(BF16)

    y = jnp.dot(h_ref[...], w_ref[...], preferred_element_type=F32)
    o_ref[...] = y.astype(o_ref.dtype)

    def up_project(latent, gain_ref, weight_ref, out_ref):
        out_ref[...] = jnp.dot(_rms(latent, gain_ref[...]).astype(BF16), weight_ref[...],
                               preferred_element_type=F32).astype(out_ref.dtype)

    @pl.when(j == CQ_TILE)
    def _():
        up_project(y[:, CQ_OFF:CQ_OFF + Q_LORA], gq_ref, wq_ref, q_ref)

    @pl.when(j == CKV_TILE)
    def _():
        up_project(y[:, CKV_OFF:CKV_OFF + KV_LORA], gkv_ref, wkv_ref, kv_ref)


def _in_proj(x, g, w, g_q, w_uq, g_kv, w_ukv, *, tm=1024):
    m, d = x.shape
    tm = min(tm, m)
    nq, nkv = w_uq.shape[1], w_ukv.shape[1]
    return pl.pallas_call(
        _in_proj_kernel,
        grid=(m // tm, IN_COLS_PADDED // IN_TN),
        in_specs=[
            pl.BlockSpec((tm, d), lambda i, j: (i, 0)),
            _resident((1, d)),
            pl.BlockSpec((d, IN_TN), lambda i, j: (0, j)),
            _resident((1, Q_LORA)), _resident((Q_LORA, nq)),
            _resident((1, KV_LORA)), _resident((KV_LORA, nkv)),
        ],
        out_specs=[pl.BlockSpec((tm, IN_TN), lambda i, j: (i, j)),
                   pl.BlockSpec((tm, nq), lambda i, j: (i, 0)),
                   pl.BlockSpec((tm, nkv), lambda i, j: (i, 0))],
        out_shape=[jax.ShapeDtypeStruct((m, IN_COLS_PADDED), BF16),
                   jax.ShapeDtypeStruct((m, nq), BF16),
                   jax.ShapeDtypeStruct((m, nkv), BF16)],
        scratch_shapes=[pltpu.VMEM((tm, d), BF16)],
        compiler_params=_params("parallel", "arbitrary"),
        name="in_proj",
    )(x, g, w, g_q, w_uq, g_kv, w_ukv)


def _ffn_kernel(x_ref, gpre_ref, wg_ref, wu_ref, wd_ref, gpost_ref, o_ref, h_ref, acc_ref, *, tail):
    j = pl.program_id(1)
    last = pl.num_programs(1) - 1
    tf = wg_ref.shape[1]

    @pl.when(j == 0)
    def _():
        h_ref[...] = _rms(x_ref[...], gpre_ref[...]).astype(BF16)
        acc_ref[...] = jnp.zeros_like(acc_ref)

    def accumulate(valid):
        h = h_ref[...]
        gate = jnp.dot(h, wg_ref[:, :valid], preferred_element_type=F32)
        up = jnp.dot(h, wu_ref[:, :valid], preferred_element_type=F32)
        act = (gate * _sigmoid(gate) * up).astype(BF16)
        acc_ref[...] += jnp.dot(act, wd_ref[:valid, :], preferred_element_type=F32)

    if tail == tf:
        accumulate(tf)
    else:
        pl.when(j < last)(lambda: accumulate(tf))
        pl.when(j == last)(lambda: accumulate(tail))

    @pl.when(j == last)
    def _():
        o_ref[...] = x_ref[...] + 0.5 * _rms(acc_ref[...], gpost_ref[...])


def _ffn(x, g_pre, wg, wu, wd, g_post, *, tm=512, tf=1024):
    m, d = x.shape
    f = wg.shape[1]
    n_tiles = pl.cdiv(f, tf)
    tm = min(tm, m)
    tail = f - (n_tiles - 1) * tf
    assert tail % LANES == 0
    return pl.pallas_call(
        functools.partial(_ffn_kernel, tail=tail),
        grid=(m // tm, n_tiles),
        in_specs=[
            pl.BlockSpec((tm, d), lambda i, j: (i, 0)),
            pl.BlockSpec((1, d), lambda i, j: (0, 0)),
            pl.BlockSpec((d, tf), lambda i, j: (0, j)),
            pl.BlockSpec((d, tf), lambda i, j: (0, j)),
            pl.BlockSpec((tf, d), lambda i, j: (j, 0)),
            pl.BlockSpec((1, d), lambda i, j: (0, 0)),
        ],
        out_specs=pl.BlockSpec((tm, d), lambda i, j: (i, 0)),
        out_shape=jax.ShapeDtypeStruct((m, d), F32),
        scratch_shapes=[pltpu.VMEM((tm, d), BF16), pltpu.VMEM((tm, d), F32)],
        compiler_params=_params("parallel", "arbitrary"),
        name="ffn",
    )(x, g_pre, wg, wu, wd, g_post)


def _mla_attn_kernel(qn_ref, qr_ref, qs_ref, cq_ref, sq_ref, kv_ref, kr_ref, ks_ref, ck_ref, sk_ref,
                     o_ref, krot_ref, *, tq, tk, n_heads):
    qi = pl.program_id(2)
    heads = list(range(n_heads))
    kv_width = QK_NOPE + V_HEAD

    @pl.when(qi == 0)
    def _():
        krot_ref[...] = (kr_ref[...].astype(F32) * ck_ref[...]
                         + ks_ref[...].astype(F32) * sk_ref[...]).astype(BF16)

    cq, sq = cq_ref[...], sq_ref[...]
    q_rot = [qr_ref[:, p * LANES:(p + 1) * LANES].astype(F32) * cq
             + qs_ref[:, p * LANES:(p + 1) * LANES].astype(F32) * sq for p in range(n_heads // 2)]
    lane = lax.broadcasted_iota(jnp.int32, (tq, LANES), 1)
    own_half = [lane < QK_ROPE, lane >= QK_ROPE]
    q_cat = [jnp.concatenate([qn_ref[:, h * QK_NOPE:(h + 1) * QK_NOPE],
                              jnp.where(own_half[h % 2], q_rot[h // 2], 0.0).astype(BF16)], axis=1)
             for h in heads]

    def scores(start):
        k_rot = krot_ref[pl.ds(start, tk), :]
        return _each(lambda h, q_: lax.dot_general(
            jnp.concatenate([kv_ref[pl.ds(start, tk), h * kv_width:h * kv_width + QK_NOPE], k_rot], axis=1), q_,
            NT, preferred_element_type=F32), heads, q_cat)

    def update(carry, s, start):
        m, l, acc = carry
        m_new = _each(lambda m_, s_: jnp.maximum(m_, jnp.max(s_, axis=0, keepdims=True)), m, s)
        alpha = _each(lambda m_, n_: jnp.exp(m_ - n_), m, m_new)
        p = _each(lambda s_, n_: jnp.exp(s_ - n_), s, m_new)
        l = _each(lambda a_, l_, p_: a_ * l_ + jnp.sum(p_, axis=0, keepdims=True), alpha, l, p)
        pv = _each(lambda h, p_: lax.dot_general(
            kv_ref[pl.ds(start, tk), h * kv_width + QK_NOPE:(h + 1) * kv_width], p_.astype(BF16), TN,
            preferred_element_type=F32), heads, p)
        acc = _each(lambda a_, acc_, pv_: a_ * acc_ + pv_, alpha, acc, pv)
        return m_new, l, acc

    def body(kj, carry):
        start = pl.multiple_of(kj * tk, tk)
        return update(carry, scores(start), start)

    init = ([jnp.full((1, tq), -1e30, F32) for _ in heads], [jnp.zeros((1, tq), F32) for _ in heads],
            [jnp.zeros((V_HEAD, tq), F32) for _ in heads])
    carry = lax.fori_loop(0, qi * (tq // tk), body, init)

    q_chunk = lax.broadcasted_iota(jnp.int32, (tk, tq), 1) // CHUNK
    for sub in range(tq // tk):
        start = pl.multiple_of(qi * tq + sub * tk, tk)
        k_chunk = (lax.broadcasted_iota(jnp.int32, (tk, tq), 0) + sub * tk) // CHUNK
        visible = k_chunk <= q_chunk
        s = _each(lambda s_: jnp.where(visible, s_, -1e30), scores(start))
        carry = update(carry, s, start)
    m, l, acc = carry
    for h in heads:
        o_ref[:, h * V_HEAD:(h + 1) * V_HEAD] = (acc[h] / l[h]).T.astype(o_ref.dtype)


def _mla_attention(q_up, kv_up, proj, cos_tab, sin_tab, *, batch, seq, tq=256, tk=256, n_heads=MLA_HEADS):
    nq = seq // tq
    rope_w = n_heads * QK_ROPE
    rope0 = MLA_HEADS * QK_NOPE // rope_w
    swap0 = (MLA_HEADS * QK_NOPE + MLA_HEADS * QK_ROPE) // rope_w
    q_rows = lambda width, col0: pl.BlockSpec((tq, width), lambda b, g, i: (b * nq + i, col0 + g))
    q_tab = pl.BlockSpec((tq, LANES), lambda b, g, i: (b * nq + i, 0))
    k_tab = pl.BlockSpec((seq, LANES), lambda b, g, i: (b, 0))
    k_cols = lambda col: pl.BlockSpec((seq, LANES), lambda b, g, i: (b, col // LANES))
    return pl.pallas_call(
        functools.partial(_mla_attn_kernel, tq=tq, tk=tk, n_heads=n_heads),
        grid=(batch, MLA_HEADS // n_heads, nq),
        in_specs=[
            q_rows(n_heads * QK_NOPE, 0), q_rows(rope_w, rope0), q_rows(rope_w, swap0), q_tab, q_tab,
            pl.BlockSpec((seq, n_heads * (QK_NOPE + V_HEAD)), lambda b, g, i: (b, g)),
            k_cols(COL_KROPE), k_cols(COL_KROPE_SWAP), k_tab, k_tab,
        ],
        out_specs=pl.BlockSpec((tq, n_heads * V_HEAD), lambda b, g, i: (b * nq + i, g)),
        out_shape=jax.ShapeDtypeStruct((batch * seq, MLA_HEADS * V_HEAD), BF16),
        scratch_shapes=[pltpu.VMEM((seq, LANES), BF16)],
        compiler_params=_params("parallel", "parallel", "arbitrary"),
        name="mla_attention",
    )(q_up, q_up, q_up, cos_tab, sin_tab, kv_up, proj, proj, cos_tab, sin_tab)


def _head_sum(x, e_ref, et_ref):
    s = _dot(x, e_ref[...])
    return _dot(s, et_ref[...])


def _shift(y, prev_row, mu, first):
    rows = lax.broadcasted_iota(jnp.int32, y.shape, 0)
    prev0 = jnp.where(first, 0.0, prev_row)
    y_prev = jnp.where(rows == 0, prev0, pltpu.roll(y, 1, 0))
    return y + (y_prev - y) * mu


def _rwkv_prep_kernel(r_ref, k_ref, v_ref, s_ref, rp_ref, kp_ref, vp_ref, sp_ref,
                      mu_rkv_ref, mu_s_ref, w0_ref, ww2_ref, a0_ref, wa2_ref, wg2_ref,
                      kk_ref, ka_ref, e_ref, et_ref,
                      ro_ref, lw_ref, ko_ref, vo_ref, ao_ref, bo_ref, go_ref, *, tiles_per_seq):
    first = (pl.program_id(0) % tiles_per_seq) == 0
    last8 = slice(7, 8)

    def shifted(ref, pref, mu):
        return _shift(ref[...].astype(F32), pref[last8, :].astype(F32), mu, first)

    r = shifted(r_ref, rp_ref, mu_rkv_ref[0:1, :])
    k = shifted(k_ref, kp_ref, mu_rkv_ref[1:2, :])
    v = shifted(v_ref, vp_ref, mu_rkv_ref[2:3, :])
    small = shifted(s_ref, sp_ref, mu_s_ref[...])
    dw = small[:, 0:DECAY_LORA]
    da = small[:, DECAY_LORA:DECAY_LORA + A_LORA]
    dg = small[:, DECAY_LORA + A_LORA:]

    u = w0_ref[...] + _dot(jnp.tanh(dw), ww2_ref[...])
    lw_ref[...] = -DECAY_SCALE * _sigmoid(u)
    a = _sigmoid(a0_ref[...] + _dot(da, wa2_ref[...]))
    go_ref[...] = _dot(_sigmoid(dg), wg2_ref[...]).astype(go_ref.dtype)

    kk = k * kk_ref[...]
    kk = kk * lax.rsqrt(jnp.maximum(_head_sum(kk * kk, e_ref, et_ref), 1e-24))
    ro_ref[...] = r.astype(ro_ref.dtype)
    ko_ref[...] = (k * (1.0 + (a - 1.0) * ka_ref[...])).astype(ko_ref.dtype)
    vo_ref[...] = v.astype(vo_ref.dtype)
    ao_ref[...] = (-kk).astype(ao_ref.dtype)
    bo_ref[...] = (kk * a).astype(bo_ref.dtype)


def _rwkv_prep(proj, mu_rkv, mu_s, w0, ww2, a0, wa2, wg2, k_k, k_a, e, et, *, seq, tm=256):
    t = proj.shape[0]
    c = RWKV_WIDTH
    small_w = DECAY_LORA + A_LORA + GATE_LORA

    def cur(width, col):
        return pl.BlockSpec((tm, width), lambda i: (i, col // width))

    def prev(width, col):
        return pl.BlockSpec((8, width), lambda i: (jnp.maximum(i * (tm // 8) - 1, 0), col // width))

    def const(shape):
        return pl.BlockSpec(shape, lambda i: (0, 0))

    out = jax.ShapeDtypeStruct((t, c), BF16)
    return pl.pallas_call(
        functools.partial(_rwkv_prep_kernel, tiles_per_seq=seq // tm),
        grid=(t // tm,),
        in_specs=[cur(c, COL_R), cur(c, COL_K), cur(c, COL_V), cur(small_w, COL_SMALL),
                  prev(c, COL_R), prev(c, COL_K), prev(c, COL_V), prev(small_w, COL_SMALL),
                  const((3, c)), const((1, small_w)), const((1, c)), const((DECAY_LORA, c)),
                  const((1, c)), const((A_LORA, c)), const((GATE_LORA, c)),
                  const((1, c)), const((1, c)), const((c, LANES)), const((LANES, c))],
        out_specs=[pl.BlockSpec((tm, c), lambda i: (i, 0))] * 7,
        out_shape=[out, jax.ShapeDtypeStruct((t, c), F32), out, out, out, out, out],
        compiler_params=_params("parallel"),
        name="rwkv_prep",
    )(proj, proj, proj, proj, proj, proj, proj, proj,
      mu_rkv, mu_s, w0, ww2, a0, wa2, wg2, k_k, k_a, e, et)


def _block_diag(x):
    lane = lax.broadcasted_iota(jnp.int32, x.shape, 1)
    zero = jnp.zeros_like(x)
    return jnp.concatenate([jnp.where(lane < RWKV_HEAD, x, zero),
                            jnp.where(lane >= RWKV_HEAD, x, zero)], axis=0)


def _unit_lower_inverse(a, eye, level_masks):
    square = lambda x: _dot(x, _block_diag(x))
    times = lambda t, x: t + _dot(t, _block_diag(x))
    base = _each(lambda x: x * level_masks[0], a)
    x2 = _each(square, base)
    x4 = _each(square, x2)
    t = _each(lambda x: eye + x, base)
    t = _each(times, t, x2)
    t = _each(times, t, x4)
    for mask in level_masks[1:]:
        ta = _each(lambda t_, a_: _dot(t_, _block_diag(a_ * mask)), t, a)
        t = _each(lambda t_, ta_: t_ + _dot(ta_, _block_diag(t_)), t, ta)
    return t


def _chunk_prepare(r, lw, k, v, a, b, consts):
    ltri, strict, incl, eye, _, level_masks = consts
    c = CHUNK
    rows = lambda x, y: jnp.concatenate([x, y], axis=0)

    def cumulative(x):
        hi, lo = _split2(x)
        return jnp.dot(ltri, hi, preferred_element_type=F32) + jnp.dot(ltri, lo, preferred_element_type=F32)

    cl = _each(cumulative, lw)
    cl_end = _each(lambda x: x[c - 1:c, :], cl)
    p_inv = _each(lambda x: jnp.exp(-x), cl)
    a_t = _each(lambda a_, cl_, lw_: a_ * jnp.exp(cl_ - lw_), a, cl, lw)
    r_t = _each(lambda r_, cl_: r_ * jnp.exp(cl_), r, cl)
    b_t = _each(jnp.multiply, b, p_inv)
    k_t = _each(jnp.multiply, k, p_inv)

    ar = _each(rows, a_t, r_t)
    g_b = _each(lambda x, y: _dot(x, _block_diag(y), NT), ar, b_t)
    g_k = _each(lambda x, y: _dot(x, _block_diag(y), NT), ar, k_t)
    a_ab = _each(lambda g: jnp.where(strict, g[:c], 0.0), g_b)
    a_rb = _each(lambda g: jnp.where(incl, g[c:], 0.0), g_b)
    a_k = _each(lambda g: rows(jnp.where(strict, g[:c], 0.0), jnp.where(incl, g[c:], 0.0)), g_k)

    t_inv = _unit_lower_inverse(a_ab, eye, level_masks)

    av = _each(lambda x, v_: _dot(x, _block_diag(v_)), a_k, v)
    tw = _each(lambda t_, a_, av_: _dot(t_, jnp.concatenate([_block_diag(a_), _block_diag(av_[:c])], axis=1)),
               t_inv, a_t, av)
    lhs = _each(lambda tw_, r_: rows(tw_[:, :LANES], r_).astype(BF16), tw, r_t)
    w_eff = _each(lambda tw_: tw_[:, LANES:], tw)
    o_local = _each(lambda av_: av_[c:], av)
    p_tail = _each(lambda e_, cl_: jnp.exp(e_ - cl_), cl_end, cl)
    bk_tail = _each(lambda b_, k_, p_: rows(b_ * p_, k_ * p_).astype(BF16), b, k, p_tail)
    p_end = _each(jnp.exp, cl_end)
    return lhs, w_eff, a_rb, o_local, v, bk_tail, p_end


def _chunk_apply(prepared, state, diag_blocks):
    c = CHUNK
    lhs, w_eff, a_rb, o_local, v, bk_tail, p_end = prepared
    uo = _each(lambda lhs_, s_: _dot(lhs_, s_, NT), lhs, state)
    u = _each(lambda uo_, w_: uo_[:c] + w_, uo, w_eff)
    o = _each(lambda uo_, arb_, u_, ol_: uo_[c:] + _dot(arb_, _block_diag(u_)) + ol_, uo, a_rb, u, o_local)
    upd = _each(lambda u_, v_, bk_: _dot(jnp.concatenate([u_, v_], axis=0), bk_, TN), u, v, bk_tail)
    new_state = _each(lambda s_, p_, upd_: s_ * p_ + jnp.where(diag_blocks, upd_, 0.0), state, p_end, upd)
    return o, new_state


def _rwkv_kernel(r_ref, lw_ref, k_ref, v_ref, a_ref, b_ref, o_ref, s_ref, *, n_chunks, n_pairs, unroll):
    @pl.when(pl.program_id(2) == 0)
    def _():
        s_ref[...] = jnp.zeros_like(s_ref)

    c = CHUNK
    row = lax.broadcasted_iota(jnp.int32, (c, LANES), 0)
    col = lax.broadcasted_iota(jnp.int32, (c, LANES), 1) % RWKV_HEAD
    ltri = jnp.where(lax.broadcasted_iota(jnp.int32, (c, c), 1) <= lax.broadcasted_iota(jnp.int32, (c, c), 0),
                     1.0, 0.0).astype(BF16)
    rr = lax.broadcasted_iota(jnp.int32, (LANES, LANES), 0) // RWKV_HEAD
    cc = lax.broadcasted_iota(jnp.int32, (LANES, LANES), 1) // RWKV_HEAD
    def same(block):
        return (row // block) == (col // block)
    level_masks = [jnp.where(same(8), 1.0, 0.0)]
    for block in (8, 16, 32):
        level_masks.append(jnp.where(same(2 * block), 1.0, 0.0) - jnp.where(same(block), 1.0, 0.0))
    eye = jnp.where(col == row, 1.0, 0.0)
    consts = (ltri, col < row, col <= row, eye, rr == cc, level_masks)

    def chunk_group(gi, carry):
        r0 = pl.multiple_of(gi * (unroll * c), unroll * c)
        tiles = [(pl.ds(r0 + u * c, c), slice(p * LANES, (p + 1) * LANES))
                 for u in range(unroll) for p in range(n_pairs)]
        load = lambda ref: [ref[sl].astype(F32) for sl in tiles]
        prepared = _chunk_prepare(load(r_ref), load(lw_ref), load(k_ref), load(v_ref), load(a_ref), load(b_ref),
                                  consts)
        state = [s_ref[p] for p in range(n_pairs)]
        for u in range(unroll):
            part = slice(u * n_pairs, (u + 1) * n_pairs)
            o, state = _chunk_apply([x[part] for x in prepared], state, consts[4])
            for p in range(n_pairs):
                o_ref[tiles[u * n_pairs + p]] = o[p]
        for p in range(n_pairs):
            s_ref[p] = state[p]
        return carry

    lax.fori_loop(0, n_chunks // unroll, chunk_group, 0)


def _rwkv_recurrence(r, lw, k, v, a, b, *, batch, seq, tm=512, n_pairs=8, unroll=2):
    t, c = r.shape
    width = n_pairs * LANES
    tm = min(tm, seq)
    nt = seq // tm
    assert (tm // CHUNK) % unroll == 0
    spec = pl.BlockSpec((tm, width), lambda bi, g, i: (bi * nt + i, g))
    return pl.pallas_call(
        functools.partial(_rwkv_kernel, n_chunks=tm // CHUNK, n_pairs=n_pairs, unroll=unroll),
        grid=(batch, c // width, nt),
        in_specs=[spec] * 6,
        out_specs=spec,
        out_shape=jax.ShapeDtypeStruct((t, c), F32),
        scratch_shapes=[pltpu.VMEM((n_pairs, LANES, LANES), F32)],
        compiler_params=_params("parallel", "parallel", "arbitrary"),
        name="rwkv_recurrence",
    )(r, lw, k, v, a, b)


def _mix_tail_kernel(x_ref, ga_ref, gb_ref, bg_ref, oa_ref, woa_ref,
                     o_ref, r_ref, k_ref, v_ref, g_ref, rk_ref, lw_ref, lb_ref, e_ref, et_ref,
                     wob_ref, wo_ref, gpost_ref, out_ref):
    inv_n = 1.0 / RWKV_HEAD
    o = o_ref[...]
    mean = _head_sum(o, e_ref, et_ref) * inv_n
    cen = o - mean
    var = _head_sum(cen * cen, e_ref, et_ref) * inv_n
    on = cen * lax.rsqrt(var + LNX_EPS) * lw_ref[...] + lb_ref[...]
    v = v_ref[...].astype(F32)
    rk = r_ref[...].astype(F32) * k_ref[...].astype(F32) * rk_ref[...]
    bonus = _head_sum(rk, e_ref, et_ref) * v
    ob = ((on + bonus) * g_ref[...].astype(F32)).astype(BF16)

    y_a = jnp.dot(oa_ref[...], woa_ref[...], preferred_element_type=F32)
    y_b = jnp.dot(ob, wob_ref[...], preferred_element_type=F32)
    gate_a = _sigmoid(ga_ref[...].astype(F32) + bg_ref[0:1, :])
    gate_b = _sigmoid(gb_ref[...].astype(F32) + bg_ref[1:2, :])
    merged = (gate_a * y_a + gate_b * y_b).astype(BF16)
    z = jnp.dot(merged, wo_ref[...], preferred_element_type=F32)
    out_ref[...] = x_ref[...] + _rms(z, gpost_ref[...])


def _mix_tail(x, proj, b_gate, o_a, w_oa, o_raw, r, k, v, g, r_k, lnx_w, lnx_b, e, et, w_ob, w_o, g_post, *, tm=256):
    t, d = x.shape
    c = RWKV_WIDTH
    row = lambda width, col=0: pl.BlockSpec((tm, width), lambda i: (i, col // width))
    return pl.pallas_call(
        _mix_tail_kernel,
        grid=(t // tm,),
        in_specs=[row(d), row(d, COL_GATE_A), row(d, COL_GATE_B), _resident((2, d)),
                  row(c), _resident((c, d)),
                  row(c), row(c), row(c), row(c), row(c),
                  _resident((1, c)), _resident((1, c)), _resident((1, c)),
                  _resident((c, LANES)), _resident((LANES, c)),
                  _resident((c, d)), _resident((d, d)), _resident((1, d))],
        out_specs=row(d),
        out_shape=jax.ShapeDtypeStruct((t, d), F32),
        compiler_params=_params("parallel"),
        name="mix_tail",
    )(x, proj, proj, b_gate, o_a, w_oa, o_raw, r, k, v, g, r_k, lnx_w, lnx_b, e, et, w_ob, w_o, g_post)


def _xattn_kernel(x_ref, gpre_ref, wq_ref, kv_ref, wo_ref, gpost_ref, out_ref):
    x = x_ref[...]
    h = _rms(x, gpre_ref[...]).astype(BF16)
    q = jnp.dot(h, wq_ref[...], preferred_element_type=F32).astype(BF16)
    width = MEM_HEADS * MEM_HEAD
    heads = []
    for hd in range(MEM_HEADS):
        lo = hd * MEM_HEAD
        k = kv_ref[:, lo:lo + MEM_HEAD]
        v = kv_ref[:, width + lo:width + lo + MEM_HEAD]
        s = lax.dot_general(q[:, lo:lo + MEM_HEAD], k, NT, preferred_element_type=F32)
        p = jnp.exp(s - jnp.max(s, axis=-1, keepdims=True))
        p = p / jnp.sum(p, axis=-1, keepdims=True)
        heads.append(jnp.dot(p.astype(BF16), v, preferred_element_type=F32).astype(BF16))
    o = jnp.concatenate(heads, axis=1)
    z = jnp.dot(o, wo_ref[...], preferred_element_type=F32)
    out_ref[...] = x + _rms(z, gpost_ref[...])


def _xattn(x, g_pre, w_cq, kv_mem, w_co, g_post, *, batch, seq, n_mem, tq=512):
    t, d = x.shape
    tq = min(tq, seq)
    nq = seq // tq
    width = MEM_HEADS * MEM_HEAD
    return pl.pallas_call(
        _xattn_kernel,
        grid=(batch, nq),
        in_specs=[pl.BlockSpec((tq, d), lambda b, i: (b * nq + i, 0)),
                  _resident((1, d)), _resident((d, width)),
                  pl.BlockSpec((n_mem, 2 * width), lambda b, i: (b, 0)),
                  _resident((width, d)), _resident((1, d))],
        out_specs=pl.BlockSpec((tq, d), lambda b, i: (b * nq + i, 0)),
        out_shape=jax.ShapeDtypeStruct((t, d), F32),
        compiler_params=_params("parallel", "arbitrary"),
        name="xattn",
    )(x, g_pre, w_cq, kv_mem, w_co, g_post)


def _rope_tables(positions):
    half = QK_ROPE // 2
    inv = ROPE_THETA ** (-jnp.arange(half, dtype=F32) / half)
    reps = LANES // half
    sign = jnp.tile(jnp.concatenate([-jnp.ones(half, F32), jnp.ones(half, F32)]), reps // 2)
    ang = positions.astype(F32).reshape(-1, 1) * jnp.tile(inv, reps)
    return jnp.cos(ang), jnp.sin(ang) * sign


def _layer(x, mem2d, positions, p, *, batch, seq, n_mem):
    row = lambda a: a.reshape(1, -1).astype(F32)
    c = RWKV_WIDTH
    d = D_MODEL

    def ffn(x, pre, gate, up, down, post):
        return _ffn(x, row(pre), gate.astype(BF16), up.astype(BF16), down.astype(BF16), row(post))

    x = ffn(x, p['n_ffn1_pre'], p['w_ffn1_gate'], p['w_ffn1_up'], p['w_ffn1_down'], p['n_ffn1_post'])

    w_in = p['w_in']
    o_rwkv = MLA_COLS
    o_gate = MLA_COLS + RWKV_COLS
    half = QK_ROPE // 2
    w_kr = w_in[:, Q_LORA + KV_LORA:MLA_COLS]
    w_kr_swap = jnp.concatenate([w_kr[:, half:], w_kr[:, :half]], axis=1)
    w_in_r = jnp.concatenate([
        w_in[:, o_gate:],
        w_in[:, o_rwkv:o_rwkv + 3 * c],
        w_in[:, :Q_LORA + KV_LORA],
        w_in[:, o_rwkv + 3 * c:o_gate],
        w_kr, w_kr, w_kr_swap, w_kr_swap,
    ], axis=1).astype(BF16)
    scale = (QK_NOPE + QK_ROPE) ** -0.5
    w_uq = p['w_uq'].reshape(Q_LORA, MLA_HEADS, QK_NOPE + QK_ROPE) * scale
    w_qr = w_uq[:, :, QK_NOPE:]
    w_qr_swap = jnp.concatenate([w_qr[:, :, half:], w_qr[:, :, :half]], axis=2)
    w_uq_r = jnp.concatenate([w_uq[:, :, :QK_NOPE].reshape(Q_LORA, -1), w_qr.reshape(Q_LORA, -1),
                              w_qr_swap.reshape(Q_LORA, -1)], axis=1).astype(BF16)
    proj, q_up, kv_up = _in_proj(x, row(p['n_mix_pre']), w_in_r, row(p['n_q_lat']), w_uq_r,
                                 row(p['n_kv_lat']), p['w_ukv'].astype(BF16))

    cos_tab, sin_tab = _rope_tables(positions)
    o_a = _mla_attention(q_up, kv_up, proj, cos_tab, sin_tab, batch=batch, seq=seq)

    head_of = jnp.arange(c) // RWKV_HEAD
    e = (head_of[:, None] == jnp.arange(LANES)[None, :]).astype(BF16)
    mu = p['mu_shift']
    r, lw, k, v, a_in, b_in, g = _rwkv_prep(
        proj, mu[:3 * c].reshape(3, c), row(mu[3 * c:]), row(p['w0']), p['w_w2'].astype(BF16),
        row(p['a0']), p['w_a2'].astype(BF16), p['w_g2'].astype(BF16), row(p['k_k']), row(p['k_a']),
        e, e.T, seq=seq)
    o_raw = _rwkv_recurrence(r, lw, k, v, a_in, b_in, batch=batch, seq=seq)

    x = _mix_tail(x, proj, p['b_gate'].reshape(2, d), o_a, p['w_oa'].astype(BF16), o_raw, r, k, v, g,
                  row(p['r_k']), row(p['lnx_w']), row(p['lnx_b']), e, e.T,
                  p['w_ob'].astype(BF16), p['w_o'].astype(BF16), row(p['n_mix_post']))

    width = MEM_HEADS * MEM_HEAD
    w_ckv = p['w_ckv'].reshape(d, MEM_HEADS, 2, MEM_HEAD).transpose(0, 2, 1, 3).reshape(d, 2 * width)
    kv_mem = _norm_matmul(mem2d, 0, d, row(p['n_mem']), w_ckv.astype(BF16), tm=512, tn=1024)
    w_cq = (p['w_cq'] * MEM_HEAD ** -0.5).astype(BF16)
    x = _xattn(x, row(p['n_x_pre']), w_cq, kv_mem, p['w_co'].astype(BF16), row(p['n_x_post']),
               batch=batch, seq=seq, n_mem=n_mem)

    return ffn(x, p['n_ffn2_pre'], p['w_ffn2_gate'], p['w_ffn2_up'], p['w_ffn2_down'], p['n_ffn2_post'])


_PARAM_NAMES = (
    'n_ffn1_pre', 'n_ffn1_post', 'w_ffn1_gate', 'w_ffn1_up', 'w_ffn1_down',
    'n_mix_pre', 'n_mix_post', 'w_in', 'b_gate',
    'n_q_lat', 'w_uq', 'n_kv_lat', 'w_ukv', 'w_oa',
    'mu_shift', 'w0', 'w_w2', 'a0', 'w_a2', 'w_g2', 'k_k', 'k_a', 'r_k', 'lnx_w', 'lnx_b', 'w_ob',
    'w_o',
    'n_x_pre', 'n_x_post', 'n_mem', 'w_cq', 'w_ckv', 'w_co',
    'n_ffn2_pre', 'n_ffn2_post', 'w_ffn2_gate', 'w_ffn2_up', 'w_ffn2_down')


def kernel(x, mem, positions, n_ffn1_pre, n_ffn1_post, w_ffn1_gate, w_ffn1_up, w_ffn1_down, n_mix_pre, n_mix_post, w_in, b_gate, n_q_lat, w_uq, n_kv_lat, w_ukv, w_oa, mu_shift, w0, w_w2, a0, w_a2, w_g2, k_k, k_a, r_k, lnx_w, lnx_b, w_ob, w_o, n_x_pre, n_x_post, n_mem, w_cq, w_ckv, w_co, n_ffn2_pre, n_ffn2_post, w_ffn2_gate, w_ffn2_up, w_ffn2_down):
    stacked = (n_ffn1_pre, n_ffn1_post, w_ffn1_gate, w_ffn1_up, w_ffn1_down, n_mix_pre, n_mix_post, w_in, b_gate,
               n_q_lat, w_uq, n_kv_lat, w_ukv, w_oa, mu_shift, w0, w_w2, a0, w_a2, w_g2, k_k, k_a, r_k, lnx_w,
               lnx_b, w_ob, w_o, n_x_pre, n_x_post, n_mem, w_cq, w_ckv, w_co, n_ffn2_pre, n_ffn2_post,
               w_ffn2_gate, w_ffn2_up, w_ffn2_down)
    batch, seq, d = x.shape
    n_mem_tokens = mem.shape[1]
    x2d = x.reshape(batch * seq, d)
    mem2d = mem.reshape(batch * n_mem_tokens, d)
    for layer in range(n_ffn1_pre.shape[0]):
        p = {name: arr[layer] for name, arr in zip(_PARAM_NAMES, stacked)}
        x2d = _layer(x2d, mem2d, positions, p, batch=batch, seq=seq, n_mem=n_mem_tokens)
    return x2d.reshape(batch, seq, d)
```

```python
import functools

import jax
import jax.numpy as jnp
from jax import lax
from jax.experimental import pallas as pl
from jax.experimental.pallas import tpu as pltpu

F32 = jnp.float32
BF16 = jnp.bfloat16

D_MODEL = 2048
D_FF = 5504
CHUNK = 64
EPS = 1e-6

MLA_HEADS = 8
QK_NOPE = 128
QK_ROPE = 64
V_HEAD = 128
Q_LORA = 512
KV_LORA = 256
ROPE_THETA = 10000.0

RWKV_HEADS = 16
RWKV_HEAD = 64
RWKV_WIDTH = RWKV_HEADS * RWKV_HEAD
DECAY_LORA = 64
A_LORA = 64
GATE_LORA = 128
LNX_EPS = 64e-5
DECAY_SCALE = 0.6065306597126334

MEM_HEADS = 4
MEM_HEAD = 256

MLA_COLS = Q_LORA + KV_LORA + QK_ROPE
RWKV_COLS = 3 * RWKV_WIDTH + DECAY_LORA + A_LORA + GATE_LORA

LANES = 128
VMEM_LIMIT_BYTES = 56 * 1024 * 1024

COL_GATE_A = 0
COL_GATE_B = 2048
COL_R = 4096
COL_K = 5120
COL_V = 6144
COL_CQ = 7168
COL_CKV = 7680
COL_SMALL = 7936
COL_KROPE = 8192
COL_KROPE_SWAP = 8320
IN_COLS_PADDED = 8448

NN = (((1,), (0,)), ((), ()))
NT = (((1,), (1,)), ((), ()))
TN = (((0,), (0,)), ((), ()))


def _dot(a, b, dims=NN):
    return lax.dot_general(a.astype(BF16), b.astype(BF16), dims, preferred_element_type=F32)


def _each(f, *cols):
    return [f(*xs) for xs in zip(*cols)]


def _split2(x):
    hi = x.astype(BF16)
    lo = (x - hi.astype(F32)).astype(BF16)
    return hi, lo


def _rms(xf, g, eps=EPS):
    return xf * lax.rsqrt(jnp.mean(xf * xf, axis=-1, keepdims=True) + eps) * g


def _sigmoid(x):
    return 1.0 / (1.0 + jnp.exp(-x))


def _params(*sem):
    return pltpu.CompilerParams(dimension_semantics=sem, vmem_limit_bytes=VMEM_LIMIT_BYTES)


def _resident(shape):
    return pl.BlockSpec(shape, lambda *_: (0,) * len(shape), pipeline_mode=pl.Buffered(1))


def _norm_matmul_kernel(x_ref, g_ref, w_ref, o_ref, h_ref):
    @pl.when(pl.program_id(1) == 0)
    def _():
        h_ref[...] = _rms(x_ref[...].astype(F32), g_ref[...]).astype(BF16)

    o_ref[...] = jnp.dot(h_ref[...], w_ref[...], preferred_element_type=F32).astype(o_ref.dtype)


def _norm_matmul(x, col_block, k_dim, g, w, *, tm, tn, out_dtype=BF16):
    m = x.shape[0]
    n = w.shape[1]
    tm = min(tm, m)
    return pl.pallas_call(
        _norm_matmul_kernel,
        grid=(m // tm, n // tn),
        in_specs=[
            pl.BlockSpec((tm, k_dim), lambda i, j: (i, col_block)),
            pl.BlockSpec((1, k_dim), lambda i, j: (0, 0)),
            pl.BlockSpec((k_dim, tn), lambda i, j: (0, j)),
        ],
        out_specs=pl.BlockSpec((tm, tn), lambda i, j: (i, j)),
        out_shape=jax.ShapeDtypeStruct((m, n), out_dtype),
        scratch_shapes=[pltpu.VMEM((tm, k_dim), BF16)],
        compiler_params=_params("parallel", "arbitrary"),
        name="norm_matmul",
    )(x, g, w)


IN_TN = 768
CQ_TILE, CQ_OFF = divmod(COL_CQ, IN_TN)
CKV_TILE, CKV_OFF = divmod(COL_CKV, IN_TN)
assert CQ_OFF + Q_LORA <= IN_TN and CKV_OFF + KV_LORA <= IN_TN and IN_COLS_PADDED % IN_TN == 0


def _in_proj_kernel(x_ref, g_ref, w_ref, gq_ref, wq_ref, gkv_ref, wkv_ref, o_ref, q_ref, kv_ref, h_ref):
    j = pl.program_id(1)

    @pl.when(j == 0)
    def _():
        h_ref[...] = _rms(x_ref[...], g_ref[...]).astype(BF16)

    y = jnp.dot(h_ref[...], w_ref[...], preferred_element_type=F32)
    o_ref[...] = y.astype(o_ref.dtype)

    def up_project(latent, gain_ref, weight_ref, out_ref):
        out_ref[...] = jnp.dot(_rms(latent, gain_ref[...]).astype(BF16), weight_ref[...],
                               preferred_element_type=F32).astype(out_ref.dtype)

    @pl.when(j == CQ_TILE)
    def _():
        up_project(y[:, CQ_OFF:CQ_OFF + Q_LORA], gq_ref, wq_ref, q_ref)

    @pl.when(j == CKV_TILE)
    def _():
        up_project(y[:, CKV_OFF:CKV_OFF + KV_LORA], gkv_ref, wkv_ref, kv_ref)


def _in_proj(x, g, w, g_q, w_uq, g_kv, w_ukv, *, tm=1024):
    m, d = x.shape
    tm = min(tm, m)
    nq, nkv = w_uq.shape[1], w_ukv.shape[1]
    return pl.pallas_call(
        _in_proj_kernel,
        grid=(m // tm, IN_COLS_PADDED // IN_TN),
        in_specs=[
            pl.BlockSpec((tm, d), lambda i, j: (i, 0)),
            _resident((1, d)),
            pl.BlockSpec((d, IN_TN), lambda i, j: (0, j)),
            _resident((1, Q_LORA)), _resident((Q_LORA, nq)),
            _resident((1, KV_LORA)), _resident((KV_LORA, nkv)),
        ],
        out_specs=[pl.BlockSpec((tm, IN_TN), lambda i, j: (i, j)),
                   pl.BlockSpec((tm, nq), lambda i, j: (i, 0)),
                   pl.BlockSpec((tm, nkv), lambda i, j: (i, 0))],
        out_shape=[jax.ShapeDtypeStruct((m, IN_COLS_PADDED), BF16),
                   jax.ShapeDtypeStruct((m, nq), BF16),
                   jax.ShapeDtypeStruct((m, nkv), BF16)],
        scratch_shapes=[pltpu.VMEM((tm, d), BF16)],
        compiler_params=_params("parallel", "arbitrary"),
        name="in_proj",
    )(x, g, w, g_q, w_uq, g_kv, w_ukv)


def _ffn_kernel(x_ref, gpre_ref, wg_ref, wu_ref, wd_ref, gpost_ref, o_ref, h_ref, acc_ref, *, tail):
    j = pl.program_id(1)
    last = pl.num_programs(1) - 1
    tf = wg_ref.shape[1]

    @pl.when(j == 0)
    def _():
        h_ref[...] = _rms(x_ref[...], gpre_ref[...]).astype(BF16)
        acc_ref[...] = jnp.zeros_like(acc_ref)

    def accumulate(valid):
        h = h_ref[...]
        gate = jnp.dot(h, wg_ref[:, :valid], preferred_element_type=F32)
        up = jnp.dot(h, wu_ref[:, :valid], preferred_element_type=F32)
        act = (gate * _sigmoid(gate) * up).astype(BF16)
        acc_ref[...] += jnp.dot(act, wd_ref[:valid, :], preferred_element_type=F32)

    if tail == tf:
        accumulate(tf)
    else:
        pl.when(j < last)(lambda: accumulate(tf))
        pl.when(j == last)(lambda: accumulate(tail))

    @pl.when(j == last)
    def _():
        o_ref[...] = x_ref[...] + 0.5 * _rms(acc_ref[...], gpost_ref[...])


def _ffn(x, g_pre, wg, wu, wd, g_post, *, tm=512, tf=512):
    m, d = x.shape
    f = wg.shape[1]
    n_tiles = pl.cdiv(f, tf)
    tm = min(tm, m)
    tail = f - (n_tiles - 1) * tf
    assert tail % LANES == 0
    return pl.pallas_call(
        functools.partial(_ffn_kernel, tail=tail),
        grid=(m // tm, n_tiles),
        in_specs=[
            pl.BlockSpec((tm, d), lambda i, j: (i, 0)),
            pl.BlockSpec((1, d), lambda i, j: (0, 0)),
            pl.BlockSpec((d, tf), lambda i, j: (0, j)),
            pl.BlockSpec((d, tf), lambda i, j: (0, j)),
            pl.BlockSpec((tf, d), lambda i, j: (j, 0)),
            pl.BlockSpec((1, d), lambda i, j: (0, 0)),
        ],
        out_specs=pl.BlockSpec((tm, d), lambda i, j: (i, 0)),
        out_shape=jax.ShapeDtypeStruct((m, d), F32),
        scratch_shapes=[pltpu.VMEM((tm, d), BF16), pltpu.VMEM((tm, d), F32)],
        compiler_params=_params("parallel", "arbitrary"),
        name="ffn",
    )(x, g_pre, wg, wu, wd, g_post)


def _mla_attn_kernel(qn_ref, qr_ref, qs_ref, cq_ref, sq_ref, kv_ref, kr_ref, ks_ref, ck_ref, sk_ref,
                     o_ref, krot_ref, *, tq, tk, n_heads):
    qi = pl.program_id(2)
    heads = list(range(n_heads))
    kv_width = QK_NOPE + V_HEAD

    @pl.when(qi == 0)
    def _():
        krot_ref[...] = (kr_ref[...].astype(F32) * ck_ref[...]
                         + ks_ref[...].astype(F32) * sk_ref[...]).astype(BF16)

    cq, sq = cq_ref[...], sq_ref[...]
    q_rot = [qr_ref[:, p * LANES:(p + 1) * LANES].astype(F32) * cq
             + qs_ref[:, p * LANES:(p + 1) * LANES].astype(F32) * sq for p in range(n_heads // 2)]
    lane = lax.broadcasted_iota(jnp.int32, (tq, LANES), 1)
    own_half = [lane < QK_ROPE, lane >= QK_ROPE]
    q_cat = [jnp.concatenate([qn_ref[:, h * QK_NOPE:(h + 1) * QK_NOPE],
                              jnp.where(own_half[h % 2], q_rot[h // 2], 0.0).astype(BF16)], axis=1)
             for h in heads]

    def scores(start):
        k_rot = krot_ref[pl.ds(start, tk), :]
        return _each(lambda h, q_: lax.dot_general(
            jnp.concatenate([kv_ref[pl.ds(start, tk), h * kv_width:h * kv_width + QK_NOPE], k_rot], axis=1), q_,
            NT, preferred_element_type=F32), heads, q_cat)

    def update(carry, s, start):
        m, l, acc = carry
        m_new = _each(lambda m_, s_: jnp.maximum(m_, jnp.max(s_, axis=0, keepdims=True)), m, s)
        alpha = _each(lambda m_, n_: jnp.exp(m_ - n_), m, m_new)
        p = _each(lambda s_, n_: jnp.exp(s_ - n_), s, m_new)
        l = _each(lambda a_, l_, p_: a_ * l_ + jnp.sum(p_, axis=0, keepdims=True), alpha, l, p)
        pv = _each(lambda h, p_: lax.dot_general(
            kv_ref[pl.ds(start, tk), h * kv_width + QK_NOPE:(h + 1) * kv_width], p_.astype(BF16), TN,
            preferred_element_type=F32), heads, p)
        acc = _each(lambda a_, acc_, pv_: a_ * acc_ + pv_, alpha, acc, pv)
        return m_new, l, acc

    def body(kj, carry):
        start = pl.multiple_of(kj * tk, tk)
        return update(carry, scores(start), start)

    init = ([jnp.full((1, tq), -1e30, F32) for _ in heads], [jnp.zeros((1, tq), F32) for _ in heads],
            [jnp.zeros((V_HEAD, tq), F32) for _ in heads])
    carry = lax.fori_loop(0, qi * (tq // tk), body, init)

    q_chunk = lax.broadcasted_iota(jnp.int32, (tk, tq), 1) // CHUNK
    for sub in range(tq // tk):
        start = pl.multiple_of(qi * tq + sub * tk, tk)
        k_chunk = (lax.broadcasted_iota(jnp.int32, (tk, tq), 0) + sub * tk) // CHUNK
        visible = k_chunk <= q_chunk
        s = _each(lambda s_: jnp.where(visible, s_, -1e30), scores(start))
        carry = update(carry, s, start)
    m, l, acc = carry
    for h in heads:
        o_ref[:, h * V_HEAD:(h + 1) * V_HEAD] = (acc[h] / l[h]).T.astype(o_ref.dtype)


def _mla_attention(q_up, kv_up, proj, cos_tab, sin_tab, *, batch, seq, tq=256, tk=256, n_heads=MLA_HEADS):
    nq = seq // tq
    rope_w = n_heads * QK_ROPE
    rope0 = MLA_HEADS * QK_NOPE // rope_w
    swap0 = (MLA_HEADS * QK_NOPE + MLA_HEADS * QK_ROPE) // rope_w
    q_rows = lambda width, col0: pl.BlockSpec((tq, width), lambda b, g, i: (b * nq + i, col0 + g))
    q_tab = pl.BlockSpec((tq, LANES), lambda b, g, i: (b * nq + i, 0))
    k_tab = pl.BlockSpec((seq, LANES), lambda b, g, i: (b, 0))
    k_cols = lambda col: pl.BlockSpec((seq, LANES), lambda b, g, i: (b, col // LANES))
    return pl.pallas_call(
        functools.partial(_mla_attn_kernel, tq=tq, tk=tk, n_heads=n_heads),
        grid=(batch, MLA_HEADS // n_heads, nq),
        in_specs=[
            q_rows(n_heads * QK_NOPE, 0), q_rows(rope_w, rope0), q_rows(rope_w, swap0), q_tab, q_tab,
            pl.BlockSpec((seq, n_heads * (QK_NOPE + V_HEAD)), lambda b, g, i: (b, g)),
            k_cols(COL_KROPE), k_cols(COL_KROPE_SWAP), k_tab, k_tab,
        ],
        out_specs=pl.BlockSpec((tq, n_heads * V_HEAD), lambda b, g, i: (b * nq + i, g)),
        out_shape=jax.ShapeDtypeStruct((batch * seq, MLA_HEADS * V_HEAD), BF16),
        scratch_shapes=[pltpu.VMEM((seq, LANES), BF16)],
        compiler_params=_params("parallel", "parallel", "arbitrary"),
        name="mla_attention",
    )(q_up, q_up, q_up, cos_tab, sin_tab, kv_up, proj, proj, cos_tab, sin_tab)


def _head_sum(x, e_ref, et_ref):
    s = _dot(x, e_ref[...])
    return _dot(s, et_ref[...])


def _shift(y, prev_row, mu, first):
    rows = lax.broadcasted_iota(jnp.int32, y.shape, 0)
    prev0 = jnp.where(first, 0.0, prev_row)
    y_prev = jnp.where(rows == 0, prev0, pltpu.roll(y, 1, 0))
    return y + (y_prev - y) * mu


def _rwkv_prep_kernel(r_ref, k_ref, v_ref, s_ref, rp_ref, kp_ref, vp_ref, sp_ref,
                      mu_rkv_ref, mu_s_ref, w0_ref, ww2_ref, a0_ref, wa2_ref, wg2_ref,
                      kk_ref, ka_ref, e_ref, et_ref,
                      ro_ref, lw_ref, ko_ref, vo_ref, ao_ref, bo_ref, go_ref, *, tiles_per_seq):
    first = (pl.program_id(0) % tiles_per_seq) == 0
    last8 = slice(7, 8)

    def shifted(ref, pref, mu):
        return _shift(ref[...].astype(F32), pref[last8, :].astype(F32), mu, first)

    r = shifted(r_ref, rp_ref, mu_rkv_ref[0:1, :])
    k = shifted(k_ref, kp_ref, mu_rkv_ref[1:2, :])
    v = shifted(v_ref, vp_ref, mu_rkv_ref[2:3, :])
    small = shifted(s_ref, sp_ref, mu_s_ref[...])
    dw = small[:, 0:DECAY_LORA]
    da = small[:, DECAY_LORA:DECAY_LORA + A_LORA]
    dg = small[:, DECAY_LORA + A_LORA:]

    u = w0_ref[...] + _dot(jnp.tanh(dw), ww2_ref[...])
    lw_ref[...] = -DECAY_SCALE * _sigmoid(u)
    a = _sigmoid(a0_ref[...] + _dot(da, wa2_ref[...]))
    go_ref[...] = _dot(_sigmoid(dg), wg2_ref[...]).astype(go_ref.dtype)

    kk = k * kk_ref[...]
    kk = kk * lax.rsqrt(jnp.maximum(_head_sum(kk * kk, e_ref, et_ref), 1e-24))
    ro_ref[...] = r.astype(ro_ref.dtype)
    ko_ref[...] = (k * (1.0 + (a - 1.0) * ka_ref[...])).astype(ko_ref.dtype)
    vo_ref[...] = v.astype(vo_ref.dtype)
    ao_ref[...] = (-kk).astype(ao_ref.dtype)
    bo_ref[...] = (kk * a).astype(bo_ref.dtype)


def _rwkv_prep(proj, mu_rkv, mu_s, w0, ww2, a0, wa2, wg2, k_k, k_a, e, et, *, seq, tm=256):
    t = proj.shape[0]
    c = RWKV_WIDTH
    small_w = DECAY_LORA + A_LORA + GATE_LORA

    def cur(width, col):
        return pl.BlockSpec((tm, width), lambda i: (i, col // width))

    def prev(width, col):
        return pl.BlockSpec((8, width), lambda i: (jnp.maximum(i * (tm // 8) - 1, 0), col // width))

    def const(shape):
        return pl.BlockSpec(shape, lambda i: (0, 0))

    out = jax.ShapeDtypeStruct((t, c), BF16)
    return pl.pallas_call(
        functools.partial(_rwkv_prep_kernel, tiles_per_seq=seq // tm),
        grid=(t // tm,),
        in_specs=[cur(c, COL_R), cur(c, COL_K), cur(c, COL_V), cur(small_w, COL_SMALL),
                  prev(c, COL_R), prev(c, COL_K), prev(c, COL_V), prev(small_w, COL_SMALL),
                  const((3, c)), const((1, small_w)), const((1, c)), const((DECAY_LORA, c)),
                  const((1, c)), const((A_LORA, c)), const((GATE_LORA, c)),
                  const((1, c)), const((1, c)), const((c, LANES)), const((LANES, c))],
        out_specs=[pl.BlockSpec((tm, c), lambda i: (i, 0))] * 7,
        out_shape=[out, jax.ShapeDtypeStruct((t, c), F32), out, out, out, out, out],
        compiler_params=_params("parallel"),
        name="rwkv_prep",
    )(proj, proj, proj, proj, proj, proj, proj, proj,
      mu_rkv, mu_s, w0, ww2, a0, wa2, wg2, k_k, k_a, e, et)


def _block_diag(x):
    lane = lax.broadcasted_iota(jnp.int32, x.shape, 1)
    zero = jnp.zeros_like(x)
    return jnp.concatenate([jnp.where(lane < RWKV_HEAD, x, zero),
                            jnp.where(lane >= RWKV_HEAD, x, zero)], axis=0)


def _unit_lower_inverse(a, eye, level_masks):
    c = CHUNK
    base = _each(lambda x: x * level_masks[0], a)
    x2 = _each(lambda x: _dot(x, _block_diag(x)), base)
    t = _each(lambda x: eye + x, base)
    tx = _each(lambda t_, x_: _dot(jnp.concatenate([t_, x_], axis=0), _block_diag(x_)), t, x2)
    t = _each(lambda t_, tx_: t_ + tx_[:c], t, tx)
    t = _each(lambda t_, tx_: t_ + _dot(t_, _block_diag(tx_[c:])), t, tx)
    for mask in level_masks[1:]:
        ta = _each(lambda t_, a_: _dot(t_, _block_diag(a_ * mask)), t, a)
        t = _each(lambda t_, ta_: t_ + _dot(ta_, _block_diag(t_)), t, ta)
    return t


def _chunk_prepare(r, lw, k, v, a, b, consts):
    ltri, strict, incl, eye, _, level_masks = consts
    c = CHUNK
    rows = lambda x, y: jnp.concatenate([x, y], axis=0)

    def cumulative(x):
        both = jnp.dot(ltri, jnp.concatenate(_split2(x), axis=1), preferred_element_type=F32)
        return both[:, :x.shape[1]] + both[:, x.shape[1]:]

    cl = _each(cumulative, lw)
    cl_end = _each(lambda x: x[c - 1:c, :], cl)
    p_inv = _each(lambda x: jnp.exp(-x), cl)
    a_t = _each(lambda a_, cl_, lw_: a_ * jnp.exp(cl_ - lw_), a, cl, lw)
    r_t = _each(lambda r_, cl_: r_ * jnp.exp(cl_), r, cl)
    b_t = _each(jnp.multiply, b, p_inv)
    k_t = _each(jnp.multiply, k, p_inv)

    ar = _each(rows, a_t, r_t)
    g = _each(lambda x, b_, k_: _dot(x, rows(_block_diag(b_), _block_diag(k_)), NT), ar, b_t, k_t)
    width = b_t[0].shape[1]
    a_ab = _each(lambda g_: jnp.where(strict, g_[:c, :width], 0.0), g)
    a_rb = _each(lambda g_: jnp.where(incl, g_[c:, :width], 0.0), g)
    a_k = _each(lambda g_: rows(jnp.where(strict, g_[:c, width:], 0.0), jnp.where(incl, g_[c:, width:], 0.0)), g)

    t_inv = _unit_lower_inverse(a_ab, eye, level_masks)

    av = _each(lambda x, v_: _dot(x, _block_diag(v_)), a_k, v)
    tw = _each(lambda t_, a_, av_: _dot(t_, jnp.concatenate([_block_diag(a_), _block_diag(av_[:c])], axis=1)),
               t_inv, a_t, av)
    lhs = _each(lambda tw_, r_: rows(tw_[:, :LANES], r_).astype(BF16), tw, r_t)
    w_eff = _each(lambda tw_: tw_[:, LANES:], tw)
    o_local = _each(lambda av_: av_[c:], av)
    p_tail = _each(lambda e_, cl_: jnp.exp(e_ - cl_), cl_end, cl)
    bk_tail = _each(lambda b_, k_, p_: rows(b_ * p_, k_ * p_).astype(BF16), b, k, p_tail)
    p_end = _each(jnp.exp, cl_end)
    return lhs, w_eff, a_rb, o_local, v, bk_tail, p_end


def _chunk_apply(prepared, state, diag_blocks):
    c = CHUNK
    lhs, w_eff, a_rb, o_local, v, bk_tail, p_end = prepared
    uo = _each(lambda lhs_, s_: _dot(lhs_, s_, NT), lhs, state)
    u = _each(lambda uo_, w_: uo_[:c] + w_, uo, w_eff)
    o = _each(lambda uo_, arb_, u_, ol_: uo_[c:] + _dot(arb_, _block_diag(u_)) + ol_, uo, a_rb, u, o_local)
    upd = _each(lambda u_, v_, bk_: _dot(jnp.concatenate([u_, v_], axis=0), bk_, TN), u, v, bk_tail)
    new_state = _each(lambda s_, p_, upd_: s_ * p_ + jnp.where(diag_blocks, upd_, 0.0), state, p_end, upd)
    return o, new_state


def _rwkv_kernel(r_ref, lw_ref, k_ref, v_ref, a_ref, b_ref, o_ref, s_ref, *, n_chunks, n_pairs, unroll):
    @pl.when(pl.program_id(2) == 0)
    def _():
        s_ref[...] = jnp.zeros_like(s_ref)

    c = CHUNK
    row = lax.broadcasted_iota(jnp.int32, (c, LANES), 0)
    col = lax.broadcasted_iota(jnp.int32, (c, LANES), 1) % RWKV_HEAD
    ltri = jnp.where(lax.broadcasted_iota(jnp.int32, (c, c), 1) <= lax.broadcasted_iota(jnp.int32, (c, c), 0),
                     1.0, 0.0).astype(BF16)
    rr = lax.broadcasted_iota(jnp.int32, (LANES, LANES), 0) // RWKV_HEAD
    cc = lax.broadcasted_iota(jnp.int32, (LANES, LANES), 1) // RWKV_HEAD
    def same(block):
        return (row // block) == (col // block)
    level_masks = [jnp.where(same(8), 1.0, 0.0)]
    for block in (8, 16, 32):
        level_masks.append(jnp.where(same(2 * block), 1.0, 0.0) - jnp.where(same(block), 1.0, 0.0))
    eye = jnp.where(col == row, 1.0, 0.0)
    consts = (ltri, col < row, col <= row, eye, rr == cc, level_masks)

    def chunk_group(gi, carry):
        r0 = pl.multiple_of(gi * (unroll * c), unroll * c)
        tiles = [(pl.ds(r0 + u * c, c), slice(p * LANES, (p + 1) * LANES))
                 for u in range(unroll) for p in range(n_pairs)]
        load = lambda ref: [ref[sl].astype(F32) for sl in tiles]
        prepared = _chunk_prepare(load(r_ref), load(lw_ref), load(k_ref), load(v_ref), load(a_ref), load(b_ref),
                                  consts)
        state = [s_ref[p] for p in range(n_pairs)]
        for u in range(unroll):
            part = slice(u * n_pairs, (u + 1) * n_pairs)
            o, state = _chunk_apply([x[part] for x in prepared], state, consts[4])
            for p in range(n_pairs):
                o_ref[tiles[u * n_pairs + p]] = o[p]
        for p in range(n_pairs):
            s_ref[p] = state[p]
        return carry

    lax.fori_loop(0, n_chunks // unroll, chunk_group, 0)


def _rwkv_recurrence(r, lw, k, v, a, b, *, batch, seq, tm=512, n_pairs=8, unroll=2):
    t, c = r.shape
    width = n_pairs * LANES
    tm = min(tm, seq)
    nt = seq // tm
    assert (tm // CHUNK) % unroll == 0
    spec = pl.BlockSpec((tm, width), lambda bi, g, i: (bi * nt + i, g))
    return pl.pallas_call(
        functools.partial(_rwkv_kernel, n_chunks=tm // CHUNK, n_pairs=n_pairs, unroll=unroll),
        grid=(batch, c // width, nt),
        in_specs=[spec] * 6,
        out_specs=spec,
        out_shape=jax.ShapeDtypeStruct((t, c), F32),
        scratch_shapes=[pltpu.VMEM((n_pairs, LANES, LANES), F32)],
        compiler_params=_params("parallel", "parallel", "arbitrary"),
        name="rwkv_recurrence",
    )(r, lw, k, v, a, b)


def _mix_tail_kernel(x_ref, ga_ref, gb_ref, bg_ref, oa_ref, woa_ref,
                     o_ref, r_ref, k_ref, v_ref, g_ref, rk_ref, lw_ref, lb_ref, e_ref, et_ref,
                     wob_ref, wo_ref, gpost_ref, out_ref):
    inv_n = 1.0 / RWKV_HEAD
    o = o_ref[...]
    mean = _head_sum(o, e_ref, et_ref) * inv_n
    cen = o - mean
    var = _head_sum(cen * cen, e_ref, et_ref) * inv_n
    on = cen * lax.rsqrt(var + LNX_EPS) * lw_ref[...] + lb_ref[...]
    v = v_ref[...].astype(F32)
    rk = r_ref[...].astype(F32) * k_ref[...].astype(F32) * rk_ref[...]
    bonus = _head_sum(rk, e_ref, et_ref) * v
    ob = ((on + bonus) * g_ref[...].astype(F32)).astype(BF16)

    y_a = jnp.dot(oa_ref[...], woa_ref[...], preferred_element_type=F32)
    y_b = jnp.dot(ob, wob_ref[...], preferred_element_type=F32)
    gate_a = _sigmoid(ga_ref[...].astype(F32) + bg_ref[0:1, :])
    gate_b = _sigmoid(gb_ref[...].astype(F32) + bg_ref[1:2, :])
    merged = (gate_a * y_a + gate_b * y_b).astype(BF16)
    z = jnp.dot(merged, wo_ref[...], preferred_element_type=F32)
    out_ref[...] = x_ref[...] + _rms(z, gpost_ref[...])


def _mix_tail(x, proj, b_gate, o_a, w_oa, o_raw, r, k, v, g, r_k, lnx_w, lnx_b, e, et, w_ob, w_o, g_post, *, tm=256):
    t, d = x.shape
    c = RWKV_WIDTH
    row = lambda width, col=0: pl.BlockSpec((tm, width), lambda i: (i, col // width))
    return pl.pallas_call(
        _mix_tail_kernel,
        grid=(t // tm,),
        in_specs=[row(d), row(d, COL_GATE_A), row(d, COL_GATE_B), _resident((2, d)),
                  row(c), _resident((c, d)),
                  row(c), row(c), row(c), row(c), row(c),
                  _resident((1, c)), _resident((1, c)), _resident((1, c)),
                  _resident((c, LANES)), _resident((LANES, c)),
                  _resident((c, d)), _resident((d, d)), _resident((1, d))],
        out_specs=row(d),
        out_shape=jax.ShapeDtypeStruct((t, d), F32),
        compiler_params=_params("parallel"),
        name="mix_tail",
    )(x, proj, proj, b_gate, o_a, w_oa, o_raw, r, k, v, g, r_k, lnx_w, lnx_b, e, et, w_ob, w_o, g_post)


def _xattn_kernel(x_ref, gpre_ref, wq_ref, kv_ref, wo_ref, gpost_ref, out_ref):
    x = x_ref[...]
    h = _rms(x, gpre_ref[...]).astype(BF16)
    q = jnp.dot(h, wq_ref[...], preferred_element_type=F32).astype(BF16)
    width = MEM_HEADS * MEM_HEAD
    heads = []
    for hd in range(MEM_HEADS):
        lo = hd * MEM_HEAD
        k = kv_ref[:, lo:lo + MEM_HEAD]
        v = kv_ref[:, width + lo:width + lo + MEM_HEAD]
        s = lax.dot_general(q[:, lo:lo + MEM_HEAD], k, NT, preferred_element_type=F32)
        p = jnp.exp(s - jnp.max(s, axis=-1, keepdims=True))
        p = p / jnp.sum(p, axis=-1, keepdims=True)
        heads.append(jnp.dot(p.astype(BF16), v, preferred_element_type=F32).astype(BF16))
    o = jnp.concatenate(heads, axis=1)
    z = jnp.dot(o, wo_ref[...], preferred_element_type=F32)
    out_ref[...] = x + _rms(z, gpost_ref[...])


def _xattn(x, g_pre, w_cq, kv_mem, w_co, g_post, *, batch, seq, n_mem, tq=512):
    t, d = x.shape
    tq = min(tq, seq)
    nq = seq // tq
    width = MEM_HEADS * MEM_HEAD
    return pl.pallas_call(
        _xattn_kernel,
        grid=(batch, nq),
        in_specs=[pl.BlockSpec((tq, d), lambda b, i: (b * nq + i, 0)),
                  _resident((1, d)), _resident((d, width)),
                  pl.BlockSpec((n_mem, 2 * width), lambda b, i: (b, 0)),
                  _resident((width, d)), _resident((1, d))],
        out_specs=pl.BlockSpec((tq, d), lambda b, i: (b * nq + i, 0)),
        out_shape=jax.ShapeDtypeStruct((t, d), F32),
        compiler_params=_params("parallel", "arbitrary"),
        name="xattn",
    )(x, g_pre, w_cq, kv_mem, w_co, g_post)


def _rope_tables(positions):
    half = QK_ROPE // 2
    inv = ROPE_THETA ** (-jnp.arange(half, dtype=F32) / half)
    reps = LANES // half
    sign = jnp.tile(jnp.concatenate([-jnp.ones(half, F32), jnp.ones(half, F32)]), reps // 2)
    ang = positions.astype(F32).reshape(-1, 1) * jnp.tile(inv, reps)
    return jnp.cos(ang), jnp.sin(ang) * sign


def _layer(x, mem2d, positions, p, *, batch, seq, n_mem):
    row = lambda a: a.reshape(1, -1).astype(F32)
    c = RWKV_WIDTH
    d = D_MODEL

    def ffn(x, pre, gate, up, down, post):
        return _ffn(x, row(pre), gate.astype(BF16), up.astype(BF16), down.astype(BF16), row(post))

    x = ffn(x, p['n_ffn1_pre'], p['w_ffn1_gate'], p['w_ffn1_up'], p['w_ffn1_down'], p['n_ffn1_post'])

    w_in = p['w_in']
    o_rwkv = MLA_COLS
    o_gate = MLA_COLS + RWKV_COLS
    half = QK_ROPE // 2
    w_kr = w_in[:, Q_LORA + KV_LORA:MLA_COLS]
    w_kr_swap = jnp.concatenate([w_kr[:, half:], w_kr[:, :half]], axis=1)
    w_in_r = jnp.concatenate([
        w_in[:, o_gate:],
        w_in[:, o_rwkv:o_rwkv + 3 * c],
        w_in[:, :Q_LORA + KV_LORA],
        w_in[:, o_rwkv + 3 * c:o_gate],
        w_kr, w_kr, w_kr_swap, w_kr_swap,
    ], axis=1).astype(BF16)
    scale = (QK_NOPE + QK_ROPE) ** -0.5
    w_uq = p['w_uq'].reshape(Q_LORA, MLA_HEADS, QK_NOPE + QK_ROPE) * scale
    w_qr = w_uq[:, :, QK_NOPE:]
    w_qr_swap = jnp.concatenate([w_qr[:, :, half:], w_qr[:, :, :half]], axis=2)
    w_uq_r = jnp.concatenate([w_uq[:, :, :QK_NOPE].reshape(Q_LORA, -1), w_qr.reshape(Q_LORA, -1),
                              w_qr_swap.reshape(Q_LORA, -1)], axis=1).astype(BF16)
    proj, q_up, kv_up = _in_proj(x, row(p['n_mix_pre']), w_in_r, row(p['n_q_lat']), w_uq_r,
                                 row(p['n_kv_lat']), p['w_ukv'].astype(BF16))

    cos_tab, sin_tab = _rope_tables(positions)
    o_a = _mla_attention(q_up, kv_up, proj, cos_tab, sin_tab, batch=batch, seq=seq)

    head_of = jnp.arange(c) // RWKV_HEAD
    e = (head_of[:, None] == jnp.arange(LANES)[None, :]).astype(BF16)
    mu = p['mu_shift']
    r, lw, k, v, a_in, b_in, g = _rwkv_prep(
        proj, mu[:3 * c].reshape(3, c), row(mu[3 * c:]), row(p['w0']), p['w_w2'].astype(BF16),
        row(p['a0']), p['w_a2'].astype(BF16), p['w_g2'].astype(BF16), row(p['k_k']), row(p['k_a']),
        e, e.T, seq=seq)
    o_raw = _rwkv_recurrence(r, lw, k, v, a_in, b_in, batch=batch, seq=seq)

    x = _mix_tail(x, proj, p['b_gate'].reshape(2, d), o_a, p['w_oa'].astype(BF16), o_raw, r, k, v, g,
                  row(p['r_k']), row(p['lnx_w']), row(p['lnx_b']), e, e.T,
                  p['w_ob'].astype(BF16), p['w_o'].astype(BF16), row(p['n_mix_post']))

    width = MEM_HEADS * MEM_HEAD
    w_ckv = p['w_ckv'].reshape(d, MEM_HEADS, 2, MEM_HEAD).transpose(0, 2, 1, 3).reshape(d, 2 * width)
    kv_mem = _norm_matmul(mem2d, 0, d, row(p['n_mem']), w_ckv.astype(BF16), tm=512, tn=1024)
    w_cq = (p['w_cq'] * MEM_HEAD ** -0.5).astype(BF16)
    x = _xattn(x, row(p['n_x_pre']), w_cq, kv_mem, p['w_co'].astype(BF16), row(p['n_x_post']),
               batch=batch, seq=seq, n_mem=n_mem)

    return ffn(x, p['n_ffn2_pre'], p['w_ffn2_gate'], p['w_ffn2_up'], p['w_ffn2_down'], p['n_ffn2_post'])


_PARAM_NAMES = (
    'n_ffn1_pre', 'n_ffn1_post', 'w_ffn1_gate', 'w_ffn1_up', 'w_ffn1_down',
    'n_mix_pre', 'n_mix_post', 'w_in', 'b_gate',
    'n_q_lat', 'w_uq', 'n_kv_lat', 'w_ukv', 'w_oa',
    'mu_shift', 'w0', 'w_w2', 'a0', 'w_a2', 'w_g2', 'k_k', 'k_a', 'r_k', 'lnx_w', 'lnx_b', 'w_ob',
    'w_o',
    'n_x_pre', 'n_x_post', 'n_mem', 'w_cq', 'w_ckv', 'w_co',
    'n_ffn2_pre', 'n_ffn2_post', 'w_ffn2_gate', 'w_ffn2_up', 'w_ffn2_down')


def kernel(x, mem, positions, n_ffn1_pre, n_ffn1_post, w_ffn1_gate, w_ffn1_up, w_ffn1_down, n_mix_pre, n_mix_post, w_in, b_gate, n_q_lat, w_uq, n_kv_lat, w_ukv, w_oa, mu_shift, w0, w_w2, a0, w_a2, w_g2, k_k, k_a, r_k, lnx_w, lnx_b, w_ob, w_o, n_x_pre, n_x_post, n_mem, w_cq, w_ckv, w_co, n_ffn2_pre, n_ffn2_post, w_ffn2_gate, w_ffn2_up, w_ffn2_down):
    stacked = (n_ffn1_pre, n_ffn1_post, w_ffn1_gate, w_ffn1_up, w_ffn1_down, n_mix_pre, n_mix_post, w_in, b_gate,
               n_q_lat, w_uq, n_kv_lat, w_ukv, w_oa, mu_shift, w0, w_w2, a0, w_a2, w_g2, k_k, k_a, r_k, lnx_w,
               lnx_b, w_ob, w_o, n_x_pre, n_x_post, n_mem, w_cq, w_ckv, w_co, n_ffn2_pre, n_ffn2_post,
               w_ffn2_gate, w_ffn2_up, w_ffn2_down)
    batch, seq, d = x.shape
    n_mem_tokens = mem.shape[1]
    x2d = x.reshape(batch * seq, d)
    mem2d = mem.reshape(batch * n_mem_tokens, d)
    for layer in range(n_ffn1_pre.shape[0]):
        p = {name: arr[layer] for name, arr in zip(_PARAM_NAMES, stacked)}
        x2d = _layer(x2d, mem2d, positions, p, batch=batch, seq=seq, n_mem=n_mem_tokens)
    return x2d.reshape(batch, seq, d)
```

```python
import functools

import jax
import jax.numpy as jnp
from jax import lax
from jax.experimental import pallas as pl
from jax.experimental.pallas import tpu as pltpu

F32 = jnp.float32
BF16 = jnp.bfloat16

D_MODEL = 2048
D_FF = 5504
CHUNK = 64
EPS = 1e-6

MLA_HEADS = 8
QK_NOPE = 128
QK_ROPE = 64
V_HEAD = 128
Q_LORA = 512
KV_LORA = 256
ROPE_THETA = 10000.0

RWKV_HEADS = 16
RWKV_HEAD = 64
RWKV_WIDTH = RWKV_HEADS * RWKV_HEAD
DECAY_LORA = 64
A_LORA = 64
GATE_LORA = 128
LNX_EPS = 64e-5
DECAY_SCALE = 0.6065306597126334

MEM_HEADS = 4
MEM_HEAD = 256

MLA_COLS = Q_LORA + KV_LORA + QK_ROPE
RWKV_COLS = 3 * RWKV_WIDTH + DECAY_LORA + A_LORA + GATE_LORA

LANES = 128
SUBLANES = 8
VMEM_LIMIT_BYTES = 56 * 1024 * 1024

SMALL_LORA = DECAY_LORA + A_LORA + GATE_LORA
COL_GATE_A = 0
COL_GATE_B = COL_GATE_A + D_MODEL
COL_R = COL_GATE_B + D_MODEL
COL_K = COL_R + RWKV_WIDTH
COL_V = COL_K + RWKV_WIDTH
COL_CQ = COL_V + RWKV_WIDTH
COL_CKV = COL_CQ + Q_LORA
COL_SMALL = COL_CKV + KV_LORA
COL_KROPE = COL_SMALL + SMALL_LORA
COL_KROPE_SWAP = COL_KROPE + 2 * QK_ROPE
IN_COLS = COL_KROPE_SWAP + 2 * QK_ROPE

NN = (((1,), (0,)), ((), ()))
NT = (((1,), (1,)), ((), ()))
TN = (((0,), (0,)), ((), ()))


def _dot(a, b, dims=NN):
    return lax.dot_general(a.astype(BF16), b.astype(BF16), dims, preferred_element_type=F32)


def _each(f, *cols):
    return [f(*xs) for xs in zip(*cols)]


def _split2(x):
    hi = x.astype(BF16)
    lo = (x - hi.astype(F32)).astype(BF16)
    return hi, lo


def _rms(xf, g, eps=EPS):
    return xf * lax.rsqrt(jnp.mean(xf * xf, axis=-1, keepdims=True) + eps) * g


def _sigmoid(x):
    return 1.0 / (1.0 + jnp.exp(-x))


def _params(*sem):
    return pltpu.CompilerParams(dimension_semantics=sem, vmem_limit_bytes=VMEM_LIMIT_BYTES)


def _resident(shape):
    return pl.BlockSpec(shape, lambda *_: (0,) * len(shape), pipeline_mode=pl.Buffered(1))


def _norm_matmul_kernel(x_ref, g_ref, w_ref, o_ref, h_ref):
    @pl.when(pl.program_id(1) == 0)
    def _():
        h_ref[...] = _rms(x_ref[...], g_ref[...]).astype(BF16)

    o_ref[...] = jnp.dot(h_ref[...], w_ref[...], preferred_element_type=F32).astype(o_ref.dtype)


def _norm_matmul(x, g, w, *, tm, tn):
    m, k_dim = x.shape
    n = w.shape[1]
    tm = min(tm, m)
    return pl.pallas_call(
        _norm_matmul_kernel,
        grid=(m // tm, n // tn),
        in_specs=[
            pl.BlockSpec((tm, k_dim), lambda i, j: (i, 0)),
            pl.BlockSpec((1, k_dim), lambda i, j: (0, 0)),
            pl.BlockSpec((k_dim, tn), lambda i, j: (0, j)),
        ],
        out_specs=pl.BlockSpec((tm, tn), lambda i, j: (i, j)),
        out_shape=jax.ShapeDtypeStruct((m, n), BF16),
        scratch_shapes=[pltpu.VMEM((tm, k_dim), BF16)],
        compiler_params=_params("parallel", "arbitrary"),
        name="norm_matmul",
    )(x, g, w)


IN_TN = 768
CQ_TILE, CQ_OFF = divmod(COL_CQ, IN_TN)
CKV_TILE, CKV_OFF = divmod(COL_CKV, IN_TN)
assert CQ_OFF + Q_LORA <= IN_TN and CKV_OFF + KV_LORA <= IN_TN and IN_COLS % IN_TN == 0


def _in_proj_kernel(x_ref, g_ref, w_ref, gq_ref, wq_ref, gkv_ref, wkv_ref, o_ref, q_ref, kv_ref, h_ref):
    j = pl.program_id(1)

    @pl.when(j == 0)
    def _():
        h_ref[...] = _rms(x_ref[...], g_ref[...]).astype(BF16)

    y = jnp.dot(h_ref[...], w_ref[...], preferred_element_type=F32)
    o_ref[...] = y.astype(o_ref.dtype)

    def up_project(latent, gain_ref, weight_ref, out_ref):
        out_ref[...] = jnp.dot(_rms(latent, gain_ref[...]).astype(BF16), weight_ref[...],
                               preferred_element_type=F32).astype(out_ref.dtype)

    @pl.when(j == CQ_TILE)
    def _():
        up_project(y[:, CQ_OFF:CQ_OFF + Q_LORA], gq_ref, wq_ref, q_ref)

    @pl.when(j == CKV_TILE)
    def _():
        up_project(y[:, CKV_OFF:CKV_OFF + KV_LORA], gkv_ref, wkv_ref, kv_ref)


def _in_proj(x, g, w, g_q, w_uq, g_kv, w_ukv, *, tm=1024):
    m, d = x.shape
    tm = min(tm, m)
    nq, nkv = w_uq.shape[1], w_ukv.shape[1]
    return pl.pallas_call(
        _in_proj_kernel,
        grid=(m // tm, IN_COLS // IN_TN),
        in_specs=[
            pl.BlockSpec((tm, d), lambda i, j: (i, 0)),
            _resident((1, d)),
            pl.BlockSpec((d, IN_TN), lambda i, j: (0, j)),
            _resident((1, Q_LORA)), _resident((Q_LORA, nq)),
            _resident((1, KV_LORA)), _resident((KV_LORA, nkv)),
        ],
        out_specs=[pl.BlockSpec((tm, IN_TN), lambda i, j: (i, j)),
                   pl.BlockSpec((tm, nq), lambda i, j: (i, 0)),
                   pl.BlockSpec((tm, nkv), lambda i, j: (i, 0))],
        out_shape=[jax.ShapeDtypeStruct((m, IN_COLS), BF16),
                   jax.ShapeDtypeStruct((m, nq), BF16),
                   jax.ShapeDtypeStruct((m, nkv), BF16)],
        scratch_shapes=[pltpu.VMEM((tm, d), BF16)],
        compiler_params=_params("parallel", "arbitrary"),
        name="in_proj",
    )(x, g, w, g_q, w_uq, g_kv, w_ukv)


def _ffn_kernel(x_ref, gpre_ref, wg_ref, wu_ref, wd_ref, gpost_ref, o_ref, h_ref, acc_ref, *, tail):
    j = pl.program_id(1)
    last = pl.num_programs(1) - 1
    tf = wg_ref.shape[1]

    @pl.when(j == 0)
    def _():
        h_ref[...] = _rms(x_ref[...], gpre_ref[...]).astype(BF16)
        acc_ref[...] = jnp.zeros_like(acc_ref)

    def accumulate(valid):
        h = h_ref[...]
        gate = jnp.dot(h, wg_ref[:, :valid], preferred_element_type=F32)
        up = jnp.dot(h, wu_ref[:, :valid], preferred_element_type=F32)
        act = (gate * _sigmoid(gate) * up).astype(BF16)
        acc_ref[...] += jnp.dot(act, wd_ref[:valid, :], preferred_element_type=F32)

    if tail == tf:
        accumulate(tf)
    else:
        pl.when(j < last)(lambda: accumulate(tf))
        pl.when(j == last)(lambda: accumulate(tail))

    @pl.when(j == last)
    def _():
        o_ref[...] = x_ref[...] + 0.5 * _rms(acc_ref[...], gpost_ref[...])


def _ffn(x, g_pre, wg, wu, wd, g_post, *, tm=512, tf=512):
    m, d = x.shape
    f = wg.shape[1]
    n_tiles = pl.cdiv(f, tf)
    tm = min(tm, m)
    tail = f - (n_tiles - 1) * tf
    assert tail % LANES == 0
    return pl.pallas_call(
        functools.partial(_ffn_kernel, tail=tail),
        grid=(m // tm, n_tiles),
        in_specs=[
            pl.BlockSpec((tm, d), lambda i, j: (i, 0)),
            pl.BlockSpec((1, d), lambda i, j: (0, 0)),
            pl.BlockSpec((d, tf), lambda i, j: (0, j)),
            pl.BlockSpec((d, tf), lambda i, j: (0, j)),
            pl.BlockSpec((tf, d), lambda i, j: (j, 0)),
            pl.BlockSpec((1, d), lambda i, j: (0, 0)),
        ],
        out_specs=pl.BlockSpec((tm, d), lambda i, j: (i, 0)),
        out_shape=jax.ShapeDtypeStruct((m, d), F32),
        scratch_shapes=[pltpu.VMEM((tm, d), BF16), pltpu.VMEM((tm, d), F32)],
        compiler_params=_params("parallel", "arbitrary"),
        name="ffn",
    )(x, g_pre, wg, wu, wd, g_post)


def _mla_attn_kernel(qn_ref, qr_ref, qs_ref, cq_ref, sq_ref, kv_ref, kr_ref, ks_ref, ck_ref, sk_ref,
                     o_ref, krot_ref, *, tq, tk, n_heads):
    qi = pl.program_id(2)
    heads = list(range(n_heads))
    kv_width = QK_NOPE + V_HEAD

    @pl.when(qi == 0)
    def _():
        krot_ref[...] = (kr_ref[...].astype(F32) * ck_ref[...]
                         + ks_ref[...].astype(F32) * sk_ref[...]).astype(BF16)

    cq, sq = cq_ref[...], sq_ref[...]
    q_rot = [qr_ref[:, p * LANES:(p + 1) * LANES].astype(F32) * cq
             + qs_ref[:, p * LANES:(p + 1) * LANES].astype(F32) * sq for p in range(n_heads // 2)]
    lane = lax.broadcasted_iota(jnp.int32, (tq, LANES), 1)
    own_half = [lane < QK_ROPE, lane >= QK_ROPE]
    q_cat = [jnp.concatenate([qn_ref[:, h * QK_NOPE:(h + 1) * QK_NOPE],
                              jnp.where(own_half[h % 2], q_rot[h // 2], 0.0).astype(BF16)], axis=1)
             for h in heads]

    def scores(start):
        k_rot = krot_ref[pl.ds(start, tk), :]
        return _each(lambda h, q_: lax.dot_general(
            jnp.concatenate([kv_ref[pl.ds(start, tk), h * kv_width:h * kv_width + QK_NOPE], k_rot], axis=1), q_,
            NT, preferred_element_type=F32), heads, q_cat)

    def update(carry, s, start):
        m, l, acc = carry
        m_new = _each(lambda m_, s_: jnp.maximum(m_, jnp.max(s_, axis=0, keepdims=True)), m, s)
        alpha = _each(lambda m_, n_: jnp.exp(m_ - n_), m, m_new)
        p = _each(lambda s_, n_: jnp.exp(s_ - n_), s, m_new)
        l = _each(lambda a_, l_, p_: a_ * l_ + jnp.sum(p_, axis=0, keepdims=True), alpha, l, p)
        pv = _each(lambda h, p_: lax.dot_general(
            kv_ref[pl.ds(start, tk), h * kv_width + QK_NOPE:(h + 1) * kv_width], p_.astype(BF16), TN,
            preferred_element_type=F32), heads, p)
        acc = _each(lambda a_, acc_, pv_: a_ * acc_ + pv_, alpha, acc, pv)
        return m_new, l, acc

    def body(kj, carry):
        start = pl.multiple_of(kj * tk, tk)
        return update(carry, scores(start), start)

    init = ([jnp.full((1, tq), -1e30, F32) for _ in heads], [jnp.zeros((1, tq), F32) for _ in heads],
            [jnp.zeros((V_HEAD, tq), F32) for _ in heads])
    carry = lax.fori_loop(0, qi * (tq // tk), body, init)

    q_chunk = lax.broadcasted_iota(jnp.int32, (tk, tq), 1) // CHUNK
    for sub in range(tq // tk):
        start = pl.multiple_of(qi * tq + sub * tk, tk)
        k_chunk = (lax.broadcasted_iota(jnp.int32, (tk, tq), 0) + sub * tk) // CHUNK
        visible = k_chunk <= q_chunk
        s = _each(lambda s_: jnp.where(visible, s_, -1e30), scores(start))
        carry = update(carry, s, start)
    m, l, acc = carry
    for h in heads:
        o_ref[:, h * V_HEAD:(h + 1) * V_HEAD] = (acc[h] / l[h]).T.astype(o_ref.dtype)


def _mla_attention(q_up, kv_up, proj, cos_tab, sin_tab, *, batch, seq, tq=256, tk=256, n_heads=MLA_HEADS):
    nq = seq // tq
    rope_w = n_heads * QK_ROPE
    rope0 = MLA_HEADS * QK_NOPE // rope_w
    swap0 = (MLA_HEADS * QK_NOPE + MLA_HEADS * QK_ROPE) // rope_w
    q_rows = lambda width, col0: pl.BlockSpec((tq, width), lambda b, g, i: (b * nq + i, col0 + g))
    q_tab = pl.BlockSpec((tq, LANES), lambda b, g, i: (b * nq + i, 0))
    k_tab = pl.BlockSpec((seq, LANES), lambda b, g, i: (b, 0))
    k_cols = lambda col: pl.BlockSpec((seq, LANES), lambda b, g, i: (b, col // LANES))
    return pl.pallas_call(
        functools.partial(_mla_attn_kernel, tq=tq, tk=tk, n_heads=n_heads),
        grid=(batch, MLA_HEADS // n_heads, nq),
        in_specs=[
            q_rows(n_heads * QK_NOPE, 0), q_rows(rope_w, rope0), q_rows(rope_w, swap0), q_tab, q_tab,
            pl.BlockSpec((seq, n_heads * (QK_NOPE + V_HEAD)), lambda b, g, i: (b, g)),
            k_cols(COL_KROPE), k_cols(COL_KROPE_SWAP), k_tab, k_tab,
        ],
        out_specs=pl.BlockSpec((tq, n_heads * V_HEAD), lambda b, g, i: (b * nq + i, g)),
        out_shape=jax.ShapeDtypeStruct((batch * seq, MLA_HEADS * V_HEAD), BF16),
        scratch_shapes=[pltpu.VMEM((seq, LANES), BF16)],
        compiler_params=_params("parallel", "parallel", "arbitrary"),
        name="mla_attention",
    )(q_up, q_up, q_up, cos_tab, sin_tab, kv_up, proj, proj, cos_tab, sin_tab)


def _head_sum(x, e_ref, et_ref):
    s = _dot(x, e_ref[...])
    return _dot(s, et_ref[...])


def _shift(y, prev_row, mu, first):
    rows = lax.broadcasted_iota(jnp.int32, y.shape, 0)
    prev0 = jnp.where(first, 0.0, prev_row)
    y_prev = jnp.where(rows == 0, prev0, pltpu.roll(y, 1, 0))
    return y + (y_prev - y) * mu


def _rwkv_prep_kernel(r_ref, k_ref, v_ref, s_ref, rp_ref, kp_ref, vp_ref, sp_ref,
                      mu_rkv_ref, mu_s_ref, w0_ref, ww2_ref, a0_ref, wa2_ref, wg2_ref,
                      kk_ref, ka_ref, e_ref, et_ref,
                      ro_ref, lw_ref, ko_ref, vo_ref, ao_ref, bo_ref, go_ref, *, tiles_per_seq):
    first = (pl.program_id(0) % tiles_per_seq) == 0
    def shifted(ref, pref, mu):
        return _shift(ref[...].astype(F32), pref[SUBLANES - 1:SUBLANES, :].astype(F32), mu, first)

    r = shifted(r_ref, rp_ref, mu_rkv_ref[0:1, :])
    k = shifted(k_ref, kp_ref, mu_rkv_ref[1:2, :])
    v = shifted(v_ref, vp_ref, mu_rkv_ref[2:3, :])
    small = shifted(s_ref, sp_ref, mu_s_ref[...])
    dw = small[:, 0:DECAY_LORA]
    da = small[:, DECAY_LORA:DECAY_LORA + A_LORA]
    dg = small[:, DECAY_LORA + A_LORA:]

    u = w0_ref[...] + _dot(jnp.tanh(dw), ww2_ref[...])
    lw_ref[...] = -DECAY_SCALE * _sigmoid(u)
    a = _sigmoid(a0_ref[...] + _dot(da, wa2_ref[...]))
    go_ref[...] = _dot(_sigmoid(dg), wg2_ref[...]).astype(go_ref.dtype)

    kk = k * kk_ref[...]
    kk = kk * lax.rsqrt(jnp.maximum(_head_sum(kk * kk, e_ref, et_ref), 1e-24))
    ro_ref[...] = r.astype(ro_ref.dtype)
    ko_ref[...] = (k * (1.0 + (a - 1.0) * ka_ref[...])).astype(ko_ref.dtype)
    vo_ref[...] = v.astype(vo_ref.dtype)
    ao_ref[...] = (-kk).astype(ao_ref.dtype)
    bo_ref[...] = (kk * a).astype(bo_ref.dtype)


def _rwkv_prep(proj, mu_rkv, mu_s, w0, ww2, a0, wa2, wg2, k_k, k_a, e, et, *, seq, tm=512):
    t = proj.shape[0]
    c = RWKV_WIDTH
    small_w = SMALL_LORA

    def cur(width, col):
        return pl.BlockSpec((tm, width), lambda i: (i, col // width))

    def prev(width, col):
        return pl.BlockSpec((SUBLANES, width),
                            lambda i: (jnp.maximum(i * (tm // SUBLANES) - 1, 0), col // width))

    def const(shape):
        return pl.BlockSpec(shape, lambda i: (0, 0))

    out = jax.ShapeDtypeStruct((t, c), BF16)
    return pl.pallas_call(
        functools.partial(_rwkv_prep_kernel, tiles_per_seq=seq // tm),
        grid=(t // tm,),
        in_specs=[cur(c, COL_R), cur(c, COL_K), cur(c, COL_V), cur(small_w, COL_SMALL),
                  prev(c, COL_R), prev(c, COL_K), prev(c, COL_V), prev(small_w, COL_SMALL),
                  const((3, c)), const((1, small_w)), const((1, c)), const((DECAY_LORA, c)),
                  const((1, c)), const((A_LORA, c)), const((GATE_LORA, c)),
                  const((1, c)), const((1, c)), const((c, LANES)), const((LANES, c))],
        out_specs=[pl.BlockSpec((tm, c), lambda i: (i, 0))] * 7,
        out_shape=[out, jax.ShapeDtypeStruct((t, c), F32), out, out, out, out, out],
        compiler_params=_params("parallel"),
        name="rwkv_prep",
    )(proj, proj, proj, proj, proj, proj, proj, proj,
      mu_rkv, mu_s, w0, ww2, a0, wa2, wg2, k_k, k_a, e, et)


def _block_diag(x):
    lane = lax.broadcasted_iota(jnp.int32, x.shape, 1)
    zero = jnp.zeros_like(x)
    return jnp.concatenate([jnp.where(lane < RWKV_HEAD, x, zero),
                            jnp.where(lane >= RWKV_HEAD, x, zero)], axis=0)


def _unit_lower_inverse(a, eye, level_masks):
    c = CHUNK
    base = _each(lambda x: x * level_masks[0], a)
    x2 = _each(lambda x: _dot(x, _block_diag(x)), base)
    t = _each(lambda x: eye + x, base)
    tx = _each(lambda t_, x_: _dot(jnp.concatenate([t_, x_], axis=0), _block_diag(x_)), t, x2)
    t = _each(lambda t_, tx_: t_ + tx_[:c], t, tx)
    t = _each(lambda t_, tx_: t_ + _dot(t_, _block_diag(tx_[c:])), t, tx)
    for mask in level_masks[1:]:
        ta = _each(lambda t_, a_: _dot(t_, _block_diag(a_ * mask)), t, a)
        t = _each(lambda t_, ta_: t_ + _dot(ta_, _block_diag(t_)), t, ta)
    return t


def _chunk_prepare(r, lw, k, v, a, b, consts):
    ltri, strict, incl, eye, _, level_masks = consts
    c = CHUNK
    rows = lambda x, y: jnp.concatenate([x, y], axis=0)

    def cumulative(x):
        both = jnp.dot(ltri, jnp.concatenate(_split2(x), axis=1), preferred_element_type=F32)
        return both[:, :x.shape[1]] + both[:, x.shape[1]:]

    cl = _each(cumulative, lw)
    cl_end = _each(lambda x: x[c - 1:c, :], cl)
    p_inv = _each(lambda x: jnp.exp(-x), cl)
    a_t = _each(lambda a_, cl_, lw_: a_ * jnp.exp(cl_ - lw_), a, cl, lw)
    r_t = _each(lambda r_, cl_: r_ * jnp.exp(cl_), r, cl)
    b_t = _each(jnp.multiply, b, p_inv)
    k_t = _each(jnp.multiply, k, p_inv)

    ar = _each(rows, a_t, r_t)
    g = _each(lambda x, b_, k_: _dot(x, rows(_block_diag(b_), _block_diag(k_)), NT), ar, b_t, k_t)
    width = b_t[0].shape[1]
    a_ab = _each(lambda g_: jnp.where(strict, g_[:c, :width], 0.0), g)
    a_rb = _each(lambda g_: jnp.where(incl, g_[c:, :width], 0.0), g)
    a_k = _each(lambda g_: rows(jnp.where(strict, g_[:c, width:], 0.0), jnp.where(incl, g_[c:, width:], 0.0)), g)

    t_inv = _unit_lower_inverse(a_ab, eye, level_masks)

    av = _each(lambda x, v_: _dot(x, _block_diag(v_)), a_k, v)
    tw = _each(lambda t_, a_, av_: _dot(t_, jnp.concatenate([_block_diag(a_), _block_diag(av_[:c])], axis=1)),
               t_inv, a_t, av)
    lhs = _each(lambda tw_, r_: rows(tw_[:, :LANES], r_).astype(BF16), tw, r_t)
    w_eff = _each(lambda tw_: tw_[:, LANES:], tw)
    o_local = _each(lambda av_: av_[c:], av)
    p_tail = _each(lambda e_, cl_: jnp.exp(e_ - cl_), cl_end, cl)
    bk_tail = _each(lambda b_, k_, p_: rows(b_ * p_, k_ * p_).astype(BF16), b, k, p_tail)
    p_end = _each(jnp.exp, cl_end)
    return lhs, w_eff, a_rb, o_local, v, bk_tail, p_end


def _chunk_apply(prepared, state, diag_blocks):
    c = CHUNK
    lhs, w_eff, a_rb, o_local, v, bk_tail, p_end = prepared
    uo = _each(lambda lhs_, s_: _dot(lhs_, s_, NT), lhs, state)
    u = _each(lambda uo_, w_: uo_[:c] + w_, uo, w_eff)
    o = _each(lambda uo_, arb_, u_, ol_: uo_[c:] + _dot(arb_, _block_diag(u_)) + ol_, uo, a_rb, u, o_local)
    upd = _each(lambda u_, v_, bk_: _dot(jnp.concatenate([u_, v_], axis=0), bk_, TN), u, v, bk_tail)
    new_state = _each(lambda s_, p_, upd_: s_ * p_ + jnp.where(diag_blocks, upd_, 0.0), state, p_end, upd)
    return o, new_state


def _rwkv_kernel(r_ref, lw_ref, k_ref, v_ref, a_ref, b_ref, o_ref, s_ref, *, n_chunks, n_pairs, unroll):
    @pl.when(pl.program_id(2) == 0)
    def _():
        s_ref[...] = jnp.zeros_like(s_ref)

    c = CHUNK
    row = lax.broadcasted_iota(jnp.int32, (c, LANES), 0)
    col = lax.broadcasted_iota(jnp.int32, (c, LANES), 1) % RWKV_HEAD
    ltri = jnp.where(lax.broadcasted_iota(jnp.int32, (c, c), 1) <= lax.broadcasted_iota(jnp.int32, (c, c), 0),
                     1.0, 0.0).astype(BF16)
    rr = lax.broadcasted_iota(jnp.int32, (LANES, LANES), 0) // RWKV_HEAD
    cc = lax.broadcasted_iota(jnp.int32, (LANES, LANES), 1) // RWKV_HEAD
    def same(block):
        return (row // block) == (col // block)
    level_masks = [jnp.where(same(8), 1.0, 0.0)]
    for block in (8, 16, 32):
        level_masks.append(jnp.where(same(2 * block), 1.0, 0.0) - jnp.where(same(block), 1.0, 0.0))
    eye = jnp.where(col == row, 1.0, 0.0)
    consts = (ltri, col < row, col <= row, eye, rr == cc, level_masks)

    def chunk_group(gi, carry):
        r0 = pl.multiple_of(gi * (unroll * c), unroll * c)
        tiles = [(pl.ds(r0 + u * c, c), slice(p * LANES, (p + 1) * LANES))
                 for u in range(unroll) for p in range(n_pairs)]
        load = lambda ref: [ref[sl].astype(F32) for sl in tiles]
        prepared = _chunk_prepare(load(r_ref), load(lw_ref), load(k_ref), load(v_ref), load(a_ref), load(b_ref),
                                  consts)
        state = [s_ref[p] for p in range(n_pairs)]
        for u in range(unroll):
            part = slice(u * n_pairs, (u + 1) * n_pairs)
            o, state = _chunk_apply([x[part] for x in prepared], state, consts[4])
            for p in range(n_pairs):
                o_ref[tiles[u * n_pairs + p]] = o[p]
        for p in range(n_pairs):
            s_ref[p] = state[p]
        return carry

    lax.fori_loop(0, n_chunks // unroll, chunk_group, 0)


def _rwkv_recurrence(r, lw, k, v, a, b, *, batch, seq, tm=512, n_pairs=8, unroll=4):
    t, c = r.shape
    width = n_pairs * LANES
    tm = min(tm, seq)
    nt = seq // tm
    assert (tm // CHUNK) % unroll == 0
    spec = pl.BlockSpec((tm, width), lambda bi, g, i: (bi * nt + i, g))
    return pl.pallas_call(
        functools.partial(_rwkv_kernel, n_chunks=tm // CHUNK, n_pairs=n_pairs, unroll=unroll),
        grid=(batch, c // width, nt),
        in_specs=[spec] * 6,
        out_specs=spec,
        out_shape=jax.ShapeDtypeStruct((t, c), F32),
        scratch_shapes=[pltpu.VMEM((n_pairs, LANES, LANES), F32)],
        compiler_params=_params("parallel", "parallel", "arbitrary"),
        name="rwkv_recurrence",
    )(r, lw, k, v, a, b)


def _mix_tail_kernel(x_ref, ga_ref, gb_ref, bg_ref, oa_ref, woa_ref,
                     o_ref, r_ref, k_ref, v_ref, g_ref, rk_ref, lw_ref, lb_ref, e_ref, et_ref,
                     wob_ref, wo_ref, gpost_ref, out_ref):
    inv_n = 1.0 / RWKV_HEAD
    o = o_ref[...]
    mean = _head_sum(o, e_ref, et_ref) * inv_n
    cen = o - mean
    var = _head_sum(cen * cen, e_ref, et_ref) * inv_n
    on = cen * lax.rsqrt(var + LNX_EPS) * lw_ref[...] + lb_ref[...]
    v = v_ref[...].astype(F32)
    rk = r_ref[...].astype(F32) * k_ref[...].astype(F32) * rk_ref[...]
    bonus = _head_sum(rk, e_ref, et_ref) * v
    ob = ((on + bonus) * g_ref[...].astype(F32)).astype(BF16)

    y_a = jnp.dot(oa_ref[...], woa_ref[...], preferred_element_type=F32)
    y_b = jnp.dot(ob, wob_ref[...], preferred_element_type=F32)
    gate_a = _sigmoid(ga_ref[...].astype(F32) + bg_ref[0:1, :])
    gate_b = _sigmoid(gb_ref[...].astype(F32) + bg_ref[1:2, :])
    merged = (gate_a * y_a + gate_b * y_b).astype(BF16)
    z = jnp.dot(merged, wo_ref[...], preferred_element_type=F32)
    out_ref[...] = x_ref[...] + _rms(z, gpost_ref[...])


def _mix_tail(x, proj, b_gate, o_a, w_oa, o_raw, r, k, v, g, r_k, lnx_w, lnx_b, e, et, w_ob, w_o, g_post, *, tm=256):
    t, d = x.shape
    c = RWKV_WIDTH
    row = lambda width, col=0: pl.BlockSpec((tm, width), lambda i: (i, col // width))
    return pl.pallas_call(
        _mix_tail_kernel,
        grid=(t // tm,),
        in_specs=[row(d), row(d, COL_GATE_A), row(d, COL_GATE_B), _resident((2, d)),
                  row(c), _resident((c, d)),
                  row(c), row(c), row(c), row(c), row(c),
                  _resident((1, c)), _resident((1, c)), _resident((1, c)),
                  _resident((c, LANES)), _resident((LANES, c)),
                  _resident((c, d)), _resident((d, d)), _resident((1, d))],
        out_specs=row(d),
        out_shape=jax.ShapeDtypeStruct((t, d), F32),
        compiler_params=_params("parallel"),
        name="mix_tail",
    )(x, proj, proj, b_gate, o_a, w_oa, o_raw, r, k, v, g, r_k, lnx_w, lnx_b, e, et, w_ob, w_o, g_post)


def _xattn_kernel(x_ref, gpre_ref, wq_ref, kv_ref, wo_ref, gpost_ref, out_ref):
    x = x_ref[...]
    h = _rms(x, gpre_ref[...]).astype(BF16)
    q = jnp.dot(h, wq_ref[...], preferred_element_type=F32).astype(BF16)
    width = MEM_HEADS * MEM_HEAD
    heads = []
    for hd in range(MEM_HEADS):
        lo = hd * MEM_HEAD
        k = kv_ref[:, lo:lo + MEM_HEAD]
        v = kv_ref[:, width + lo:width + lo + MEM_HEAD]
        s = lax.dot_general(q[:, lo:lo + MEM_HEAD], k, NT, preferred_element_type=F32)
        p = jnp.exp(s - jnp.max(s, axis=-1, keepdims=True))
        p = p / jnp.sum(p, axis=-1, keepdims=True)
        heads.append(jnp.dot(p.astype(BF16), v, preferred_element_type=F32).astype(BF16))
    o = jnp.concatenate(heads, axis=1)
    z = jnp.dot(o, wo_ref[...], preferred_element_type=F32)
    out_ref[...] = x + _rms(z, gpost_ref[...])


def _xattn(x, g_pre, w_cq, kv_mem, w_co, g_post, *, batch, seq, n_mem, tq=512):
    t, d = x.shape
    tq = min(tq, seq)
    nq = seq // tq
    width = MEM_HEADS * MEM_HEAD
    return pl.pallas_call(
        _xattn_kernel,
        grid=(batch, nq),
        in_specs=[pl.BlockSpec((tq, d), lambda b, i: (b * nq + i, 0)),
                  _resident((1, d)), _resident((d, width)),
                  pl.BlockSpec((n_mem, 2 * width), lambda b, i: (b, 0)),
                  _resident((width, d)), _resident((1, d))],
        out_specs=pl.BlockSpec((tq, d), lambda b, i: (b * nq + i, 0)),
        out_shape=jax.ShapeDtypeStruct((t, d), F32),
        compiler_params=_params("parallel", "parallel"),
        name="xattn",
    )(x, g_pre, w_cq, kv_mem, w_co, g_post)


def _rope_tables(positions):
    half = QK_ROPE // 2
    inv = ROPE_THETA ** (-jnp.arange(half, dtype=F32) / half)
    reps = LANES // half
    sign = jnp.tile(jnp.concatenate([-jnp.ones(half, F32), jnp.ones(half, F32)]), reps // 2)
    ang = positions.astype(F32).reshape(-1, 1) * jnp.tile(inv, reps)
    return jnp.cos(ang), jnp.sin(ang) * sign


def _layer(x, mem2d, positions, p, *, batch, seq, n_mem):
    row = lambda a: a.reshape(1, -1).astype(F32)
    c = RWKV_WIDTH
    d = D_MODEL

    def ffn(x, pre, gate, up, down, post):
        return _ffn(x, row(pre), gate.astype(BF16), up.astype(BF16), down.astype(BF16), row(post))

    x = ffn(x, p['n_ffn1_pre'], p['w_ffn1_gate'], p['w_ffn1_up'], p['w_ffn1_down'], p['n_ffn1_post'])

    w_in = p['w_in']
    o_rwkv = MLA_COLS
    o_gate = MLA_COLS + RWKV_COLS
    half = QK_ROPE // 2
    w_kr = w_in[:, Q_LORA + KV_LORA:MLA_COLS]
    w_kr_swap = jnp.concatenate([w_kr[:, half:], w_kr[:, :half]], axis=1)
    w_in_r = jnp.concatenate([
        w_in[:, o_gate:],
        w_in[:, o_rwkv:o_rwkv + 3 * c],
        w_in[:, :Q_LORA + KV_LORA],
        w_in[:, o_rwkv + 3 * c:o_gate],
        w_kr, w_kr, w_kr_swap, w_kr_swap,
    ], axis=1).astype(BF16)
    scale = (QK_NOPE + QK_ROPE) ** -0.5
    w_uq = p['w_uq'].reshape(Q_LORA, MLA_HEADS, QK_NOPE + QK_ROPE) * scale
    w_qr = w_uq[:, :, QK_NOPE:]
    w_qr_swap = jnp.concatenate([w_qr[:, :, half:], w_qr[:, :, :half]], axis=2)
    w_uq_r = jnp.concatenate([w_uq[:, :, :QK_NOPE].reshape(Q_LORA, -1), w_qr.reshape(Q_LORA, -1),
                              w_qr_swap.reshape(Q_LORA, -1)], axis=1).astype(BF16)
    proj, q_up, kv_up = _in_proj(x, row(p['n_mix_pre']), w_in_r, row(p['n_q_lat']), w_uq_r,
                                 row(p['n_kv_lat']), p['w_ukv'].astype(BF16))

    cos_tab, sin_tab = _rope_tables(positions)
    o_a = _mla_attention(q_up, kv_up, proj, cos_tab, sin_tab, batch=batch, seq=seq)

    head_of = jnp.arange(c) // RWKV_HEAD
    e = (head_of[:, None] == jnp.arange(LANES)[None, :]).astype(BF16)
    mu = p['mu_shift']
    r, lw, k, v, a_in, b_in, g = _rwkv_prep(
        proj, mu[:3 * c].reshape(3, c), row(mu[3 * c:]), row(p['w0']), p['w_w2'].astype(BF16),
        row(p['a0']), p['w_a2'].astype(BF16), p['w_g2'].astype(BF16), row(p['k_k']), row(p['k_a']),
        e, e.T, seq=seq)
    o_raw = _rwkv_recurrence(r, lw, k, v, a_in, b_in, batch=batch, seq=seq)

    x = _mix_tail(x, proj, p['b_gate'].reshape(2, d), o_a, p['w_oa'].astype(BF16), o_raw, r, k, v, g,
                  row(p['r_k']), row(p['lnx_w']), row(p['lnx_b']), e, e.T,
                  p['w_ob'].astype(BF16), p['w_o'].astype(BF16), row(p['n_mix_post']))

    width = MEM_HEADS * MEM_HEAD
    w_ckv = p['w_ckv'].reshape(d, MEM_HEADS, 2, MEM_HEAD).transpose(0, 2, 1, 3).reshape(d, 2 * width)
    kv_mem = _norm_matmul(mem2d, row(p['n_mem']), w_ckv.astype(BF16), tm=512, tn=1024)
    w_cq = (p['w_cq'] * MEM_HEAD ** -0.5).astype(BF16)
    x = _xattn(x, row(p['n_x_pre']), w_cq, kv_mem, p['w_co'].astype(BF16), row(p['n_x_post']),
               batch=batch, seq=seq, n_mem=n_mem)

    return ffn(x, p['n_ffn2_pre'], p['w_ffn2_gate'], p['w_ffn2_up'], p['w_ffn2_down'], p['n_ffn2_post'])


_PARAM_NAMES = (
    'n_ffn1_pre', 'n_ffn1_post', 'w_ffn1_gate', 'w_ffn1_up', 'w_ffn1_down',
    'n_mix_pre', 'n_mix_post', 'w_in', 'b_gate',
    'n_q_lat', 'w_uq', 'n_kv_lat', 'w_ukv', 'w_oa',
    'mu_shift', 'w0', 'w_w2', 'a0', 'w_a2', 'w_g2', 'k_k', 'k_a', 'r_k', 'lnx_w', 'lnx_b', 'w_ob',
    'w_o',
    'n_x_pre', 'n_x_post', 'n_mem', 'w_cq', 'w_ckv', 'w_co',
    'n_ffn2_pre', 'n_ffn2_post', 'w_ffn2_gate', 'w_ffn2_up', 'w_ffn2_down')


def kernel(x, mem, positions, n_ffn1_pre, n_ffn1_post, w_ffn1_gate, w_ffn1_up, w_ffn1_down, n_mix_pre, n_mix_post, w_in, b_gate, n_q_lat, w_uq, n_kv_lat, w_ukv, w_oa, mu_shift, w0, w_w2, a0, w_a2, w_g2, k_k, k_a, r_k, lnx_w, lnx_b, w_ob, w_o, n_x_pre, n_x_post, n_mem, w_cq, w_ckv, w_co, n_ffn2_pre, n_ffn2_post, w_ffn2_gate, w_ffn2_up, w_ffn2_down):
    stacked = (n_ffn1_pre, n_ffn1_post, w_ffn1_gate, w_ffn1_up, w_ffn1_down, n_mix_pre, n_mix_post, w_in, b_gate,
               n_q_lat, w_uq, n_kv_lat, w_ukv, w_oa, mu_shift, w0, w_w2, a0, w_a2, w_g2, k_k, k_a, r_k, lnx_w,
               lnx_b, w_ob, w_o, n_x_pre, n_x_post, n_mem, w_cq, w_ckv, w_co, n_ffn2_pre, n_ffn2_post,
               w_ffn2_gate, w_ffn2_up, w_ffn2_down)
    batch, seq, d = x.shape
    n_mem_tokens = mem.shape[1]
    x2d = x.reshape(batch * seq, d)
    mem2d = mem.reshape(batch * n_mem_tokens, d)
    for layer in range(n_ffn1_pre.shape[0]):
        p = {name: arr[layer] for name, arr in zip(_PARAM_NAMES, stacked)}
        x2d = _layer(x2d, mem2d, positions, p, batch=batch, seq=seq, n_mem=n_mem_tokens)
    return x2d.reshape(batch, seq, d)
```

```python
import functools

import jax
import jax.numpy as jnp
from jax import lax
from jax.experimental import pallas as pl
from jax.experimental.pallas import tpu as pltpu

F32 = jnp.float32
BF16 = jnp.bfloat16

D_MODEL = 2048
D_FF = 5504
CHUNK = 64
EPS = 1e-6

MLA_HEADS = 8
QK_NOPE = 128
QK_ROPE = 64
V_HEAD = 128
Q_LORA = 512
KV_LORA = 256
ROPE_THETA = 10000.0

RWKV_HEADS = 16
RWKV_HEAD = 64
RWKV_WIDTH = RWKV_HEADS * RWKV_HEAD
DECAY_LORA = 64
A_LORA = 64
GATE_LORA = 128
LNX_EPS = 64e-5
DECAY_SCALE = 0.6065306597126334

MEM_HEADS = 4
MEM_HEAD = 256

MLA_COLS = Q_LORA + KV_LORA + QK_ROPE
RWKV_COLS = 3 * RWKV_WIDTH + DECAY_LORA + A_LORA + GATE_LORA

LANES = 128
SUBLANES = 8
VMEM_LIMIT_BYTES = 56 * 1024 * 1024

SMALL_LORA = DECAY_LORA + A_LORA + GATE_LORA
COL_GATE_A = 0
COL_GATE_B = COL_GATE_A + D_MODEL
COL_R = COL_GATE_B + D_MODEL
COL_K = COL_R + RWKV_WIDTH
COL_V = COL_K + RWKV_WIDTH
COL_CQ = COL_V + RWKV_WIDTH
COL_CKV = COL_CQ + Q_LORA
COL_SMALL = COL_CKV + KV_LORA
COL_KROPE = COL_SMALL + SMALL_LORA
COL_KROPE_SWAP = COL_KROPE + 2 * QK_ROPE
IN_COLS = COL_KROPE_SWAP + 2 * QK_ROPE

NN = (((1,), (0,)), ((), ()))
NT = (((1,), (1,)), ((), ()))
TN = (((0,), (0,)), ((), ()))


def _dot(a, b, dims=NN):
    return lax.dot_general(a.astype(BF16), b.astype(BF16), dims, preferred_element_type=F32)


def _each(f, *cols):
    return [f(*xs) for xs in zip(*cols)]


def _split2(x):
    hi = x.astype(BF16)
    lo = (x - hi.astype(F32)).astype(BF16)
    return hi, lo


def _rms(xf, g, eps=EPS):
    return xf * lax.rsqrt(jnp.mean(xf * xf, axis=-1, keepdims=True) + eps) * g


def _sigmoid(x):
    return 1.0 / (1.0 + jnp.exp(-x))


def _params(*sem):
    return pltpu.CompilerParams(dimension_semantics=sem, vmem_limit_bytes=VMEM_LIMIT_BYTES)


def _resident(shape):
    return pl.BlockSpec(shape, lambda *_: (0,) * len(shape), pipeline_mode=pl.Buffered(1))


def _norm_matmul_kernel(x_ref, g_ref, w_ref, o_ref, h_ref):
    @pl.when(pl.program_id(1) == 0)
    def _():
        h_ref[...] = _rms(x_ref[...], g_ref[...]).astype(BF16)

    o_ref[...] = jnp.dot(h_ref[...], w_ref[...], preferred_element_type=F32).astype(o_ref.dtype)


def _norm_matmul(x, g, w, *, tm, tn):
    m, k_dim = x.shape
    n = w.shape[1]
    tm = min(tm, m)
    return pl.pallas_call(
        _norm_matmul_kernel,
        grid=(m // tm, n // tn),
        in_specs=[
            pl.BlockSpec((tm, k_dim), lambda i, j: (i, 0)),
            pl.BlockSpec((1, k_dim), lambda i, j: (0, 0)),
            pl.BlockSpec((k_dim, tn), lambda i, j: (0, j)),
        ],
        out_specs=pl.BlockSpec((tm, tn), lambda i, j: (i, j)),
        out_shape=jax.ShapeDtypeStruct((m, n), BF16),
        scratch_shapes=[pltpu.VMEM((tm, k_dim), BF16)],
        compiler_params=_params("parallel", "arbitrary"),
        name="norm_matmul",
    )(x, g, w)


IN_TN = 768
CQ_TILE, CQ_OFF = divmod(COL_CQ, IN_TN)
CKV_TILE, CKV_OFF = divmod(COL_CKV, IN_TN)
assert CQ_OFF + Q_LORA <= IN_TN and CKV_OFF + KV_LORA <= IN_TN and IN_COLS % IN_TN == 0
assert CQ_TILE > 0 and CKV_TILE > 0


def _in_proj_kernel(x_ref, g_ref, w_ref, gq_ref, wq_ref, gkv_ref, wkv_ref, o_ref, q_ref, kv_ref, h_ref):
    j = pl.program_id(1)

    def project(h):
        y = jnp.dot(h, w_ref[...], preferred_element_type=F32)
        o_ref[...] = y.astype(o_ref.dtype)
        return y

    def up_project(latent, gain_ref, weight_ref, out_ref):
        out_ref[...] = jnp.dot(_rms(latent, gain_ref[...]).astype(BF16), weight_ref[...],
                               preferred_element_type=F32).astype(out_ref.dtype)

    @pl.when(j == 0)
    def _():
        h = _rms(x_ref[...], g_ref[...]).astype(BF16)
        h_ref[...] = h
        project(h)

    @pl.when(j > 0)
    def _():
        y = project(h_ref[...])

        @pl.when(j == CQ_TILE)
        def _():
            up_project(y[:, CQ_OFF:CQ_OFF + Q_LORA], gq_ref, wq_ref, q_ref)

        @pl.when(j == CKV_TILE)
        def _():
            up_project(y[:, CKV_OFF:CKV_OFF + KV_LORA], gkv_ref, wkv_ref, kv_ref)


def _in_proj(x, g, w, g_q, w_uq, g_kv, w_ukv, *, tm=1024):
    m, d = x.shape
    tm = min(tm, m)
    nq, nkv = w_uq.shape[1], w_ukv.shape[1]
    return pl.pallas_call(
        _in_proj_kernel,
        grid=(m // tm, IN_COLS // IN_TN),
        in_specs=[
            pl.BlockSpec((tm, d), lambda i, j: (i, 0)),
            _resident((1, d)),
            pl.BlockSpec((d, IN_TN), lambda i, j: (0, j)),
            _resident((1, Q_LORA)), _resident((Q_LORA, nq)),
            _resident((1, KV_LORA)), _resident((KV_LORA, nkv)),
        ],
        out_specs=[pl.BlockSpec((tm, IN_TN), lambda i, j: (i, j)),
                   pl.BlockSpec((tm, nq), lambda i, j: (i, 0)),
                   pl.BlockSpec((tm, nkv), lambda i, j: (i, 0))],
        out_shape=[jax.ShapeDtypeStruct((m, IN_COLS), BF16),
                   jax.ShapeDtypeStruct((m, nq), BF16),
                   jax.ShapeDtypeStruct((m, nkv), BF16)],
        scratch_shapes=[pltpu.VMEM((tm, d), BF16)],
        compiler_params=_params("parallel", "arbitrary"),
        name="in_proj",
    )(x, g, w, g_q, w_uq, g_kv, w_ukv)


def _ffn_kernel(x_ref, gpre_ref, wg_ref, wu_ref, wd_ref, gpost_ref, o_ref, h_ref, acc_ref, *, tail):
    j = pl.program_id(1)
    last = pl.num_programs(1) - 1
    tf = wg_ref.shape[1]

    def contribution(h, valid):
        gate = jnp.dot(h, wg_ref[:, :valid], preferred_element_type=F32)
        up = jnp.dot(h, wu_ref[:, :valid], preferred_element_type=F32)
        act = (gate * _sigmoid(gate) * up).astype(BF16)
        return jnp.dot(act, wd_ref[:valid, :], preferred_element_type=F32)

    @pl.when(j == 0)
    def _():
        h = _rms(x_ref[...], gpre_ref[...]).astype(BF16)
        h_ref[...] = h
        acc_ref[...] = contribution(h, tf)

    @pl.when((j > 0) & (j < last))
    def _():
        acc_ref[...] += contribution(h_ref[...], tf)

    @pl.when(j == last)
    def _():
        acc = acc_ref[...] + contribution(h_ref[...], tail)
        o_ref[...] = x_ref[...] + 0.5 * _rms(acc, gpost_ref[...])


def _ffn(x, g_pre, wg, wu, wd, g_post, *, tm=512, tf=512):
    m, d = x.shape
    f = wg.shape[1]
    n_tiles = pl.cdiv(f, tf)
    tm = min(tm, m)
    tail = f - (n_tiles - 1) * tf
    assert tail % LANES == 0 and n_tiles >= 2
    return pl.pallas_call(
        functools.partial(_ffn_kernel, tail=tail),
        grid=(m // tm, n_tiles),
        in_specs=[
            pl.BlockSpec((tm, d), lambda i, j: (i, 0)),
            pl.BlockSpec((1, d), lambda i, j: (0, 0)),
            pl.BlockSpec((d, tf), lambda i, j: (0, j)),
            pl.BlockSpec((d, tf), lambda i, j: (0, j)),
            pl.BlockSpec((tf, d), lambda i, j: (j, 0)),
            pl.BlockSpec((1, d), lambda i, j: (0, 0)),
        ],
        out_specs=pl.BlockSpec((tm, d), lambda i, j: (i, 0)),
        out_shape=jax.ShapeDtypeStruct((m, d), F32),
        scratch_shapes=[pltpu.VMEM((tm, d), BF16), pltpu.VMEM((tm, d), F32)],
        compiler_params=_params("parallel", "arbitrary"),
        name="ffn",
    )(x, g_pre, wg, wu, wd, g_post)


def _mla_attn_kernel(qn_ref, qr_ref, qs_ref, cq_ref, sq_ref, kv_ref, kr_ref, ks_ref, ck_ref, sk_ref,
                     o_ref, krot_ref, *, tq, tk, n_heads):
    qi = pl.program_id(2)
    heads = list(range(n_heads))
    kv_width = QK_NOPE + V_HEAD

    @pl.when(qi == 0)
    def _():
        krot_ref[...] = (kr_ref[...].astype(F32) * ck_ref[...]
                         + ks_ref[...].astype(F32) * sk_ref[...]).astype(BF16)

    cq, sq = cq_ref[...], sq_ref[...]
    q_rot = [qr_ref[:, p * LANES:(p + 1) * LANES].astype(F32) * cq
             + qs_ref[:, p * LANES:(p + 1) * LANES].astype(F32) * sq for p in range(n_heads // 2)]
    lane = lax.broadcasted_iota(jnp.int32, (tq, LANES), 1)
    own_half = [lane < QK_ROPE, lane >= QK_ROPE]
    q_cat = [jnp.concatenate([qn_ref[:, h * QK_NOPE:(h + 1) * QK_NOPE],
                              jnp.where(own_half[h % 2], q_rot[h // 2], 0.0).astype(BF16)], axis=1)
             for h in heads]

    def scores(start):
        k_rot = krot_ref[pl.ds(start, tk), :]
        return _each(lambda h, q_: lax.dot_general(
            jnp.concatenate([kv_ref[pl.ds(start, tk), h * kv_width:h * kv_width + QK_NOPE], k_rot], axis=1), q_,
            NT, preferred_element_type=F32), heads, q_cat)

    def update(carry, s, start):
        m, l, acc = carry
        m_new = _each(lambda m_, s_: jnp.maximum(m_, jnp.max(s_, axis=0, keepdims=True)), m, s)
        alpha = _each(lambda m_, n_: jnp.exp(m_ - n_), m, m_new)
        p = _each(lambda s_, n_: jnp.exp(s_ - n_), s, m_new)
        l = _each(lambda a_, l_, p_: a_ * l_ + jnp.sum(p_, axis=0, keepdims=True), alpha, l, p)
        pv = _each(lambda h, p_: lax.dot_general(
            kv_ref[pl.ds(start, tk), h * kv_width + QK_NOPE:(h + 1) * kv_width], p_.astype(BF16), TN,
            preferred_element_type=F32), heads, p)
        acc = _each(lambda a_, acc_, pv_: a_ * acc_ + pv_, alpha, acc, pv)
        return m_new, l, acc

    def body(kj, carry):
        start = pl.multiple_of(kj * tk, tk)
        return update(carry, scores(start), start)

    init = ([jnp.full((1, tq), -1e30, F32) for _ in heads], [jnp.zeros((1, tq), F32) for _ in heads],
            [jnp.zeros((V_HEAD, tq), F32) for _ in heads])
    carry = lax.fori_loop(0, qi * (tq // tk), body, init)

    q_chunk = lax.broadcasted_iota(jnp.int32, (tk, tq), 1) // CHUNK
    for sub in range(tq // tk):
        start = pl.multiple_of(qi * tq + sub * tk, tk)
        k_chunk = (lax.broadcasted_iota(jnp.int32, (tk, tq), 0) + sub * tk) // CHUNK
        visible = k_chunk <= q_chunk
        s = _each(lambda s_: jnp.where(visible, s_, -1e30), scores(start))
        carry = update(carry, s, start)
    m, l, acc = carry
    for h in heads:
        o_ref[:, h * V_HEAD:(h + 1) * V_HEAD] = (acc[h] / l[h]).T.astype(o_ref.dtype)


def _mla_attention(q_up, kv_up, proj, cos_tab, sin_tab, *, batch, seq, tq=256, tk=256, n_heads=MLA_HEADS):
    nq = seq // tq
    rope_w = n_heads * QK_ROPE
    rope0 = MLA_HEADS * QK_NOPE // rope_w
    swap0 = (MLA_HEADS * QK_NOPE + MLA_HEADS * QK_ROPE) // rope_w
    q_rows = lambda width, col0: pl.BlockSpec((tq, width), lambda b, g, i: (b * nq + i, col0 + g))
    q_tab = pl.BlockSpec((tq, LANES), lambda b, g, i: (b * nq + i, 0))
    k_tab = pl.BlockSpec((seq, LANES), lambda b, g, i: (b, 0))
    k_cols = lambda col: pl.BlockSpec((seq, LANES), lambda b, g, i: (b, col // LANES))
    return pl.pallas_call(
        functools.partial(_mla_attn_kernel, tq=tq, tk=tk, n_heads=n_heads),
        grid=(batch, MLA_HEADS // n_heads, nq),
        in_specs=[
            q_rows(n_heads * QK_NOPE, 0), q_rows(rope_w, rope0), q_rows(rope_w, swap0), q_tab, q_tab,
            pl.BlockSpec((seq, n_heads * (QK_NOPE + V_HEAD)), lambda b, g, i: (b, g)),
            k_cols(COL_KROPE), k_cols(COL_KROPE_SWAP), k_tab, k_tab,
        ],
        out_specs=pl.BlockSpec((tq, n_heads * V_HEAD), lambda b, g, i: (b * nq + i, g)),
        out_shape=jax.ShapeDtypeStruct((batch * seq, MLA_HEADS * V_HEAD), BF16),
        scratch_shapes=[pltpu.VMEM((seq, LANES), BF16)],
        compiler_params=_params("parallel", "parallel", "arbitrary"),
        name="mla_attention",
    )(q_up, q_up, q_up, cos_tab, sin_tab, kv_up, proj, proj, cos_tab, sin_tab)


def _head_sum(x, e_ref, et_ref):
    s = _dot(x, e_ref[...])
    return _dot(s, et_ref[...])


def _shift(y, prev_row, mu, first):
    rows = lax.broadcasted_iota(jnp.int32, y.shape, 0)
    prev0 = jnp.where(first, 0.0, prev_row)
    y_prev = jnp.where(rows == 0, prev0, pltpu.roll(y, 1, 0))
    return y + (y_prev - y) * mu


def _rwkv_prep_kernel(r_ref, k_ref, v_ref, s_ref, rp_ref, kp_ref, vp_ref, sp_ref,
                      mu_rkv_ref, mu_s_ref, w0_ref, ww2_ref, a0_ref, wa2_ref, wg2_ref,
                      kk_ref, ka_ref, e_ref, et_ref,
                      ro_ref, lw_ref, ko_ref, vo_ref, ao_ref, bo_ref, go_ref, *, tiles_per_seq):
    first = (pl.program_id(0) % tiles_per_seq) == 0
    def shifted(ref, pref, mu):
        return _shift(ref[...].astype(F32), pref[SUBLANES - 1:SUBLANES, :].astype(F32), mu, first)

    r = shifted(r_ref, rp_ref, mu_rkv_ref[0:1, :])
    k = shifted(k_ref, kp_ref, mu_rkv_ref[1:2, :])
    v = shifted(v_ref, vp_ref, mu_rkv_ref[2:3, :])
    small = shifted(s_ref, sp_ref, mu_s_ref[...])
    dw = small[:, 0:DECAY_LORA]
    da = small[:, DECAY_LORA:DECAY_LORA + A_LORA]
    dg = small[:, DECAY_LORA + A_LORA:]

    u = w0_ref[...] + _dot(jnp.tanh(dw), ww2_ref[...])
    lw_ref[...] = -DECAY_SCALE * _sigmoid(u)
    a = _sigmoid(a0_ref[...] + _dot(da, wa2_ref[...]))
    go_ref[...] = _dot(_sigmoid(dg), wg2_ref[...]).astype(go_ref.dtype)

    kk = k * kk_ref[...]
    kk = kk * lax.rsqrt(jnp.maximum(_head_sum(kk * kk, e_ref, et_ref), 1e-24))
    ro_ref[...] = r.astype(ro_ref.dtype)
    ko_ref[...] = (k * (1.0 + (a - 1.0) * ka_ref[...])).astype(ko_ref.dtype)
    vo_ref[...] = v.astype(vo_ref.dtype)
    ao_ref[...] = (-kk).astype(ao_ref.dtype)
    bo_ref[...] = (kk * a).astype(bo_ref.dtype)


def _rwkv_prep(proj, mu_rkv, mu_s, w0, ww2, a0, wa2, wg2, k_k, k_a, e, et, *, seq, tm=512):
    t = proj.shape[0]
    c = RWKV_WIDTH
    small_w = SMALL_LORA
    tm = min(tm, seq)

    def cur(width, col):
        return pl.BlockSpec((tm, width), lambda i: (i, col // width))

    def prev(width, col):
        return pl.BlockSpec((SUBLANES, width),
                            lambda i: (jnp.maximum(i * (tm // SUBLANES) - 1, 0), col // width))

    def const(shape):
        return pl.BlockSpec(shape, lambda i: (0, 0))

    out = jax.ShapeDtypeStruct((t, c), BF16)
    return pl.pallas_call(
        functools.partial(_rwkv_prep_kernel, tiles_per_seq=seq // tm),
        grid=(t // tm,),
        in_specs=[cur(c, COL_R), cur(c, COL_K), cur(c, COL_V), cur(small_w, COL_SMALL),
                  prev(c, COL_R), prev(c, COL_K), prev(c, COL_V), prev(small_w, COL_SMALL),
                  const((3, c)), const((1, small_w)), const((1, c)), const((DECAY_LORA, c)),
                  const((1, c)), const((A_LORA, c)), const((GATE_LORA, c)),
                  const((1, c)), const((1, c)), const((c, LANES)), const((LANES, c))],
        out_specs=[pl.BlockSpec((tm, c), lambda i: (i, 0))] * 7,
        out_shape=[out, jax.ShapeDtypeStruct((t, c), F32), out, out, out, out, out],
        compiler_params=_params("parallel"),
        name="rwkv_prep",
    )(proj, proj, proj, proj, proj, proj, proj, proj,
      mu_rkv, mu_s, w0, ww2, a0, wa2, wg2, k_k, k_a, e, et)


def _block_diag(x):
    lane = lax.broadcasted_iota(jnp.int32, x.shape, 1)
    zero = jnp.zeros_like(x)
    return jnp.concatenate([jnp.where(lane < RWKV_HEAD, x, zero),
                            jnp.where(lane >= RWKV_HEAD, x, zero)], axis=0)


def _unit_lower_inverse(a, eye, level_masks):
    c = CHUNK
    base = _each(lambda x: x * level_masks[0], a)
    x2 = _each(lambda x: _dot(x, _block_diag(x)), base)
    t = _each(lambda x: eye + x, base)
    tx = _each(lambda t_, x_: _dot(jnp.concatenate([t_, x_], axis=0), _block_diag(x_)), t, x2)
    t = _each(lambda t_, tx_: t_ + tx_[:c], t, tx)
    t = _each(lambda t_, tx_: t_ + _dot(t_, _block_diag(tx_[c:])), t, tx)
    for mask in level_masks[1:]:
        ta = _each(lambda t_, a_: _dot(t_, _block_diag(a_ * mask)), t, a)
        t = _each(lambda t_, ta_: t_ + _dot(ta_, _block_diag(t_)), t, ta)
    return t


def _chunk_prepare(r, lw, k, v, a, b, consts):
    ltri, strict, incl, eye, _, level_masks = consts
    c = CHUNK
    rows = lambda x, y: jnp.concatenate([x, y], axis=0)

    def cumulative(x):
        both = jnp.dot(ltri, jnp.concatenate(_split2(x), axis=1), preferred_element_type=F32)
        return both[:, :x.shape[1]] + both[:, x.shape[1]:]

    cl = _each(cumulative, lw)
    cl_end = _each(lambda x: x[c - 1:c, :], cl)
    p_inv = _each(lambda x: jnp.exp(-x), cl)
    a_t = _each(lambda a_, cl_, lw_: a_ * jnp.exp(cl_ - lw_), a, cl, lw)
    r_t = _each(lambda r_, cl_: r_ * jnp.exp(cl_), r, cl)
    b_t = _each(jnp.multiply, b, p_inv)
    k_t = _each(jnp.multiply, k, p_inv)

    ar = _each(rows, a_t, r_t)
    g = _each(lambda x, b_, k_: _dot(x, rows(_block_diag(b_), _block_diag(k_)), NT), ar, b_t, k_t)
    width = b_t[0].shape[1]
    a_ab = _each(lambda g_: jnp.where(strict, g_[:c, :width], 0.0), g)
    a_rb = _each(lambda g_: jnp.where(incl, g_[c:, :width], 0.0), g)
    a_k = _each(lambda g_: rows(jnp.where(strict, g_[:c, width:], 0.0), jnp.where(incl, g_[c:, width:], 0.0)), g)

    t_inv = _unit_lower_inverse(a_ab, eye, level_masks)

    av = _each(lambda x, v_: _dot(x, _block_diag(v_)), a_k, v)
    tw = _each(lambda t_, a_, av_: _dot(t_, jnp.concatenate([_block_diag(a_), _block_diag(av_[:c])], axis=1)),
               t_inv, a_t, av)
    lhs = _each(lambda tw_, r_: rows(tw_[:, :LANES], r_).astype(BF16), tw, r_t)
    w_eff = _each(lambda tw_: tw_[:, LANES:], tw)
    o_local = _each(lambda av_: av_[c:], av)
    p_tail = _each(lambda e_, cl_: jnp.exp(e_ - cl_), cl_end, cl)
    bk_tail = _each(lambda b_, k_, p_: rows(b_ * p_, k_ * p_).astype(BF16), b, k, p_tail)
    p_end = _each(jnp.exp, cl_end)
    return lhs, w_eff, a_rb, o_local, v, bk_tail, p_end


def _chunk_apply(prepared, state, diag_blocks):
    c = CHUNK
    lhs, w_eff, a_rb, o_local, v, bk_tail, p_end = prepared
    uo = _each(lambda lhs_, s_: _dot(lhs_, s_, NT), lhs, state)
    u = _each(lambda uo_, w_: uo_[:c] + w_, uo, w_eff)
    o = _each(lambda uo_, arb_, u_, ol_: uo_[c:] + _dot(arb_, _block_diag(u_)) + ol_, uo, a_rb, u, o_local)
    upd = _each(lambda u_, v_, bk_: _dot(jnp.concatenate([u_, v_], axis=0), bk_, TN), u, v, bk_tail)
    new_state = _each(lambda s_, p_, upd_: s_ * p_ + jnp.where(diag_blocks, upd_, 0.0), state, p_end, upd)
    return o, new_state


def _rwkv_kernel(r_ref, lw_ref, k_ref, v_ref, a_ref, b_ref, o_ref, s_ref, *, n_chunks, n_pairs, unroll):
    @pl.when(pl.program_id(2) == 0)
    def _():
        s_ref[...] = jnp.zeros_like(s_ref)

    c = CHUNK
    row = lax.broadcasted_iota(jnp.int32, (c, LANES), 0)
    col = lax.broadcasted_iota(jnp.int32, (c, LANES), 1) % RWKV_HEAD
    ltri = jnp.where(lax.broadcasted_iota(jnp.int32, (c, c), 1) <= lax.broadcasted_iota(jnp.int32, (c, c), 0),
                     1.0, 0.0).astype(BF16)
    rr = lax.broadcasted_iota(jnp.int32, (LANES, LANES), 0) // RWKV_HEAD
    cc = lax.broadcasted_iota(jnp.int32, (LANES, LANES), 1) // RWKV_HEAD
    def same(block):
        return (row // block) == (col // block)
    level_masks = [jnp.where(same(8), 1.0, 0.0)]
    for block in (8, 16, 32):
        level_masks.append(jnp.where(same(2 * block), 1.0, 0.0) - jnp.where(same(block), 1.0, 0.0))
    eye = jnp.where(col == row, 1.0, 0.0)
    consts = (ltri, col < row, col <= row, eye, rr == cc, level_masks)

    def chunk_group(gi, carry):
        r0 = pl.multiple_of(gi * (unroll * c), unroll * c)
        tiles = [(pl.ds(r0 + u * c, c), slice(p * LANES, (p + 1) * LANES))
                 for u in range(unroll) for p in range(n_pairs)]
        load = lambda ref: [ref[sl].astype(F32) for sl in tiles]
        prepared = _chunk_prepare(load(r_ref), load(lw_ref), load(k_ref), load(v_ref), load(a_ref), load(b_ref),
                                  consts)
        state = [s_ref[p] for p in range(n_pairs)]
        for u in range(unroll):
            part = slice(u * n_pairs, (u + 1) * n_pairs)
            o, state = _chunk_apply([x[part] for x in prepared], state, consts[4])
            for p in range(n_pairs):
                o_ref[tiles[u * n_pairs + p]] = o[p]
        for p in range(n_pairs):
            s_ref[p] = state[p]
        return carry

    lax.fori_loop(0, n_chunks // unroll, chunk_group, 0)


def _rwkv_recurrence(r, lw, k, v, a, b, *, batch, seq, tm=512, n_pairs=8, unroll=4):
    t, c = r.shape
    width = n_pairs * LANES
    tm = min(tm, seq)
    nt = seq // tm
    assert (tm // CHUNK) % unroll == 0
    spec = pl.BlockSpec((tm, width), lambda bi, g, i: (bi * nt + i, g))
    return pl.pallas_call(
        functools.partial(_rwkv_kernel, n_chunks=tm // CHUNK, n_pairs=n_pairs, unroll=unroll),
        grid=(batch, c // width, nt),
        in_specs=[spec] * 6,
        out_specs=spec,
        out_shape=jax.ShapeDtypeStruct((t, c), F32),
        scratch_shapes=[pltpu.VMEM((n_pairs, LANES, LANES), F32)],
        compiler_params=_params("parallel", "parallel", "arbitrary"),
        name="rwkv_recurrence",
    )(r, lw, k, v, a, b)


def _mix_tail_kernel(x_ref, ga_ref, gb_ref, bg_ref, oa_ref, woa_ref,
                     o_ref, r_ref, k_ref, v_ref, g_ref, rk_ref, lw_ref, lb_ref, e_ref, et_ref,
                     wob_ref, wo_ref, gpost_ref, out_ref):
    inv_n = 1.0 / RWKV_HEAD
    o = o_ref[...]
    mean = _head_sum(o, e_ref, et_ref) * inv_n
    cen = o - mean
    var = _head_sum(cen * cen, e_ref, et_ref) * inv_n
    on = cen * lax.rsqrt(var + LNX_EPS) * lw_ref[...] + lb_ref[...]
    v = v_ref[...].astype(F32)
    rk = r_ref[...].astype(F32) * k_ref[...].astype(F32) * rk_ref[...]
    bonus = _head_sum(rk, e_ref, et_ref) * v
    ob = ((on + bonus) * g_ref[...].astype(F32)).astype(BF16)

    y_a = jnp.dot(oa_ref[...], woa_ref[...], preferred_element_type=F32)
    y_b = jnp.dot(ob, wob_ref[...], preferred_element_type=F32)
    gate_a = _sigmoid(ga_ref[...].astype(F32) + bg_ref[0:1, :])
    gate_b = _sigmoid(gb_ref[...].astype(F32) + bg_ref[1:2, :])
    merged = (gate_a * y_a + gate_b * y_b).astype(BF16)
    z = jnp.dot(merged, wo_ref[...], preferred_element_type=F32)
    out_ref[...] = x_ref[...] + _rms(z, gpost_ref[...])


def _mix_tail(x, proj, b_gate, o_a, w_oa, o_raw, r, k, v, g, r_k, lnx_w, lnx_b, e, et, w_ob, w_o, g_post, *, tm=256):
    t, d = x.shape
    c = RWKV_WIDTH
    row = lambda width, col=0: pl.BlockSpec((tm, width), lambda i: (i, col // width))
    return pl.pallas_call(
        _mix_tail_kernel,
        grid=(t // tm,),
        in_specs=[row(d), row(d, COL_GATE_A), row(d, COL_GATE_B), _resident((2, d)),
                  row(c), _resident((c, d)),
                  row(c), row(c), row(c), row(c), row(c),
                  _resident((1, c)), _resident((1, c)), _resident((1, c)),
                  _resident((c, LANES)), _resident((LANES, c)),
                  _resident((c, d)), _resident((d, d)), _resident((1, d))],
        out_specs=row(d),
        out_shape=jax.ShapeDtypeStruct((t, d), F32),
        compiler_params=_params("parallel"),
        name="mix_tail",
    )(x, proj, proj, b_gate, o_a, w_oa, o_raw, r, k, v, g, r_k, lnx_w, lnx_b, e, et, w_ob, w_o, g_post)


def _xattn_kernel(x_ref, gpre_ref, wq_ref, kv_ref, wo_ref, gpost_ref, out_ref):
    x = x_ref[...]
    h = _rms(x, gpre_ref[...]).astype(BF16)
    q = jnp.dot(h, wq_ref[...], preferred_element_type=F32).astype(BF16)
    width = MEM_HEADS * MEM_HEAD
    heads = []
    for hd in range(MEM_HEADS):
        lo = hd * MEM_HEAD
        k = kv_ref[:, lo:lo + MEM_HEAD]
        v = kv_ref[:, width + lo:width + lo + MEM_HEAD]
        s = lax.dot_general(q[:, lo:lo + MEM_HEAD], k, NT, preferred_element_type=F32)
        p = jnp.exp(s - jnp.max(s, axis=-1, keepdims=True))
        p = p / jnp.sum(p, axis=-1, keepdims=True)
        heads.append(jnp.dot(p.astype(BF16), v, preferred_element_type=F32).astype(BF16))
    o = jnp.concatenate(heads, axis=1)
    z = jnp.dot(o, wo_ref[...], preferred_element_type=F32)
    out_ref[...] = x + _rms(z, gpost_ref[...])


def _xattn(x, g_pre, w_cq, kv_mem, w_co, g_post, *, batch, seq, n_mem, tq=512):
    t, d = x.shape
    tq = min(tq, seq)
    nq = seq // tq
    width = MEM_HEADS * MEM_HEAD
    return pl.pallas_call(
        _xattn_kernel,
        grid=(batch, nq),
        in_specs=[pl.BlockSpec((tq, d), lambda b, i: (b * nq + i, 0)),
                  _resident((1, d)), _resident((d, width)),
                  pl.BlockSpec((n_mem, 2 * width), lambda b, i: (b, 0)),
                  _resident((width, d)), _resident((1, d))],
        out_specs=pl.BlockSpec((tq, d), lambda b, i: (b * nq + i, 0)),
        out_shape=jax.ShapeDtypeStruct((t, d), F32),
        compiler_params=_params("parallel", "parallel"),
        name="xattn",
    )(x, g_pre, w_cq, kv_mem, w_co, g_post)


def _rope_tables(positions):
    half = QK_ROPE // 2
    inv = ROPE_THETA ** (-jnp.arange(half, dtype=F32) / half)
    reps = LANES // half
    sign = jnp.tile(jnp.concatenate([-jnp.ones(half, F32), jnp.ones(half, F32)]), reps // 2)
    ang = positions.astype(F32).reshape(-1, 1) * jnp.tile(inv, reps)
    return jnp.cos(ang), jnp.sin(ang) * sign


def _layer(x, mem2d, positions, p, *, batch, seq, n_mem):
    row = lambda a: a.reshape(1, -1).astype(F32)
    c = RWKV_WIDTH
    d = D_MODEL

    def ffn(x, pre, gate, up, down, post):
        return _ffn(x, row(pre), gate.astype(BF16), up.astype(BF16), down.astype(BF16), row(post))

    x = ffn(x, p['n_ffn1_pre'], p['w_ffn1_gate'], p['w_ffn1_up'], p['w_ffn1_down'], p['n_ffn1_post'])

    w_in = p['w_in']
    o_rwkv = MLA_COLS
    o_gate = MLA_COLS + RWKV_COLS
    half = QK_ROPE // 2
    w_kr = w_in[:, Q_LORA + KV_LORA:MLA_COLS]
    w_kr_swap = jnp.concatenate([w_kr[:, half:], w_kr[:, :half]], axis=1)
    w_in_r = jnp.concatenate([
        w_in[:, o_gate:],
        w_in[:, o_rwkv:o_rwkv + 3 * c],
        w_in[:, :Q_LORA + KV_LORA],
        w_in[:, o_rwkv + 3 * c:o_gate],
        w_kr, w_kr, w_kr_swap, w_kr_swap,
    ], axis=1).astype(BF16)
    scale = (QK_NOPE + QK_ROPE) ** -0.5
    w_uq = p['w_uq'].reshape(Q_LORA, MLA_HEADS, QK_NOPE + QK_ROPE) * scale
    w_qr = w_uq[:, :, QK_NOPE:]
    w_qr_swap = jnp.concatenate([w_qr[:, :, half:], w_qr[:, :, :half]], axis=2)
    w_uq_r = jnp.concatenate([w_uq[:, :, :QK_NOPE].reshape(Q_LORA, -1), w_qr.reshape(Q_LORA, -1),
                              w_qr_swap.reshape(Q_LORA, -1)], axis=1).astype(BF16)
    proj, q_up, kv_up = _in_proj(x, row(p['n_mix_pre']), w_in_r, row(p['n_q_lat']), w_uq_r,
                                 row(p['n_kv_lat']), p['w_ukv'].astype(BF16))

    cos_tab, sin_tab = _rope_tables(positions)
    o_a = _mla_attention(q_up, kv_up, proj, cos_tab, sin_tab, batch=batch, seq=seq)

    head_of = jnp.arange(c) // RWKV_HEAD
    e = (head_of[:, None] == jnp.arange(LANES)[None, :]).astype(BF16)
    mu = p['mu_shift']
    r, lw, k, v, a_in, b_in, g = _rwkv_prep(
        proj, mu[:3 * c].reshape(3, c), row(mu[3 * c:]), row(p['w0']), p['w_w2'].astype(BF16),
        row(p['a0']), p['w_a2'].astype(BF16), p['w_g2'].astype(BF16), row(p['k_k']), row(p['k_a']),
        e, e.T, seq=seq)
    o_raw = _rwkv_recurrence(r, lw, k, v, a_in, b_in, batch=batch, seq=seq)

    x = _mix_tail(x, proj, p['b_gate'].reshape(2, d), o_a, p['w_oa'].astype(BF16), o_raw, r, k, v, g,
                  row(p['r_k']), row(p['lnx_w']), row(p['lnx_b']), e, e.T,
                  p['w_ob'].astype(BF16), p['w_o'].astype(BF16), row(p['n_mix_post']))

    width = MEM_HEADS * MEM_HEAD
    w_ckv = p['w_ckv'].reshape(d, MEM_HEADS, 2, MEM_HEAD).transpose(0, 2, 1, 3).reshape(d, 2 * width)
    kv_mem = _norm_matmul(mem2d, row(p['n_mem']), w_ckv.astype(BF16), tm=512, tn=1024)
    w_cq = (p['w_cq'] * MEM_HEAD ** -0.5).astype(BF16)
    x = _xattn(x, row(p['n_x_pre']), w_cq, kv_mem, p['w_co'].astype(BF16), row(p['n_x_post']),
               batch=batch, seq=seq, n_mem=n_mem)

    return ffn(x, p['n_ffn2_pre'], p['w_ffn2_gate'], p['w_ffn2_up'], p['w_ffn2_down'], p['n_ffn2_post'])


_PARAM_NAMES = (
    'n_ffn1_pre', 'n_ffn1_post', 'w_ffn1_gate', 'w_ffn1_up', 'w_ffn1_down',
    'n_mix_pre', 'n_mix_post', 'w_in', 'b_gate',
    'n_q_lat', 'w_uq', 'n_kv_lat', 'w_ukv', 'w_oa',
    'mu_shift', 'w0', 'w_w2', 'a0', 'w_a2', 'w_g2', 'k_k', 'k_a', 'r_k', 'lnx_w', 'lnx_b', 'w_ob',
    'w_o',
    'n_x_pre', 'n_x_post', 'n_mem', 'w_cq', 'w_ckv', 'w_co',
    'n_ffn2_pre', 'n_ffn2_post', 'w_ffn2_gate', 'w_ffn2_up', 'w_ffn2_down')


def kernel(x, mem, positions, n_ffn1_pre, n_ffn1_post, w_ffn1_gate, w_ffn1_up, w_ffn1_down, n_mix_pre, n_mix_post, w_in, b_gate, n_q_lat, w_uq, n_kv_lat, w_ukv, w_oa, mu_shift, w0, w_w2, a0, w_a2, w_g2, k_k, k_a, r_k, lnx_w, lnx_b, w_ob, w_o, n_x_pre, n_x_post, n_mem, w_cq, w_ckv, w_co, n_ffn2_pre, n_ffn2_post, w_ffn2_gate, w_ffn2_up, w_ffn2_down):
    stacked = (n_ffn1_pre, n_ffn1_post, w_ffn1_gate, w_ffn1_up, w_ffn1_down, n_mix_pre, n_mix_post, w_in, b_gate,
               n_q_lat, w_uq, n_kv_lat, w_ukv, w_oa, mu_shift, w0, w_w2, a0, w_a2, w_g2, k_k, k_a, r_k, lnx_w,
               lnx_b, w_ob, w_o, n_x_pre, n_x_post, n_mem, w_cq, w_ckv, w_co, n_ffn2_pre, n_ffn2_post,
               w_ffn2_gate, w_ffn2_up, w_ffn2_down)
    batch, seq, d = x.shape
    n_mem_tokens = mem.shape[1]
    x2d = x.reshape(batch * seq, d)
    mem2d = mem.reshape(batch * n_mem_tokens, d)
    for layer in range(n_ffn1_pre.shape[0]):
        p = {name: arr[layer] for name, arr in zip(_PARAM_NAMES, stacked)}
        x2d = _layer(x2d, mem2d, positions, p, batch=batch, seq=seq, n_mem=n_mem_tokens)
    return x2d.reshape(batch, seq, d)
```

```python
import functools

import jax
import jax.numpy as jnp
from jax import lax
from jax.experimental import pallas as pl
from jax.experimental.pallas import tpu as pltpu

F32 = jnp.float32
BF16 = jnp.bfloat16

D_MODEL = 2048
D_FF = 5504
CHUNK = 64
EPS = 1e-6

MLA_HEADS = 8
QK_NOPE = 128
QK_ROPE = 64
V_HEAD = 128
Q_LORA = 512
KV_LORA = 256
ROPE_THETA = 10000.0

RWKV_HEADS = 16
RWKV_HEAD = 64
RWKV_WIDTH = RWKV_HEADS * RWKV_HEAD
DECAY_LORA = 64
A_LORA = 64
GATE_LORA = 128
LNX_EPS = 64e-5
DECAY_SCALE = 0.6065306597126334

MEM_HEADS = 4
MEM_HEAD = 256

MLA_COLS = Q_LORA + KV_LORA + QK_ROPE
RWKV_COLS = 3 * RWKV_WIDTH + DECAY_LORA + A_LORA + GATE_LORA

LANES = 128
SUBLANES = 8
VMEM_LIMIT_BYTES = 56 * 1024 * 1024

SMALL_LORA = DECAY_LORA + A_LORA + GATE_LORA
COL_GATE_A = 0
COL_GATE_B = COL_GATE_A + D_MODEL
COL_R = COL_GATE_B + D_MODEL
COL_K = COL_R + RWKV_WIDTH
COL_V = COL_K + RWKV_WIDTH
COL_CQ = COL_V + RWKV_WIDTH
COL_CKV = COL_CQ + Q_LORA
COL_SMALL = COL_CKV + KV_LORA
COL_KROPE = COL_SMALL + SMALL_LORA
COL_KROPE_SWAP = COL_KROPE + 2 * QK_ROPE
IN_COLS = COL_KROPE_SWAP + 2 * QK_ROPE

NN = (((1,), (0,)), ((), ()))
NT = (((1,), (1,)), ((), ()))
TN = (((0,), (0,)), ((), ()))


def _dot(a, b, dims=NN):
    return lax.dot_general(a.astype(BF16), b.astype(BF16), dims, preferred_element_type=F32)


def _each(f, *cols):
    return [f(*xs) for xs in zip(*cols)]


def _split2(x):
    hi = x.astype(BF16)
    lo = (x - hi.astype(F32)).astype(BF16)
    return hi, lo


def _rms(xf, g, eps=EPS):
    return xf * lax.rsqrt(jnp.mean(xf * xf, axis=-1, keepdims=True) + eps) * g


def _sigmoid(x):
    return 1.0 / (1.0 + jnp.exp(-x))


def _params(*sem):
    return pltpu.CompilerParams(dimension_semantics=sem, vmem_limit_bytes=VMEM_LIMIT_BYTES)


def _resident(shape):
    return pl.BlockSpec(shape, lambda *_: (0,) * len(shape), pipeline_mode=pl.Buffered(1))


def _norm_matmul_kernel(x_ref, g_ref, w_ref, o_ref, h_ref):
    @pl.when(pl.program_id(1) == 0)
    def _():
        h_ref[...] = _rms(x_ref[...], g_ref[...]).astype(BF16)

    o_ref[...] = jnp.dot(h_ref[...], w_ref[...], preferred_element_type=F32).astype(o_ref.dtype)


def _norm_matmul(x, g, w, *, tm, tn):
    m, k_dim = x.shape
    n = w.shape[1]
    tm = min(tm, m)
    return pl.pallas_call(
        _norm_matmul_kernel,
        grid=(m // tm, n // tn),
        in_specs=[
            pl.BlockSpec((tm, k_dim), lambda i, j: (i, 0)),
            pl.BlockSpec((1, k_dim), lambda i, j: (0, 0)),
            pl.BlockSpec((k_dim, tn), lambda i, j: (0, j)),
        ],
        out_specs=pl.BlockSpec((tm, tn), lambda i, j: (i, j)),
        out_shape=jax.ShapeDtypeStruct((m, n), BF16),
        scratch_shapes=[pltpu.VMEM((tm, k_dim), BF16)],
        compiler_params=_params("parallel", "arbitrary"),
        name="norm_matmul",
    )(x, g, w)


IN_TN = 768
CQ_TILE, CQ_OFF = divmod(COL_CQ, IN_TN)
CKV_TILE, CKV_OFF = divmod(COL_CKV, IN_TN)
assert CQ_OFF + Q_LORA <= IN_TN and CKV_OFF + KV_LORA <= IN_TN and IN_COLS % IN_TN == 0
assert CQ_TILE > 0 and CKV_TILE > 0


def _in_proj_kernel(x_ref, g_ref, w_ref, gq_ref, wq_ref, gkv_ref, wkv_ref, o_ref, q_ref, kv_ref, h_ref):
    j = pl.program_id(1)

    def project(h):
        y = jnp.dot(h, w_ref[...], preferred_element_type=F32)
        o_ref[...] = y.astype(o_ref.dtype)
        return y

    def up_project(latent, gain_ref, weight_ref, out_ref):
        out_ref[...] = jnp.dot(_rms(latent, gain_ref[...]).astype(BF16), weight_ref[...],
                               preferred_element_type=F32).astype(out_ref.dtype)

    @pl.when(j == 0)
    def _():
        h = _rms(x_ref[...], g_ref[...]).astype(BF16)
        h_ref[...] = h
        project(h)

    @pl.when(j > 0)
    def _():
        y = project(h_ref[...])

        @pl.when(j == CQ_TILE)
        def _():
            up_project(y[:, CQ_OFF:CQ_OFF + Q_LORA], gq_ref, wq_ref, q_ref)

        @pl.when(j == CKV_TILE)
        def _():
            up_project(y[:, CKV_OFF:CKV_OFF + KV_LORA], gkv_ref, wkv_ref, kv_ref)


def _in_proj(x, g, w, g_q, w_uq, g_kv, w_ukv, *, tm=1024):
    m, d = x.shape
    tm = min(tm, m)
    nq, nkv = w_uq.shape[1], w_ukv.shape[1]
    return pl.pallas_call(
        _in_proj_kernel,
        grid=(m // tm, IN_COLS // IN_TN),
        in_specs=[
            pl.BlockSpec((tm, d), lambda i, j: (i, 0)),
            _resident((1, d)),
            pl.BlockSpec((d, IN_TN), lambda i, j: (0, j)),
            _resident((1, Q_LORA)), _resident((Q_LORA, nq)),
            _resident((1, KV_LORA)), _resident((KV_LORA, nkv)),
        ],
        out_specs=[pl.BlockSpec((tm, IN_TN), lambda i, j: (i, j)),
                   pl.BlockSpec((tm, nq), lambda i, j: (i, 0)),
                   pl.BlockSpec((tm, nkv), lambda i, j: (i, 0))],
        out_shape=[jax.ShapeDtypeStruct((m, IN_COLS), BF16),
                   jax.ShapeDtypeStruct((m, nq), BF16),
                   jax.ShapeDtypeStruct((m, nkv), BF16)],
        scratch_shapes=[pltpu.VMEM((tm, d), BF16)],
        compiler_params=_params("parallel", "arbitrary"),
        name="in_proj",
    )(x, g, w, g_q, w_uq, g_kv, w_ukv)


def _ffn_kernel(x_ref, gpre_ref, wg_ref, wu_ref, wd_ref, gpost_ref, o_ref, h_ref, acc_ref, *, tail):
    j = pl.program_id(1)
    last = pl.num_programs(1) - 1
    tf = wg_ref.shape[1]

    @pl.when(j == 0)
    def _():
        h_ref[...] = _rms(x_ref[...], gpre_ref[...]).astype(BF16)
        acc_ref[...] = jnp.zeros_like(acc_ref)

    def accumulate(valid):
        h = h_ref[...]
        gate = jnp.dot(h, wg_ref[:, :valid], preferred_element_type=F32)
        up = jnp.dot(h, wu_ref[:, :valid], preferred_element_type=F32)
        act = (gate * _sigmoid(gate) * up).astype(BF16)
        acc_ref[...] += jnp.dot(act, wd_ref[:valid, :], preferred_element_type=F32)

    if tail == tf:
        accumulate(tf)
    else:
        pl.when(j < last)(lambda: accumulate(tf))
        pl.when(j == last)(lambda: accumulate(tail))

    @pl.when(j == last)
    def _():
        o_ref[...] = x_ref[...] + 0.5 * _rms(acc_ref[...], gpost_ref[...])


def _ffn(x, g_pre, wg, wu, wd, g_post, *, tm=1024, tf=256):
    m, d = x.shape
    f = wg.shape[1]
    n_tiles = pl.cdiv(f, tf)
    tm = min(tm, m)
    tail = f - (n_tiles - 1) * tf
    assert tail % LANES == 0 and n_tiles >= 2
    return pl.pallas_call(
        functools.partial(_ffn_kernel, tail=tail),
        grid=(m // tm, n_tiles),
        in_specs=[
            pl.BlockSpec((tm, d), lambda i, j: (i, 0)),
            pl.BlockSpec((1, d), lambda i, j: (0, 0)),
            pl.BlockSpec((d, tf), lambda i, j: (0, j)),
            pl.BlockSpec((d, tf), lambda i, j: (0, j)),
            pl.BlockSpec((tf, d), lambda i, j: (j, 0)),
            pl.BlockSpec((1, d), lambda i, j: (0, 0)),
        ],
        out_specs=pl.BlockSpec((tm, d), lambda i, j: (i, 0)),
        out_shape=jax.ShapeDtypeStruct((m, d), F32),
        scratch_shapes=[pltpu.VMEM((tm, d), BF16), pltpu.VMEM((tm, d), F32)],
        compiler_params=_params("parallel", "arbitrary"),
        name="ffn",
    )(x, g_pre, wg, wu, wd, g_post)


def _mla_attn_kernel(qn_ref, qr_ref, qs_ref, cq_ref, sq_ref, kv_ref, kr_ref, ks_ref, ck_ref, sk_ref,
                     o_ref, krot_ref, *, tq, tk, n_heads):
    qi = pl.program_id(2)
    heads = list(range(n_heads))
    kv_width = QK_NOPE + V_HEAD

    @pl.when(qi == 0)
    def _():
        krot_ref[...] = (kr_ref[...].astype(F32) * ck_ref[...]
                         + ks_ref[...].astype(F32) * sk_ref[...]).astype(BF16)

    cq, sq = cq_ref[...], sq_ref[...]
    q_rot = [qr_ref[:, p * LANES:(p + 1) * LANES].astype(F32) * cq
             + qs_ref[:, p * LANES:(p + 1) * LANES].astype(F32) * sq for p in range(n_heads // 2)]
    lane = lax.broadcasted_iota(jnp.int32, (tq, LANES), 1)
    own_half = [lane < QK_ROPE, lane >= QK_ROPE]
    q_cat = [jnp.concatenate([qn_ref[:, h * QK_NOPE:(h + 1) * QK_NOPE],
                              jnp.where(own_half[h % 2], q_rot[h // 2], 0.0).astype(BF16)], axis=1)
             for h in heads]

    def scores(start):
        k_rot = krot_ref[pl.ds(start, tk), :]
        return _each(lambda h, q_: lax.dot_general(
            jnp.concatenate([kv_ref[pl.ds(start, tk), h * kv_width:h * kv_width + QK_NOPE], k_rot], axis=1), q_,
            NT, preferred_element_type=F32), heads, q_cat)

    def update(carry, s, start):
        m, l, acc = carry
        m_new = _each(lambda m_, s_: jnp.maximum(m_, jnp.max(s_, axis=0, keepdims=True)), m, s)
        alpha = _each(lambda m_, n_: jnp.exp(m_ - n_), m, m_new)
        p = _each(lambda s_, n_: jnp.exp(s_ - n_), s, m_new)
        l = _each(lambda a_, l_, p_: a_ * l_ + jnp.sum(p_, axis=0, keepdims=True), alpha, l, p)
        pv = _each(lambda h, p_: lax.dot_general(
            kv_ref[pl.ds(start, tk), h * kv_width + QK_NOPE:(h + 1) * kv_width], p_.astype(BF16), TN,
            preferred_element_type=F32), heads, p)
        acc = _each(lambda a_, acc_, pv_: a_ * acc_ + pv_, alpha, acc, pv)
        return m_new, l, acc

    def body(kj, carry):
        start = pl.multiple_of(kj * tk, tk)
        return update(carry, scores(start), start)

    init = ([jnp.full((1, tq), -1e30, F32) for _ in heads], [jnp.zeros((1, tq), F32) for _ in heads],
            [jnp.zeros((V_HEAD, tq), F32) for _ in heads])
    carry = lax.fori_loop(0, qi * (tq // tk), body, init)

    q_chunk = lax.broadcasted_iota(jnp.int32, (tk, tq), 1) // CHUNK
    for sub in range(tq // tk):
        start = pl.multiple_of(qi * tq + sub * tk, tk)
        k_chunk = (lax.broadcasted_iota(jnp.int32, (tk, tq), 0) + sub * tk) // CHUNK
        visible = k_chunk <= q_chunk
        s = _each(lambda s_: jnp.where(visible, s_, -1e30), scores(start))
        carry = update(carry, s, start)
    m, l, acc = carry
    for h in heads:
        o_ref[:, h * V_HEAD:(h + 1) * V_HEAD] = (acc[h] / l[h]).T.astype(o_ref.dtype)


def _mla_attention(q_up, kv_up, proj, cos_tab, sin_tab, *, batch, seq, tq=256, tk=256, n_heads=MLA_HEADS):
    nq = seq // tq
    rope_w = n_heads * QK_ROPE
    rope0 = MLA_HEADS * QK_NOPE // rope_w
    swap0 = (MLA_HEADS * QK_NOPE + MLA_HEADS * QK_ROPE) // rope_w
    q_rows = lambda width, col0: pl.BlockSpec((tq, width), lambda b, g, i: (b * nq + i, col0 + g))
    q_tab = pl.BlockSpec((tq, LANES), lambda b, g, i: (b * nq + i, 0))
    k_tab = pl.BlockSpec((seq, LANES), lambda b, g, i: (b, 0))
    k_cols = lambda col: pl.BlockSpec((seq, LANES), lambda b, g, i: (b, col // LANES))
    return pl.pallas_call(
        functools.partial(_mla_attn_kernel, tq=tq, tk=tk, n_heads=n_heads),
        grid=(batch, MLA_HEADS // n_heads, nq),
        in_specs=[
            q_rows(n_heads * QK_NOPE, 0), q_rows(rope_w, rope0), q_rows(rope_w, swap0), q_tab, q_tab,
            pl.BlockSpec((seq, n_heads * (QK_NOPE + V_HEAD)), lambda b, g, i: (b, g)),
            k_cols(COL_KROPE), k_cols(COL_KROPE_SWAP), k_tab, k_tab,
        ],
        out_specs=pl.BlockSpec((tq, n_heads * V_HEAD), lambda b, g, i: (b * nq + i, g)),
        out_shape=jax.ShapeDtypeStruct((batch * seq, MLA_HEADS * V_HEAD), BF16),
        scratch_shapes=[pltpu.VMEM((seq, LANES), BF16)],
        compiler_params=_params("parallel", "parallel", "arbitrary"),
        name="mla_attention",
    )(q_up, q_up, q_up, cos_tab, sin_tab, kv_up, proj, proj, cos_tab, sin_tab)


def _head_sum(x, e_ref, et_ref):
    s = _dot(x, e_ref[...])
    return _dot(s, et_ref[...])


def _shift(y, prev_row, mu, first):
    rows = lax.broadcasted_iota(jnp.int32, y.shape, 0)
    prev0 = jnp.where(first, 0.0, prev_row)
    y_prev = jnp.where(rows == 0, prev0, pltpu.roll(y, 1, 0))
    return y + (y_prev - y) * mu


def _rwkv_prep_kernel(r_ref, k_ref, v_ref, s_ref, rp_ref, kp_ref, vp_ref, sp_ref,
                      mu_rkv_ref, mu_s_ref, w0_ref, ww2_ref, a0_ref, wa2_ref, wg2_ref,
                      kk_ref, ka_ref, e_ref, et_ref,
                      ro_ref, lw_ref, ko_ref, vo_ref, ao_ref, bo_ref, go_ref, *, tiles_per_seq):
    first = (pl.program_id(0) % tiles_per_seq) == 0
    def shifted(ref, pref, mu):
        return _shift(ref[...].astype(F32), pref[SUBLANES - 1:SUBLANES, :].astype(F32), mu, first)

    r = shifted(r_ref, rp_ref, mu_rkv_ref[0:1, :])
    k = shifted(k_ref, kp_ref, mu_rkv_ref[1:2, :])
    v = shifted(v_ref, vp_ref, mu_rkv_ref[2:3, :])
    small = shifted(s_ref, sp_ref, mu_s_ref[...])
    dw = small[:, 0:DECAY_LORA]
    da = small[:, DECAY_LORA:DECAY_LORA + A_LORA]
    dg = small[:, DECAY_LORA + A_LORA:]

    u = w0_ref[...] + _dot(jnp.tanh(dw), ww2_ref[...])
    lw_ref[...] = -DECAY_SCALE * _sigmoid(u)
    a = _sigmoid(a0_ref[...] + _dot(da, wa2_ref[...]))
    go_ref[...] = _dot(_sigmoid(dg), wg2_ref[...]).astype(go_ref.dtype)

    kk = k * kk_ref[...]
    kk = kk * lax.rsqrt(jnp.maximum(_head_sum(kk * kk, e_ref, et_ref), 1e-24))
    ro_ref[...] = r.astype(ro_ref.dtype)
    ko_ref[...] = (k * (1.0 + (a - 1.0) * ka_ref[...])).astype(ko_ref.dtype)
    vo_ref[...] = v.astype(vo_ref.dtype)
    ao_ref[...] = (-kk).astype(ao_ref.dtype)
    bo_ref[...] = (kk * a).astype(bo_ref.dtype)


def _rwkv_prep(proj, mu_rkv, mu_s, w0, ww2, a0, wa2, wg2, k_k, k_a, e, et, *, seq, tm=512):
    t = proj.shape[0]
    c = RWKV_WIDTH
    small_w = SMALL_LORA
    tm = min(tm, seq)

    def cur(width, col):
        return pl.BlockSpec((tm, width), lambda i: (i, col // width))

    def prev(width, col):
        return pl.BlockSpec((SUBLANES, width),
                            lambda i: (jnp.maximum(i * (tm // SUBLANES) - 1, 0), col // width))

    def const(shape):
        return pl.BlockSpec(shape, lambda i: (0, 0))

    out = jax.ShapeDtypeStruct((t, c), BF16)
    return pl.pallas_call(
        functools.partial(_rwkv_prep_kernel, tiles_per_seq=seq // tm),
        grid=(t // tm,),
        in_specs=[cur(c, COL_R), cur(c, COL_K), cur(c, COL_V), cur(small_w, COL_SMALL),
                  prev(c, COL_R), prev(c, COL_K), prev(c, COL_V), prev(small_w, COL_SMALL),
                  const((3, c)), const((1, small_w)), const((1, c)), const((DECAY_LORA, c)),
                  const((1, c)), const((A_LORA, c)), const((GATE_LORA, c)),
                  const((1, c)), const((1, c)), const((c, LANES)), const((LANES, c))],
        out_specs=[pl.BlockSpec((tm, c), lambda i: (i, 0))] * 7,
        out_shape=[out, jax.ShapeDtypeStruct((t, c), F32), out, out, out, out, out],
        compiler_params=_params("parallel"),
        name="rwkv_prep",
    )(proj, proj, proj, proj, proj, proj, proj, proj,
      mu_rkv, mu_s, w0, ww2, a0, wa2, wg2, k_k, k_a, e, et)


def _block_diag(x):
    lane = lax.broadcasted_iota(jnp.int32, x.shape, 1)
    zero = jnp.zeros_like(x)
    return jnp.concatenate([jnp.where(lane < RWKV_HEAD, x, zero),
                            jnp.where(lane >= RWKV_HEAD, x, zero)], axis=0)


def _unit_lower_inverse(a, eye, level_masks):
    c = CHUNK
    base = _each(lambda x: x * level_masks[0], a)
    x2 = _each(lambda x: _dot(x, _block_diag(x)), base)
    t = _each(lambda x: eye + x, base)
    tx = _each(lambda t_, x_: _dot(jnp.concatenate([t_, x_], axis=0), _block_diag(x_)), t, x2)
    t = _each(lambda t_, tx_: t_ + tx_[:c], t, tx)
    t = _each(lambda t_, tx_: t_ + _dot(t_, _block_diag(tx_[c:])), t, tx)
    for mask in level_masks[1:]:
        ta = _each(lambda t_, a_: _dot(t_, _block_diag(a_ * mask)), t, a)
        t = _each(lambda t_, ta_: t_ + _dot(ta_, _block_diag(t_)), t, ta)
    return t


def _chunk_prepare(r, lw, k, v, a, b, consts):
    ltri, strict, incl, eye, _, level_masks = consts
    c = CHUNK
    rows = lambda x, y: jnp.concatenate([x, y], axis=0)

    def cumulative(x):
        both = jnp.dot(ltri, jnp.concatenate(_split2(x), axis=1), preferred_element_type=F32)
        return both[:, :x.shape[1]] + both[:, x.shape[1]:]

    cl = _each(cumulative, lw)
    cl_end = _each(lambda x: x[c - 1:c, :], cl)
    p_inv = _each(lambda x: jnp.exp(-x), cl)
    a_t = _each(lambda a_, cl_, lw_: a_ * jnp.exp(cl_ - lw_), a, cl, lw)
    r_t = _each(lambda r_, cl_: r_ * jnp.exp(cl_), r, cl)
    b_t = _each(jnp.multiply, b, p_inv)
    k_t = _each(jnp.multiply, k, p_inv)

    ar = _each(rows, a_t, r_t)
    g = _each(lambda x, b_, k_: _dot(x, rows(_block_diag(b_), _block_diag(k_)), NT), ar, b_t, k_t)
    width = b_t[0].shape[1]
    a_ab = _each(lambda g_: jnp.where(strict, g_[:c, :width], 0.0), g)
    a_rb = _each(lambda g_: jnp.where(incl, g_[c:, :width], 0.0), g)
    a_k = _each(lambda g_: rows(jnp.where(strict, g_[:c, width:], 0.0), jnp.where(incl, g_[c:, width:], 0.0)), g)

    t_inv = _unit_lower_inverse(a_ab, eye, level_masks)

    av = _each(lambda x, v_: _dot(x, _block_diag(v_)), a_k, v)
    tw = _each(lambda t_, a_, av_: _dot(t_, jnp.concatenate([_block_diag(a_), _block_diag(av_[:c])], axis=1)),
               t_inv, a_t, av)
    lhs = _each(lambda tw_, r_: rows(tw_[:, :LANES], r_).astype(BF16), tw, r_t)
    w_eff = _each(lambda tw_: tw_[:, LANES:], tw)
    o_local = _each(lambda av_: av_[c:], av)
    p_tail = _each(lambda e_, cl_: jnp.exp(e_ - cl_), cl_end, cl)
    bk_tail = _each(lambda b_, k_, p_: rows(b_ * p_, k_ * p_).astype(BF16), b, k, p_tail)
    p_end = _each(jnp.exp, cl_end)
    return lhs, w_eff, a_rb, o_local, v, bk_tail, p_end


def _chunk_apply(prepared, state, diag_blocks):
    c = CHUNK
    lhs, w_eff, a_rb, o_local, v, bk_tail, p_end = prepared
    uo = _each(lambda lhs_, s_: _dot(lhs_, s_, NT), lhs, state)
    u = _each(lambda uo_, w_: uo_[:c] + w_, uo, w_eff)
    o = _each(lambda uo_, arb_, u_, ol_: uo_[c:] + _dot(arb_, _block_diag(u_)) + ol_, uo, a_rb, u, o_local)
    upd = _each(lambda u_, v_, bk_: _dot(jnp.concatenate([u_, v_], axis=0), bk_, TN), u, v, bk_tail)
    new_state = _each(lambda s_, p_, upd_: s_ * p_ + jnp.where(diag_blocks, upd_, 0.0), state, p_end, upd)
    return o, new_state


def _rwkv_kernel(r_ref, lw_ref, k_ref, v_ref, a_ref, b_ref, o_ref, s_ref, *, n_chunks, n_pairs, unroll):
    @pl.when(pl.program_id(2) == 0)
    def _():
        s_ref[...] = jnp.zeros_like(s_ref)

    c = CHUNK
    row = lax.broadcasted_iota(jnp.int32, (c, LANES), 0)
    col = lax.broadcasted_iota(jnp.int32, (c, LANES), 1) % RWKV_HEAD
    ltri = jnp.where(lax.broadcasted_iota(jnp.int32, (c, c), 1) <= lax.broadcasted_iota(jnp.int32, (c, c), 0),
                     1.0, 0.0).astype(BF16)
    rr = lax.broadcasted_iota(jnp.int32, (LANES, LANES), 0) // RWKV_HEAD
    cc = lax.broadcasted_iota(jnp.int32, (LANES, LANES), 1) // RWKV_HEAD
    def same(block):
        return (row // block) == (col // block)
    level_masks = [jnp.where(same(8), 1.0, 0.0)]
    for block in (8, 16, 32):
        level_masks.append(jnp.where(same(2 * block), 1.0, 0.0) - jnp.where(same(block), 1.0, 0.0))
    eye = jnp.where(col == row, 1.0, 0.0)
    consts = (ltri, col < row, col <= row, eye, rr == cc, level_masks)

    def chunk_group(gi, carry):
        r0 = pl.multiple_of(gi * (unroll * c), unroll * c)
        tiles = [(pl.ds(r0 + u * c, c), slice(p * LANES, (p + 1) * LANES))
                 for u in range(unroll) for p in range(n_pairs)]
        load = lambda ref: [ref[sl].astype(F32) for sl in tiles]
        prepared = _chunk_prepare(load(r_ref), load(lw_ref), load(k_ref), load(v_ref), load(a_ref), load(b_ref),
                                  consts)
        state = [s_ref[p] for p in range(n_pairs)]
        for u in range(unroll):
            part = slice(u * n_pairs, (u + 1) * n_pairs)
            o, state = _chunk_apply([x[part] for x in prepared], state, consts[4])
            for p in range(n_pairs):
                o_ref[tiles[u * n_pairs + p]] = o[p]
        for p in range(n_pairs):
            s_ref[p] = state[p]
        return carry

    lax.fori_loop(0, n_chunks // unroll, chunk_group, 0)


def _rwkv_recurrence(r, lw, k, v, a, b, *, batch, seq, tm=512, n_pairs=8, unroll=4):
    t, c = r.shape
    width = n_pairs * LANES
    tm = min(tm, seq)
    nt = seq // tm
    assert (tm // CHUNK) % unroll == 0
    spec = pl.BlockSpec((tm, width), lambda bi, g, i: (bi * nt + i, g))
    return pl.pallas_call(
        functools.partial(_rwkv_kernel, n_chunks=tm // CHUNK, n_pairs=n_pairs, unroll=unroll),
        grid=(batch, c // width, nt),
        in_specs=[spec] * 6,
        out_specs=spec,
        out_shape=jax.ShapeDtypeStruct((t, c), F32),
        scratch_shapes=[pltpu.VMEM((n_pairs, LANES, LANES), F32)],
        compiler_params=_params("parallel", "parallel", "arbitrary"),
        name="rwkv_recurrence",
    )(r, lw, k, v, a, b)


def _mix_tail_kernel(x_ref, ga_ref, gb_ref, bg_ref, oa_ref, woa_ref,
                     o_ref, r_ref, k_ref, v_ref, g_ref, rk_ref, lw_ref, lb_ref, e_ref, et_ref,
                     wob_ref, wo_ref, gpost_ref, out_ref):
    inv_n = 1.0 / RWKV_HEAD
    o = o_ref[...]
    mean = _head_sum(o, e_ref, et_ref) * inv_n
    cen = o - mean
    var = _head_sum(cen * cen, e_ref, et_ref) * inv_n
    on = cen * lax.rsqrt(var + LNX_EPS) * lw_ref[...] + lb_ref[...]
    v = v_ref[...].astype(F32)
    rk = r_ref[...].astype(F32) * k_ref[...].astype(F32) * rk_ref[...]
    bonus = _head_sum(rk, e_ref, et_ref) * v
    ob = ((on + bonus) * g_ref[...].astype(F32)).astype(BF16)

    y_a = jnp.dot(oa_ref[...], woa_ref[...], preferred_element_type=F32)
    y_b = jnp.dot(ob, wob_ref[...], preferred_element_type=F32)
    gate_a = _sigmoid(ga_ref[...].astype(F32) + bg_ref[0:1, :])
    gate_b = _sigmoid(gb_ref[...].astype(F32) + bg_ref[1:2, :])
    merged = (gate_a * y_a + gate_b * y_b).astype(BF16)
    z = jnp.dot(merged, wo_ref[...], preferred_element_type=F32)
    out_ref[...] = x_ref[...] + _rms(z, gpost_ref[...])


def _mix_tail(x, proj, b_gate, o_a, w_oa, o_raw, r, k, v, g, r_k, lnx_w, lnx_b, e, et, w_ob, w_o, g_post, *, tm=256):
    t, d = x.shape
    c = RWKV_WIDTH
    row = lambda width, col=0: pl.BlockSpec((tm, width), lambda i: (i, col // width))
    return pl.pallas_call(
        _mix_tail_kernel,
        grid=(t // tm,),
        in_specs=[row(d), row(d, COL_GATE_A), row(d, COL_GATE_B), _resident((2, d)),
                  row(c), _resident((c, d)),
                  row(c), row(c), row(c), row(c), row(c),
                  _resident((1, c)), _resident((1, c)), _resident((1, c)),
                  _resident((c, LANES)), _resident((LANES, c)),
                  _resident((c, d)), _resident((d, d)), _resident((1, d))],
        out_specs=row(d),
        out_shape=jax.ShapeDtypeStruct((t, d), F32),
        compiler_params=_params("parallel"),
        name="mix_tail",
    )(x, proj, proj, b_gate, o_a, w_oa, o_raw, r, k, v, g, r_k, lnx_w, lnx_b, e, et, w_ob, w_o, g_post)


def _xattn_kernel(x_ref, gpre_ref, wq_ref, kv_ref, wo_ref, gpost_ref, out_ref):
    x = x_ref[...]
    h = _rms(x, gpre_ref[...]).astype(BF16)
    q = jnp.dot(h, wq_ref[...], preferred_element_type=F32).astype(BF16)
    width = MEM_HEADS * MEM_HEAD
    heads = []
    for hd in range(MEM_HEADS):
        lo = hd * MEM_HEAD
        k = kv_ref[:, lo:lo + MEM_HEAD]
        v = kv_ref[:, width + lo:width + lo + MEM_HEAD]
        s = lax.dot_general(q[:, lo:lo + MEM_HEAD], k, NT, preferred_element_type=F32)
        p = jnp.exp(s - jnp.max(s, axis=-1, keepdims=True))
        p = p / jnp.sum(p, axis=-1, keepdims=True)
        heads.append(jnp.dot(p.astype(BF16), v, preferred_element_type=F32).astype(BF16))
    o = jnp.concatenate(heads, axis=1)
    z = jnp.dot(o, wo_ref[...], preferred_element_type=F32)
    out_ref[...] = x + _rms(z, gpost_ref[...])


def _xattn(x, g_pre, w_cq, kv_mem, w_co, g_post, *, batch, seq, n_mem, tq=512):
    t, d = x.shape
    tq = min(tq, seq)
    nq = seq // tq
    width = MEM_HEADS * MEM_HEAD
    return pl.pallas_call(
        _xattn_kernel,
        grid=(batch, nq),
        in_specs=[pl.BlockSpec((tq, d), lambda b, i: (b * nq + i, 0)),
                  _resident((1, d)), _resident((d, width)),
                  pl.BlockSpec((n_mem, 2 * width), lambda b, i: (b, 0)),
                  _resident((width, d)), _resident((1, d))],
        out_specs=pl.BlockSpec((tq, d), lambda b, i: (b * nq + i, 0)),
        out_shape=jax.ShapeDtypeStruct((t, d), F32),
        compiler_params=_params("parallel", "parallel"),
        name="xattn",
    )(x, g_pre, w_cq, kv_mem, w_co, g_post)


def _rope_tables(positions):
    half = QK_ROPE // 2
    inv = ROPE_THETA ** (-jnp.arange(half, dtype=F32) / half)
    reps = LANES // half
    sign = jnp.tile(jnp.concatenate([-jnp.ones(half, F32), jnp.ones(half, F32)]), reps // 2)
    ang = positions.astype(F32).reshape(-1, 1) * jnp.tile(inv, reps)
    return jnp.cos(ang), jnp.sin(ang) * sign


def _layer(x, mem2d, positions, p, *, batch, seq, n_mem):
    row = lambda a: a.reshape(1, -1).astype(F32)
    c = RWKV_WIDTH
    d = D_MODEL

    def ffn(x, pre, gate, up, down, post):
        return _ffn(x, row(pre), gate.astype(BF16), up.astype(BF16), down.astype(BF16), row(post))

    x = ffn(x, p['n_ffn1_pre'], p['w_ffn1_gate'], p['w_ffn1_up'], p['w_ffn1_down'], p['n_ffn1_post'])

    w_in = p['w_in']
    o_rwkv = MLA_COLS
    o_gate = MLA_COLS + RWKV_COLS
    half = QK_ROPE // 2
    w_kr = w_in[:, Q_LORA + KV_LORA:MLA_COLS]
    w_kr_swap = jnp.concatenate([w_kr[:, half:], w_kr[:, :half]], axis=1)
    w_in_r = jnp.concatenate([
        w_in[:, o_gate:],
        w_in[:, o_rwkv:o_rwkv + 3 * c],
        w_in[:, :Q_LORA + KV_LORA],
        w_in[:, o_rwkv + 3 * c:o_gate],
        w_kr, w_kr, w_kr_swap, w_kr_swap,
    ], axis=1).astype(BF16)
    scale = (QK_NOPE + QK_ROPE) ** -0.5
    w_uq = p['w_uq'].reshape(Q_LORA, MLA_HEADS, QK_NOPE + QK_ROPE) * scale
    w_qr = w_uq[:, :, QK_NOPE:]
    w_qr_swap = jnp.concatenate([w_qr[:, :, half:], w_qr[:, :, :half]], axis=2)
    w_uq_r = jnp.concatenate([w_uq[:, :, :QK_NOPE].reshape(Q_LORA, -1), w_qr.reshape(Q_LORA, -1),
                              w_qr_swap.reshape(Q_LORA, -1)], axis=1).astype(BF16)
    proj, q_up, kv_up = _in_proj(x, row(p['n_mix_pre']), w_in_r, row(p['n_q_lat']), w_uq_r,
                                 row(p['n_kv_lat']), p['w_ukv'].astype(BF16))

    cos_tab, sin_tab = _rope_tables(positions)
    o_a = _mla_attention(q_up, kv_up, proj, cos_tab, sin_tab, batch=batch, seq=seq)

    head_of = jnp.arange(c) // RWKV_HEAD
    e = (head_of[:, None] == jnp.arange(LANES)[None, :]).astype(BF16)
    mu = p['mu_shift']
    r, lw, k, v, a_in, b_in, g = _rwkv_prep(
        proj, mu[:3 * c].reshape(3, c), row(mu[3 * c:]), row(p['w0']), p['w_w2'].astype(BF16),
        row(p['a0']), p['w_a2'].astype(BF16), p['w_g2'].astype(BF16), row(p['k_k']), row(p['k_a']),
        e, e.T, seq=seq)
    o_raw = _rwkv_recurrence(r, lw, k, v, a_in, b_in, batch=batch, seq=seq)

    x = _mix_tail(x, proj, p['b_gate'].reshape(2, d), o_a, p['w_oa'].astype(BF16), o_raw, r, k, v, g,
                  row(p['r_k']), row(p['lnx_w']), row(p['lnx_b']), e, e.T,
                  p['w_ob'].astype(BF16), p['w_o'].astype(BF16), row(p['n_mix_post']))

    width = MEM_HEADS * MEM_HEAD
    w_ckv = p['w_ckv'].reshape(d, MEM_HEADS, 2, MEM_HEAD).transpose(0, 2, 1, 3).reshape(d, 2 * width)
    kv_mem = _norm_matmul(mem2d, row(p['n_mem']), w_ckv.astype(BF16), tm=512, tn=1024)
    w_cq = (p['w_cq'] * MEM_HEAD ** -0.5).astype(BF16)
    x = _xattn(x, row(p['n_x_pre']), w_cq, kv_mem, p['w_co'].astype(BF16), row(p['n_x_post']),
               batch=batch, seq=seq, n_mem=n_mem)

    return ffn(x, p['n_ffn2_pre'], p['w_ffn2_gate'], p['w_ffn2_up'], p['w_ffn2_down'], p['n_ffn2_post'])


_PARAM_NAMES = (
    'n_ffn1_pre', 'n_ffn1_post', 'w_ffn1_gate', 'w_ffn1_up', 'w_ffn1_down',
    'n_mix_pre', 'n_mix_post', 'w_in', 'b_gate',
    'n_q_lat', 'w_uq', 'n_kv_lat', 'w_ukv', 'w_oa',
    'mu_shift', 'w0', 'w_w2', 'a0', 'w_a2', 'w_g2', 'k_k', 'k_a', 'r_k', 'lnx_w', 'lnx_b', 'w_ob',
    'w_o',
    'n_x_pre', 'n_x_post', 'n_mem', 'w_cq', 'w_ckv', 'w_co',
    'n_ffn2_pre', 'n_ffn2_post', 'w_ffn2_gate', 'w_ffn2_up', 'w_ffn2_down')


def kernel(x, mem, positions, n_ffn1_pre, n_ffn1_post, w_ffn1_gate, w_ffn1_up, w_ffn1_down, n_mix_pre, n_mix_post, w_in, b_gate, n_q_lat, w_uq, n_kv_lat, w_ukv, w_oa, mu_shift, w0, w_w2, a0, w_a2, w_g2, k_k, k_a, r_k, lnx_w, lnx_b, w_ob, w_o, n_x_pre, n_x_post, n_mem, w_cq, w_ckv, w_co, n_ffn2_pre, n_ffn2_post, w_ffn2_gate, w_ffn2_up, w_ffn2_down):
    stacked = (n_ffn1_pre, n_ffn1_post, w_ffn1_gate, w_ffn1_up, w_ffn1_down, n_mix_pre, n_mix_post, w_in, b_gate,
               n_q_lat, w_uq, n_kv_lat, w_ukv, w_oa, mu_shift, w0, w_w2, a0, w_a2, w_g2, k_k, k_a, r_k, lnx_w,
               lnx_b, w_ob, w_o, n_x_pre, n_x_post, n_mem, w_cq, w_ckv, w_co, n_ffn2_pre, n_ffn2_post,
               w_ffn2_gate, w_ffn2_up, w_ffn2_down)
    batch, seq, d = x.shape
    n_mem_tokens = mem.shape[1]
    x2d = x.reshape(batch * seq, d)
    mem2d = mem.reshape(batch * n_mem_tokens, d)
    for layer in range(n_ffn1_pre.shape[0]):
        p = {name: arr[layer] for name, arr in zip(_PARAM_NAMES, stacked)}
        x2d = _layer(x2d, mem2d, positions, p, batch=batch, seq=seq, n_mem=n_mem_tokens)
    return x2d.reshape(batch, seq, d)
```
